```python
import math
import jax, jax.numpy as jnp
from jax import lax
import numpy as np

D_MODEL = 2048
BATCH = 8
SEQ = 2048
DEPTH = 4

N_MIXERS = 2
N_NSA_LAYERS = (DEPTH + N_MIXERS - 1) // N_MIXERS
N_MLSTM_LAYERS = DEPTH // N_MIXERS

D_FF = 5632
FFN_RES = 0.5
N_MOD = 9
RMS_EPS = 1e-6
NEG_INF = -1e30

NSA_HEADS = 16
NSA_HEAD_DIM = 128
NSA_KV_GROUPS = 4
NSA_HPG = NSA_HEADS // NSA_KV_GROUPS
NSA_Q_DIM = NSA_HEADS * NSA_HEAD_DIM
NSA_KV_DIM = NSA_KV_GROUPS * NSA_HEAD_DIM
NSA_IN_DIM = NSA_Q_DIM + 6 * NSA_KV_DIM + 3 * NSA_HEADS
CMP_BLOCK = 32
CMP_STRIDE = 16
CMP_HIDDEN = 512
SEL_BLOCK = 64
SEL_TOPK = 16
N_LOCAL_BLOCKS = 2
FORCE_SCORE = 1e9
WINDOW = 512
Q_BLOCK = 128
SEL_Q_BLOCK = 32
ROPE_THETA = 10000.0

ML_HEADS = 8
ML_QK_DIM = 128
ML_V_DIM = 256
ML_CONV = 4
ML_CHUNK = 64
ML_QK_COLS = 2 * ML_HEADS * ML_QK_DIM
ML_IN_DIM = ML_QK_COLS + 2 * ML_HEADS * ML_V_DIM + 2 * ML_HEADS

kernel_name = 'hybrid_nsa_mlstm_macaron_adaln'


def rms_norm(x, g):
    xf = x.astype(jnp.float32)
    y = xf * lax.rsqrt(jnp.mean(xf * xf, axis=-1, keepdims=True) + RMS_EPS)
    return (y * g.astype(jnp.float32)).astype(x.dtype)


def rope(x, pos):
    half = x.shape[-1] // 2
    freqs = ROPE_THETA ** (-jnp.arange(half, dtype=jnp.float32) / half)
    ang = pos.astype(jnp.float32)[:, None] * freqs[None, :]
    cos, sin = jnp.cos(ang), jnp.sin(ang)
    xf = x.astype(jnp.float32)
    x1, x2 = xf[..., :half], xf[..., half:]
    return jnp.concatenate([x1 * cos - x2 * sin, x2 * cos + x1 * sin], axis=-1).astype(x.dtype)


def masked_softmax(s, mask):
    s = jnp.where(mask, s.astype(jnp.float32), NEG_INF)
    p = jax.nn.softmax(s, axis=-1)
    return jnp.where(mask, p, 0.0)


def swiglu(h, w_in, w_out):
    g, u = jnp.split(h @ w_in, 2, axis=-1)
    return (jax.nn.silu(g) * u) @ w_out


def nsa_mixer(h, w_in, cmp_pos, cmp_w1, cmp_w2, w_out):
    B, S, _ = h.shape
    G, R, Dh = NSA_KV_GROUPS, NSA_HPG, NSA_HEAD_DIM
    scale = Dh ** -0.5
    sizes = [NSA_Q_DIM] + [NSA_KV_DIM] * 6 + [3 * NSA_HEADS]
    cuts = np.cumsum(sizes)[:-1].tolist()
    q, k_c, v_c, k_s, v_s, k_w, v_w, gate_logits = jnp.split(h @ w_in, cuts, axis=-1)
    q = q.reshape(B, S, G, R, Dh).transpose(0, 2, 3, 1, 4)

    def kv(t):
        return t.reshape(B, S, G, Dh).transpose(0, 2, 1, 3)

    k_c, v_c, k_s, v_s, k_w, v_w = kv(k_c), kv(v_c), kv(k_s), kv(v_s), kv(k_w), kv(v_w)
    pos = jnp.arange(S)
    q_rot = rope(q, pos)
    k_s = rope(k_s, pos)
    k_w = rope(k_w, pos)

    n_cmp = (S - CMP_BLOCK) // CMP_STRIDE + 1
    cmp_start = jnp.arange(n_cmp) * CMP_STRIDE
    blk_idx = cmp_start[:, None] + jnp.arange(CMP_BLOCK)[None, :]

    def compress(t, pe, w1, w2):
        blocks = t[:, :, blk_idx, :] + pe
        flat = blocks.reshape(B, G, n_cmp, CMP_BLOCK * Dh)
        return jax.nn.silu(flat @ w1) @ w2

    kc = compress(k_c, cmp_pos[0], cmp_w1[0], cmp_w2[0])
    vc = compress(v_c, cmp_pos[1], cmp_w1[1], cmp_w2[1])
    cmp_end = cmp_start + CMP_BLOCK - 1
    s_c = jnp.einsum('bgrqd,bgcd->bgrqc', q, kc) * scale
    p_c = masked_softmax(s_c, cmp_end[None, :] <= pos[:, None])
    o_cmp = jnp.einsum('bgrqc,bgcd->bgrqd', p_c.astype(vc.dtype), vc)

    n_sel = S // SEL_BLOCK
    jsel = jnp.arange(n_sel)
    cover = ((cmp_start[:, None] < (jsel[None, :] + 1) * SEL_BLOCK) &
             (cmp_start[:, None] + CMP_BLOCK > jsel[None, :] * SEL_BLOCK)).astype(jnp.float32)
    imp = jnp.einsum('bgrqc,cj->bgqj', p_c, cover)
    q_blk = pos // SEL_BLOCK
    causal_blk = jsel[None, :] <= q_blk[:, None]
    forced = (jsel[None, :] == 0) | (causal_blk & (jsel[None, :] > q_blk[:, None] - N_LOCAL_BLOCKS))
    imp = jnp.where(forced, FORCE_SCORE, jnp.where(causal_blk, imp, -1.0))
    top_k = min(SEL_TOPK, n_sel)
    top_val, top_idx = lax.top_k(imp, top_k)
    top_ok = top_val >= 0.0

    ks_blocks = k_s.reshape(B, G, n_sel, SEL_BLOCK, Dh)
    vs_blocks = v_s.reshape(B, G, n_sel, SEL_BLOCK, Dh)
    nq = S // SEL_Q_BLOCK
    q_chunks = jnp.moveaxis(q_rot.reshape(B, G, R, nq, SEL_Q_BLOCK, Dh), 3, 0)
    idx_chunks = jnp.moveaxis(top_idx.reshape(B, G, nq, SEL_Q_BLOCK, top_k), 2, 0)
    ok_chunks = jnp.moveaxis(top_ok.reshape(B, G, nq, SEL_Q_BLOCK, top_k), 2, 0)
    pos_chunks = pos.reshape(nq, SEL_Q_BLOCK)
    b_ix = jnp.arange(B)[:, None, None, None]
    g_ix = jnp.arange(G)[None, :, None, None]
    tok_in_blk = jnp.arange(SEL_BLOCK)
    n_keys = top_k * SEL_BLOCK

    def sel_chunk(args):
        qc, idx, ok, tq = args
        kg = ks_blocks[b_ix, g_ix, idx]
        vg = vs_blocks[b_ix, g_ix, idx]
        s = jnp.einsum('bgrqd,bgqkld->bgrqkl', qc, kg) * scale
        key_pos = idx[..., None] * SEL_BLOCK + tok_in_blk
        mask = ok[..., None] & (key_pos <= tq[None, None, :, None, None])
        p = masked_softmax(s.reshape(B, G, R, SEL_Q_BLOCK, n_keys),
                           mask.reshape(B, G, 1, SEL_Q_BLOCK, n_keys))
        p = p.reshape(B, G, R, SEL_Q_BLOCK, top_k, SEL_BLOCK).astype(vg.dtype)
        return jnp.einsum('bgrqkl,bgqkld->bgrqd', p, vg)

    o_sel = lax.map(sel_chunk, (q_chunks, idx_chunks, ok_chunks, pos_chunks))
    o_sel = jnp.moveaxis(o_sel, 0, 3).reshape(B, G, R, S, Dh)

    nb = S // Q_BLOCK
    span = WINDOW + Q_BLOCK
    kw_pad = jnp.pad(k_w, ((0, 0), (0, 0), (WINDOW, 0), (0, 0)))
    vw_pad = jnp.pad(v_w, ((0, 0), (0, 0), (WINDOW, 0), (0, 0)))
    qw_chunks = jnp.moveaxis(q_rot.reshape(B, G, R, nb, Q_BLOCK, Dh), 3, 0)
    offs = jnp.arange(span)
    q_offs = jnp.arange(Q_BLOCK)

    def win_chunk(args):
        qc, blk = args
        start = blk * Q_BLOCK
        kb = lax.dynamic_slice_in_dim(kw_pad, start, span, axis=2)
        vb = lax.dynamic_slice_in_dim(vw_pad, start, span, axis=2)
        s = jnp.einsum('bgrqd,bgkd->bgrqk', qc, kb) * scale
        tq = start + q_offs
        kp = start - WINDOW + offs
        mask = (kp[None, :] >= 0) & (kp[None, :] <= tq[:, None]) & (tq[:, None] - kp[None, :] < WINDOW)
        p = masked_softmax(s, mask).astype(vb.dtype)
        return jnp.einsum('bgrqk,bgkd->bgrqd', p, vb)

    o_win = lax.map(win_chunk, (qw_chunks, jnp.arange(nb)))
    o_win = jnp.moveaxis(o_win, 0, 3).reshape(B, G, R, S, Dh)

    gates = jax.nn.sigmoid(gate_logits.reshape(B, S, G, R, 3).transpose(0, 2, 3, 1, 4))
    o = gates[..., 0:1] * o_cmp + gates[..., 1:2] * o_sel + gates[..., 2:3] * o_win
    o = o.transpose(0, 3, 1, 2, 4).reshape(B, S, NSA_Q_DIM)
    return o @ w_out


def mlstm_cell(q, k, v, ig, logf):
    B, H, S, dk = q.shape
    dv = v.shape[-1]
    L = ML_CHUNK
    nc = S // L
    q = q.reshape(B, H, nc, L, dk)
    k = k.reshape(B, H, nc, L, dk)
    v = v.reshape(B, H, nc, L, dv)
    ig = ig.reshape(B, H, nc, L)
    b = jnp.cumsum(logf.reshape(B, H, nc, L), axis=-1)
    b_tot = b[..., -1]

    w_end = b_tot[..., None] - b + ig
    m_loc = jnp.max(w_end, axis=-1)
    e_end = jnp.exp(w_end - m_loc[..., None])
    C_loc = jnp.einsum('bhcs,bhcsv,bhcsk->bhcvk', e_end, v, k)
    n_loc = jnp.einsum('bhcs,bhcsk->bhck', e_end, k)

    def step(carry, xs):
        C, n, m = carry
        C_l, n_l, m_l, bt = xs
        m_new = jnp.maximum(bt + m, m_l)
        a = jnp.exp(bt + m - m_new)
        bc = jnp.exp(m_l - m_new)
        C_new = a[..., None, None] * C + bc[..., None, None] * C_l
        n_new = a[..., None] * n + bc[..., None] * n_l
        return (C_new, n_new, m_new), (C, n, m)

    init = (jnp.zeros((B, H, dv, dk), jnp.float32), jnp.zeros((B, H, dk), jnp.float32),
            jnp.zeros((B, H), jnp.float32))
    xs = (jnp.moveaxis(C_loc, 2, 0), jnp.moveaxis(n_loc, 2, 0),
          jnp.moveaxis(m_loc, 2, 0), jnp.moveaxis(b_tot, 2, 0))
    _, (C_prev, n_prev, m_prev) = lax.scan(step, init, xs)
    C_prev = jnp.moveaxis(C_prev, 0, 2)
    n_prev = jnp.moveaxis(n_prev, 0, 2)
    m_prev = jnp.moveaxis(m_prev, 0, 2)

    log_inter = b + m_prev[..., None]
    causal = jnp.tril(jnp.ones((L, L), dtype=bool))
    log_intra = jnp.where(causal, b[..., :, None] - b[..., None, :] + ig[..., None, :], -jnp.inf)
    m_t = jnp.maximum(log_inter, jnp.max(log_intra, axis=-1))
    e_inter = jnp.exp(log_inter - m_t)
    e_intra = jnp.exp(log_intra - m_t[..., None])
    qk = jnp.einsum('bhctd,bhcsd->bhcts', q, k) * e_intra
    num = (jnp.einsum('bhcts,bhcsv->bhctv', qk, v) +
           e_inter[..., None] * jnp.einsum('bhcvk,bhctk->bhctv', C_prev, q))
    den = qk.sum(-1) + e_inter * jnp.einsum('bhck,bhctk->bhct', n_prev, q)
    h = num / jnp.maximum(jnp.abs(den), jnp.exp(-m_t))[..., None]
    return h.reshape(B, H, S, dv)


def mlstm_mixer(h, w_in, conv_w, conv_b, gate_b, mh_gain, w_out):
    B, S, _ = h.shape
    H, dk, dv = ML_HEADS, ML_QK_DIM, ML_V_DIM
    cuts = [ML_QK_COLS, ML_QK_COLS + H * dv, ML_QK_COLS + 2 * H * dv, ML_QK_COLS + 2 * H * dv + H]
    qk, v, o, ig, fg = jnp.split(h @ w_in, cuts, axis=-1)
    qk = lax.conv_general_dilated(qk, conv_w[:, None, :], window_strides=(1,),
                                  padding=[(ML_CONV - 1, 0)],
                                  dimension_numbers=('NWC', 'WIO', 'NWC'),
                                  feature_group_count=ML_QK_COLS)
    qk = jax.nn.silu(qk + conv_b)
    q, k = jnp.split(qk, 2, axis=-1)

    def heads(t, d):
        return t.reshape(B, S, H, d).transpose(0, 2, 1, 3).astype(jnp.float32)

    q = heads(q, dk)
    k = heads(k, dk) * (dk ** -0.5)
    v = heads(v, dv)
    ig = (ig + gate_b[0]).astype(jnp.float32).transpose(0, 2, 1)
    logf = jax.nn.log_sigmoid((fg + gate_b[1]).astype(jnp.float32)).transpose(0, 2, 1)
    hc = mlstm_cell(q, k, v, ig, logf)
    hn = hc * lax.rsqrt(jnp.mean(hc * hc, axis=-1, keepdims=True) + RMS_EPS)
    hn = hn.transpose(0, 2, 1, 3).reshape(B, S, H * dv) * mh_gain.astype(jnp.float32)
    out = jax.nn.sigmoid(o) * hn.astype(h.dtype)
    return out @ w_out


def setup_inputs(seed: int = 0) -> dict:
    key = jax.random.key(seed)
    ks = jax.random.split(key, 24)

    def w(k, shape, fan_in, mult=1.0):
        return jax.random.normal(k, shape, jnp.float32) * (mult * fan_in ** -0.5)

    def nrm(k, shape, s):
        return jax.random.normal(k, shape, jnp.float32) * s

    D, NL, ML = D_MODEL, N_NSA_LAYERS, N_MLSTM_LAYERS
    forget_b = jnp.linspace(3.0, 6.0, ML_HEADS, dtype=jnp.float32)[None, :] + nrm(ks[21], (ML, ML_HEADS), 0.1)
    input_b = nrm(ks[22], (ML, ML_HEADS), 0.1)
    return {
        'x': nrm(ks[0], (BATCH, SEQ, D), 1.0),
        'c': nrm(ks[1], (BATCH, D), 1.0),
        'mod_w': w(ks[2], (DEPTH, D, N_MOD * D), D, 0.5),
        'mod_b': nrm(ks[3], (DEPTH, N_MOD * D), 0.02),
        'norm_g': 1.0 + nrm(ks[4], (DEPTH, 6, D), 0.05),
        'ffn_pre_w_in': w(ks[5], (DEPTH, D, 2 * D_FF), D),
        'ffn_pre_w_out': w(ks[6], (DEPTH, D_FF, D), D_FF),
        'ffn_post_w_in': w(ks[7], (DEPTH, D, 2 * D_FF), D),
        'ffn_post_w_out': w(ks[8], (DEPTH, D_FF, D), D_FF),
        'nsa_w_in': w(ks[9], (NL, D, NSA_IN_DIM), D),
        'nsa_cmp_pos': nrm(ks[10], (NL, 2, CMP_BLOCK, NSA_HEAD_DIM), 0.5),
        'nsa_cmp_w1': w(ks[11], (NL, 2, CMP_BLOCK * NSA_HEAD_DIM, CMP_HIDDEN), CMP_BLOCK * NSA_HEAD_DIM),
        'nsa_cmp_w2': w(ks[12], (NL, 2, CMP_HIDDEN, NSA_HEAD_DIM), CMP_HIDDEN),
        'nsa_w_out': w(ks[13], (NL, NSA_Q_DIM, D), NSA_Q_DIM),
        'ml_w_in': w(ks[14], (ML, D, ML_IN_DIM), D),
        'ml_conv_w': w(ks[15], (ML, ML_CONV, ML_QK_COLS), ML_CONV),
        'ml_conv_b': nrm(ks[16], (ML, ML_QK_COLS), 0.02),
        'ml_gate_b': jnp.stack([input_b, forget_b], axis=1),
        'ml_mh_gain': 1.0 + nrm(ks[17], (ML, ML_HEADS * ML_V_DIM), 0.05),
        'ml_w_out': w(ks[18], (ML, ML_HEADS * ML_V_DIM, D), ML_HEADS * ML_V_DIM),
    }


def reference(x, c, mod_w, mod_b, norm_g, ffn_pre_w_in, ffn_pre_w_out, ffn_post_w_in, ffn_post_w_out,
              nsa_w_in, nsa_cmp_pos, nsa_cmp_w1, nsa_cmp_w2, nsa_w_out,
              ml_w_in, ml_conv_w, ml_conv_b, ml_gate_b, ml_mh_gain, ml_w_out):
    c_act = jax.nn.silu(c)

    def pre(h, g, shift, scale):
        return rms_norm(h, g) * (1.0 + scale[:, None, :]) + shift[:, None, :]

    for i in range(DEPTH):
        mod = c_act @ mod_w[i] + mod_b[i]
        sh1, sc1, g1, sh2, sc2, g2, sh3, sc3, g3 = jnp.split(mod, N_MOD, axis=-1)
        y = swiglu(pre(x, norm_g[i, 0], sh1, sc1), ffn_pre_w_in[i], ffn_pre_w_out[i])
        x = x + FFN_RES * g1[:, None, :] * rms_norm(y, norm_g[i, 1])
        h = pre(x, norm_g[i, 2], sh2, sc2)
        j = i // N_MIXERS
        if i % N_MIXERS == 0:
            y = nsa_mixer(h, nsa_w_in[j], nsa_cmp_pos[j], nsa_cmp_w1[j], nsa_cmp_w2[j], nsa_w_out[j])
        else:
            y = mlstm_mixer(h, ml_w_in[j], ml_conv_w[j], ml_conv_b[j], ml_gate_b[j], ml_mh_gain[j], ml_w_out[j])
        x = x + g2[:, None, :] * rms_norm(y, norm_g[i, 3])
        y = swiglu(pre(x, norm_g[i, 4], sh3, sc3), ffn_post_w_in[i], ffn_post_w_out[i])
        x = x + FFN_RES * g3[:, None, :] * rms_norm(y, norm_g[i, 5])
    return x
```

```python
import functools

import numpy as np
import jax
import jax.numpy as jnp
from jax import lax
from jax.experimental import pallas as pl
from jax.experimental.pallas import tpu as pltpu

F32 = jnp.float32
BF16 = jnp.bfloat16
HIGHEST = lax.Precision.HIGHEST

D_MODEL = 2048
DEPTH = 4
D_FF = 5632
FFN_RES = 0.5
N_MOD = 9
RMS_EPS = 1e-6
NEG_INF = -1e30

NSA_HEADS = 16
NSA_HEAD_DIM = 128
NSA_KV_GROUPS = 4
NSA_HPG = NSA_HEADS // NSA_KV_GROUPS
NSA_Q_DIM = NSA_HEADS * NSA_HEAD_DIM
NSA_KV_DIM = NSA_KV_GROUPS * NSA_HEAD_DIM
NSA_MAIN_DIM = NSA_Q_DIM + 6 * NSA_KV_DIM
CMP_BLOCK = 32
CMP_STRIDE = 16
CMP_HIDDEN = 512
SEL_BLOCK = 64
SEL_TOPK = 16
N_LOCAL_BLOCKS = 2
FORCE_SCORE = 1e9
WINDOW = 512
ROPE_THETA = 10000.0

ML_HEADS = 8
ML_QK_DIM = 128
ML_V_DIM = 256
ML_CONV = 4
ML_CHUNK = 64
ML_QK_COLS = 2 * ML_HEADS * ML_QK_DIM
ML_MAIN_DIM = ML_QK_COLS + 2 * ML_HEADS * ML_V_DIM

LANES = 128
VMEM_LIMIT = 56 * 1024 * 1024

NT_DIMS = (((1,), (1,)), ((), ()))
TN_DIMS = (((0,), (0,)), ((), ()))


def _cparams(sem):
    return pltpu.CompilerParams(dimension_semantics=sem, vmem_limit_bytes=VMEM_LIMIT)


def _silu(x):
    return x * jax.nn.sigmoid(x)


def _rms(x):
    return x * lax.rsqrt(jnp.mean(x * x, axis=-1, keepdims=True) + RMS_EPS)


def _dot(a, b):
    return jnp.dot(a, b, preferred_element_type=F32)


def _dot_nt(a, b):
    return lax.dot_general(a, b, NT_DIMS, preferred_element_type=F32)


def _mod_kernel(c_ref, w_ref, b_ref, o_ref):
    ca = _silu(c_ref[...]).astype(BF16)
    o_ref[...] = _dot(ca, w_ref[...].astype(BF16)) + b_ref[...]


def _mod_call(c, mod_w, mod_b):
    depth, d, n = mod_w.shape
    bsz = c.shape[0]
    tn = 1024
    return pl.pallas_call(
        _mod_kernel,
        out_shape=jax.ShapeDtypeStruct((depth, bsz, n), F32),
        grid=(depth, n // tn),
        in_specs=[
            pl.BlockSpec((bsz, d), lambda i, j: (0, 0)),
            pl.BlockSpec((None, d, tn), lambda i, j: (i, 0, j)),
            pl.BlockSpec((None, 1, tn), lambda i, j: (i, 0, j)),
        ],
        out_specs=pl.BlockSpec((None, bsz, tn), lambda i, j: (i, 0, j)),
        compiler_params=_cparams(("parallel", "parallel")),
        name="mod",
    )(c, mod_w, mod_b.reshape(depth, 1, n))


def _prenorm(x, mod_ref, ng_ref, mrow, grow):
    y = _rms(x) * ng_ref[grow:grow + 1, :]
    return y * (1.0 + mod_ref[mrow + 1:mrow + 2, :]) + mod_ref[mrow:mrow + 1, :]


def _ffn_kernel(x_ref, mod_ref, ng_ref, wg_ref, wu_ref, wo_ref, o_ref, xn_ref, acc_ref, *, mrow, grow):
    j = pl.program_id(2)

    @pl.when(j == 0)
    def _():
        xn_ref[...] = _prenorm(x_ref[...], mod_ref, ng_ref, mrow, grow).astype(BF16)
        acc_ref[...] = jnp.zeros_like(acc_ref)

    xn = xn_ref[...]
    g = _dot(xn, wg_ref[...])
    u = _dot(xn, wu_ref[...])
    acc_ref[...] += _dot((_silu(g) * u).astype(BF16), wo_ref[...])

    @pl.when(j == pl.num_programs(2) - 1)
    def _():
        yn = _rms(acc_ref[...]) * ng_ref[grow + 1:grow + 2, :]
        o_ref[...] = x_ref[...] + FFN_RES * mod_ref[mrow + 2:mrow + 3, :] * yn


def _ffn_call(x, mod, ng, w_in, w_out, *, mrow, grow):
    bsz, s, d = x.shape
    dff = w_out.shape[0]
    tm, tf = 512, 512
    nf = dff // tf
    return pl.pallas_call(
        functools.partial(_ffn_kernel, mrow=mrow, grow=grow),
        out_shape=jax.ShapeDtypeStruct(x.shape, F32),
        grid=(bsz, s // tm, nf),
        in_specs=[
            pl.BlockSpec((None, tm, d), lambda b, m, j: (b, m, 0)),
            pl.BlockSpec((None, N_MOD, d), lambda b, m, j: (b, 0, 0)),
            pl.BlockSpec(ng.shape, lambda b, m, j: (0, 0)),
            pl.BlockSpec((d, tf), lambda b, m, j: (0, j)),
            pl.BlockSpec((d, tf), lambda b, m, j: (0, nf + j)),
            pl.BlockSpec((tf, d), lambda b, m, j: (j, 0)),
        ],
        out_specs=pl.BlockSpec((None, tm, d), lambda b, m, j: (b, m, 0)),
        scratch_shapes=[pltpu.VMEM((tm, d), BF16), pltpu.VMEM((tm, d), F32)],
        compiler_params=_cparams(("parallel", "parallel", "arbitrary")),
        name="ffn",
    )(x, mod, ng, w_in, w_in, w_out)


def _proj_kernel(x_ref, mod_ref, ng_ref, w_ref, wgate_ref, o_ref, og_ref, xn_ref, *, mrow, grow):
    @pl.when(pl.program_id(2) == 0)
    def _():
        xn = _prenorm(x_ref[...], mod_ref, ng_ref, mrow, grow).astype(BF16)
        xn_ref[...] = xn
        og_ref[...] = _dot(xn, wgate_ref[...])

    o_ref[...] = _dot(xn_ref[...], w_ref[...])


def _proj_call(x, mod, ng, w_main, w_gate, *, mrow, grow):
    bsz, s, d = x.shape
    n = w_main.shape[1]
    ngate = w_gate.shape[1]
    tm, tn = 512, 1024
    return pl.pallas_call(
        functools.partial(_proj_kernel, mrow=mrow, grow=grow),
        out_shape=(jax.ShapeDtypeStruct((bsz, s, n), F32), jax.ShapeDtypeStruct((bsz, s, ngate), F32)),
        grid=(bsz, s // tm, n // tn),
        in_specs=[
            pl.BlockSpec((None, tm, d), lambda b, m, j: (b, m, 0)),
            pl.BlockSpec((None, N_MOD, d), lambda b, m, j: (b, 0, 0)),
            pl.BlockSpec(ng.shape, lambda b, m, j: (0, 0)),
            pl.BlockSpec((d, tn), lambda b, m, j: (0, j)),
            pl.BlockSpec((d, ngate), lambda b, m, j: (0, 0)),
        ],
        out_specs=(pl.BlockSpec((None, tm, tn), lambda b, m, j: (b, m, j)),
                   pl.BlockSpec((None, tm, ngate), lambda b, m, j: (b, m, 0))),
        scratch_shapes=[pltpu.VMEM((tm, d), BF16)],
        compiler_params=_cparams(("parallel", "parallel", "arbitrary")),
        name="proj",
    )(x, mod, ng, w_main, w_gate)


def _oproj_kernel(a_ref, w_ref, x_ref, mod_ref, ng_ref, o_ref, *, mrow, grow):
    yn = _rms(_dot(a_ref[...], w_ref[...])) * ng_ref[grow:grow + 1, :]
    o_ref[...] = x_ref[...] + mod_ref[mrow:mrow + 1, :] * yn


def _oproj_call(a, w, x, mod, ng, *, mrow, grow):
    bsz, s, d = x.shape
    k = a.shape[-1]
    tm = 256
    return pl.pallas_call(
        functools.partial(_oproj_kernel, mrow=mrow, grow=grow),
        out_shape=jax.ShapeDtypeStruct(x.shape, F32),
        grid=(bsz, s // tm),
        in_specs=[
            pl.BlockSpec((None, tm, k), lambda b, m: (b, m, 0)),
            pl.BlockSpec((k, d), lambda b, m: (0, 0)),
            pl.BlockSpec((None, tm, d), lambda b, m: (b, m, 0)),
            pl.BlockSpec((None, N_MOD, d), lambda b, m: (b, 0, 0)),
            pl.BlockSpec(ng.shape, lambda b, m: (0, 0)),
        ],
        out_specs=pl.BlockSpec((None, tm, d), lambda b, m: (b, m, 0)),
        compiler_params=_cparams(("parallel", "parallel")),
        name="oproj",
    )(a, w, x, mod, ng)


def _rope(x, cos, sin_signed):
    return x * cos + pltpu.roll(x, NSA_HEAD_DIM // 2, 1) * sin_signed


def _nsa_prep_kernel(kc_ref, vc_ref, ks_ref, vs_ref, kw_ref, vw_ref, pe_ref, w1_ref, w2_ref, cos_ref, sin_ref,
                     kco_ref, vco_ref, kso_ref, vso_ref, kwo_ref, vwo_ref):
    s = kc_ref.shape[0]
    nrow = s // CMP_STRIDE
    half = CMP_BLOCK // CMP_STRIDE

    def compress(t_ref, idx, out_ref):
        streams = [t_ref[pl.ds(j, nrow, stride=CMP_STRIDE), :] for j in range(CMP_STRIDE)]
        hidden = None
        for h in range(half):
            flat = jnp.concatenate(
                [(streams[j] + pe_ref[idx, h * CMP_STRIDE + j:h * CMP_STRIDE + j + 1, :]).astype(BF16)
                 for j in range(CMP_STRIDE)], axis=1)
            w = w1_ref[idx, h * CMP_STRIDE * NSA_HEAD_DIM:(h + 1) * CMP_STRIDE * NSA_HEAD_DIM, :]
            part = _dot(flat, w)
            if h:
                part = pltpu.roll(part, nrow - h, 0)
            hidden = part if hidden is None else hidden + part
        out_ref[...] = _dot(_silu(hidden).astype(BF16), w2_ref[idx]).astype(out_ref.dtype)

    compress(kc_ref, 0, kco_ref)
    compress(vc_ref, 1, vco_ref)
    cos = cos_ref[...]
    sin = sin_ref[...]
    kso_ref[...] = _rope(ks_ref[...], cos, sin).astype(BF16)
    kwo_ref[...] = _rope(kw_ref[...], cos, sin).astype(BF16)
    vso_ref[...] = vs_ref[...].astype(BF16)
    vwo_ref[...] = vw_ref[...].astype(BF16)


def _nsa_prep_call(main, pe, w1, w2, cos, sin):
    bsz, s, _ = main.shape
    g, dh = NSA_KV_GROUPS, NSA_HEAD_DIM
    nrow = s // CMP_STRIDE
    q_blocks = NSA_Q_DIM // dh

    def col(k):
        return pl.BlockSpec((None, s, dh), lambda b, gi, k=k: (b, 0, q_blocks + k * g + gi))

    def full(a):
        return pl.BlockSpec(a.shape, lambda b, gi, nd=a.ndim: (0,) * nd)

    kv_out = pl.BlockSpec((None, None, s, dh), lambda b, gi: (b, gi, 0, 0))
    cmp_out = pl.BlockSpec((None, None, nrow, dh), lambda b, gi: (b, gi, 0, 0))
    kv_shape = jax.ShapeDtypeStruct((bsz, g, s, dh), BF16)
    cmp_shape = jax.ShapeDtypeStruct((bsz, g, nrow, dh), BF16)
    return pl.pallas_call(
        _nsa_prep_kernel,
        out_shape=(cmp_shape, cmp_shape, kv_shape, kv_shape, kv_shape, kv_shape),
        grid=(bsz, g),
        in_specs=[col(0), col(1), col(2), col(3), col(4), col(5), full(pe), full(w1), full(w2), full(cos), full(sin)],
        out_specs=(cmp_out, cmp_out, kv_out, kv_out, kv_out, kv_out),
        compiler_params=_cparams(("parallel", "parallel")),
        name="nsa_prep",
    )(main, main, main, main, main, main, pe, w1, w2, cos, sin)


def _nsa_attn_kernel(q_ref, gl_ref, cos_ref, sin_ref, kc_ref, vc_ref, ks_ref, vs_ref, kw_ref, vw_ref,
                     expand_ref, cover_ref, o_ref, qrot_scr, m_scr, l_scr, acc_scr, out_scr, *, tq, tk, n_sel):
    qi = pl.program_id(2)
    dh = NSA_HEAD_DIM
    scale = dh ** -0.5
    pos_q = qi * tq + lax.broadcasted_iota(jnp.int32, (tq, 1), 0)
    lane = lax.broadcasted_iota(jnp.int32, (1, LANES), 1)
    cos = cos_ref[...]
    sin = sin_ref[...]
    gates = jax.nn.sigmoid(gl_ref[...])

    cmask = lane * CMP_STRIDE + (CMP_BLOCK - 1) <= pos_q
    kc = kc_ref[...]
    vc = vc_ref[...]
    p_sum = jnp.zeros((tq, LANES), F32)
    for r in range(NSA_HPG):
        q_r = q_ref[:, r * dh:(r + 1) * dh]
        sc = jnp.where(cmask, _dot_nt(q_r.astype(BF16), kc) * scale, NEG_INF)
        e = jnp.exp(sc - jnp.max(sc, axis=-1, keepdims=True))
        p = jnp.where(cmask, e / jnp.sum(e, axis=-1, keepdims=True), 0.0)
        p_sum = p_sum + p
        out_scr[r] = gates[:, 3 * r:3 * r + 1] * _dot(p.astype(BF16), vc)
        qrot_scr[r] = _rope(q_r, cos, sin).astype(BF16)

    imp = jnp.dot(p_sum, cover_ref[...], precision=HIGHEST, preferred_element_type=F32)
    q_blk = pos_q // SEL_BLOCK
    causal = lane <= q_blk
    forced = (lane == 0) | (causal & (lane > q_blk - N_LOCAL_BLOCKS))
    imp = jnp.where(forced, FORCE_SCORE, jnp.where(causal, imp, -1.0))
    rank = jnp.zeros((tq, LANES), F32)
    for jp in range(n_sel):
        other = imp[:, jp:jp + 1]
        beats = (other > imp) | ((other == imp) & (lane > jp))
        rank = rank + jnp.where(beats, 1.0, 0.0)
    selected = jnp.where((rank < float(min(SEL_TOPK, n_sel))) & (imp >= 0.0), 1.0, 0.0).astype(BF16)

    def reset():
        m_scr[...] = jnp.full(m_scr.shape, NEG_INF, F32)
        l_scr[...] = jnp.zeros(l_scr.shape, F32)
        acc_scr[...] = jnp.zeros(acc_scr.shape, F32)

    def flash_step(k_ref, v_ref, k0, allowed):
        kt = k_ref[pl.ds(k0, tk), :]
        vt = v_ref[pl.ds(k0, tk), :]
        for r in range(NSA_HPG):
            sc = jnp.where(allowed, _dot_nt(qrot_scr[r], kt) * scale, NEG_INF)
            m_old = m_scr[r]
            m_new = jnp.maximum(m_old, jnp.max(sc, axis=-1, keepdims=True))
            alpha = jnp.exp(m_old - m_new)
            p = jnp.exp(sc - m_new)
            l_scr[r] = alpha * l_scr[r] + jnp.sum(p, axis=-1, keepdims=True)
            acc_scr[r] = alpha * acc_scr[r] + _dot(p.astype(BF16), vt)
            m_scr[r] = m_new

    def finish(branch):
        for r in range(NSA_HPG):
            out_scr[r] = out_scr[r] + gates[:, 3 * r + branch:3 * r + branch + 1] * (acc_scr[r] / l_scr[r])

    def key_pos(k0):
        return k0 + lax.broadcasted_iota(jnp.int32, (1, tk), 1)

    reset()

    def sel_body(kt, carry):
        k0 = pl.multiple_of(kt * tk, tk)
        picked = _dot(selected, expand_ref[:, pl.ds(k0, tk)])
        flash_step(ks_ref, vs_ref, k0, (picked > 0.5) & (key_pos(k0) <= pos_q))
        return carry

    lax.fori_loop(0, (qi * tq) // tk + tq // tk, sel_body, 0)
    finish(1)

    reset()

    def win_body(kt, carry):
        k0 = pl.multiple_of(kt * tk, tk)
        kp = key_pos(k0)
        flash_step(kw_ref, vw_ref, k0, (kp <= pos_q) & (pos_q - kp < WINDOW))
        return carry

    first = jnp.maximum(qi * tq - WINDOW, 0) // tk
    lax.fori_loop(first, (qi * tq) // tk + tq // tk, win_body, 0)
    finish(2)

    for r in range(NSA_HPG):
        o_ref[:, r * dh:(r + 1) * dh] = out_scr[r].astype(o_ref.dtype)


def _nsa_attn_call(main, gl, cos, sin, kc, vc, ks, vs, kw, vw, expand, cover):
    bsz, s, _ = main.shape
    g, r, dh = NSA_KV_GROUPS, NSA_HPG, NSA_HEAD_DIM
    tq = tk = 256
    nrow = kc.shape[2]

    def kv(n):
        return pl.BlockSpec((None, None, n, dh), lambda b, gi, qi: (b, gi, 0, 0))

    return pl.pallas_call(
        functools.partial(_nsa_attn_kernel, tq=tq, tk=tk, n_sel=s // SEL_BLOCK),
        out_shape=jax.ShapeDtypeStruct((bsz, s, NSA_Q_DIM), BF16),
        grid=(bsz, g, s // tq),
        in_specs=[
            pl.BlockSpec((None, tq, r * dh), lambda b, gi, qi: (b, qi, gi)),
            pl.BlockSpec((None, tq, LANES), lambda b, gi, qi: (b, qi, gi)),
            pl.BlockSpec((tq, dh), lambda b, gi, qi: (qi, 0)),
            pl.BlockSpec((tq, dh), lambda b, gi, qi: (qi, 0)),
            kv(nrow), kv(nrow), kv(s), kv(s), kv(s), kv(s),
            pl.BlockSpec(expand.shape, lambda b, gi, qi: (0, 0)),
            pl.BlockSpec(cover.shape, lambda b, gi, qi: (0, 0)),
        ],
        out_specs=pl.BlockSpec((None, tq, r * dh), lambda b, gi, qi: (b, qi, gi)),
        scratch_shapes=[
            pltpu.VMEM((r, tq, dh), BF16),
            pltpu.VMEM((r, tq, 1), F32),
            pltpu.VMEM((r, tq, 1), F32),
            pltpu.VMEM((r, tq, dh), F32),
            pltpu.VMEM((r, tq, dh), F32),
        ],
        compiler_params=_cparams(("parallel", "parallel", "arbitrary")),
        name="nsa_attn",
    )(main, gl, cos, sin, kc, vc, ks, vs, kw, vw, expand, cover)


def _nsa_tables(s):
    half = NSA_HEAD_DIM // 2
    freqs = ROPE_THETA ** (-jnp.arange(half, dtype=F32) / half)
    ang = jnp.arange(s).astype(F32)[:, None] * freqs[None, :]
    cos, sin = jnp.cos(ang), jnp.sin(ang)
    cos_full = jnp.concatenate([cos, cos], axis=-1)
    sin_signed = jnp.concatenate([-sin, sin], axis=-1)
    expand = (np.arange(s)[None, :] // SEL_BLOCK == np.arange(LANES)[:, None]).astype(np.float32)
    n_cmp = (s - CMP_BLOCK) // CMP_STRIDE + 1
    cs = np.arange(LANES)[:, None] * CMP_STRIDE
    js = np.arange(LANES)[None, :]
    cover = ((cs < (js + 1) * SEL_BLOCK) & (cs + CMP_BLOCK > js * SEL_BLOCK)
             & (np.arange(LANES)[:, None] < n_cmp) & (js < s // SEL_BLOCK)).astype(np.float32)
    return cos_full, sin_signed, jnp.asarray(expand, BF16), jnp.asarray(cover, F32)


def _nsa_layer(x, mod, ng, w_in, cmp_pos, cmp_w1, cmp_w2, w_out, tables):
    cos, sin, expand, cover = tables
    g, r = NSA_KV_GROUPS, NSA_HPG
    w_main = w_in[:, :NSA_MAIN_DIM].astype(BF16)
    w_gate = w_in[:, NSA_MAIN_DIM:].reshape(D_MODEL, g, 3 * r)
    w_gate = jnp.pad(w_gate, ((0, 0), (0, 0), (0, LANES - 3 * r))).reshape(D_MODEL, g * LANES).astype(BF16)
    main, gl = _proj_call(x, mod, ng, w_main, w_gate, mrow=3, grow=2)
    kc, vc, ks, vs, kw, vw = _nsa_prep_call(main, cmp_pos, cmp_w1.astype(BF16), cmp_w2.astype(BF16), cos, sin)
    o = _nsa_attn_call(main, gl, cos, sin, kc, vc, ks, vs, kw, vw, expand, cover)
    return _oproj_call(o, w_out.astype(BF16), x, mod, ng, mrow=5, grow=3)


def _mlstm_kernel(q_ref, k_ref, v_ref, og_ref, gi_ref, gf_ref, bi_ref, bf_ref, cwq_ref, cwk_ref, cbq_ref, cbk_ref,
                  gain_ref, o_ref, q_scr, k_scr, va_scr, ct_scr, b_scr, rowi_scr, mloc_scr, bcol_scr, ecol_scr):
    s, dk = q_ref.shape
    dv = v_ref.shape[1]
    chunk = ML_CHUNK
    nc = s // chunk
    row = lax.broadcasted_iota(jnp.int32, (s, 1), 0)

    def conv_silu(x, w_ref, b_ref):
        acc = x * w_ref[ML_CONV - 1:ML_CONV, :]
        for d in range(1, ML_CONV):
            shifted = jnp.where(row >= d, pltpu.roll(x, d, 0), 0.0)
            acc = acc + shifted * w_ref[ML_CONV - 1 - d:ML_CONV - d, :]
        return _silu(acc + b_ref[...])

    q_scr[...] = conv_silu(q_ref[...], cwq_ref, cbq_ref).astype(BF16)
    k_scr[...] = conv_silu(k_ref[...], cwk_ref, cbk_ref) * (dk ** -0.5)
    va_scr[:, :dv] = v_ref[...].astype(BF16)
    va_scr[:, dv:] = jnp.ones((s, va_scr.shape[1] - dv), BF16)

    ig = gi_ref[...] + bi_ref[...]
    fg = gf_ref[...] + bf_ref[...]
    logf = jnp.minimum(fg, 0.0) - jnp.log1p(jnp.exp(-jnp.abs(fg)))
    tri_r = lax.broadcasted_iota(jnp.int32, (chunk, chunk), 0)
    tri_c = lax.broadcasted_iota(jnp.int32, (chunk, chunk), 1)
    upper = jnp.where(tri_r <= tri_c, 1.0, 0.0)
    b = jnp.dot(logf, upper, precision=HIGHEST, preferred_element_type=F32)
    w_end = b[:, chunk - 1:chunk] - b + ig
    m_loc = jnp.max(w_end, axis=-1, keepdims=True)
    e_end = jnp.exp(w_end - m_loc)
    b_scr[...] = b
    rowi_scr[...] = ig - b
    mloc_scr[...] = jnp.broadcast_to(m_loc, mloc_scr.shape)
    eye = jnp.where(tri_r == tri_c, 1.0, 0.0)
    cols = lax.dot_general(eye, jnp.concatenate([b, e_end], axis=0), NT_DIMS, precision=HIGHEST,
                           preferred_element_type=F32)
    for c in range(nc):
        bcol_scr[c] = jnp.broadcast_to(cols[:, c:c + 1], (chunk, LANES))
        ecol_scr[c] = jnp.broadcast_to(cols[:, nc + c:nc + c + 1], (chunk, LANES))

    ct_scr[...] = jnp.zeros(ct_scr.shape, F32)
    lower = tri_c <= tri_r
    gain = gain_ref[...]

    def body(c, m_prev):
        r0 = pl.multiple_of(c * chunk, chunk)
        q = q_scr[pl.ds(r0, chunk), :]
        k = k_scr[pl.ds(r0, chunk), :]
        va = va_scr[pl.ds(r0, chunk), :]
        b_row = b_scr[pl.ds(c, 1), :]
        b_tot = b_row[:, chunk - 1:chunk]
        m_l = mloc_scr[pl.ds(c, 1), :][:, :1]
        bcol = bcol_scr[c]
        log_intra = jnp.where(lower, bcol[:, :chunk] + rowi_scr[pl.ds(c, 1), :], NEG_INF)
        log_inter = bcol[:, :1] + m_prev
        m_t = jnp.maximum(log_inter, jnp.max(log_intra, axis=-1, keepdims=True))
        e_inter = jnp.exp(log_inter - m_t)
        qk = _dot_nt(q, k.astype(BF16)) * jnp.exp(log_intra - m_t)
        ct = ct_scr[...]
        tot = _dot(qk.astype(BF16), va) + e_inter * _dot(q, ct.astype(BF16))
        h = tot[:, :dv] / jnp.maximum(jnp.abs(tot[:, dv:dv + 1]), jnp.exp(-m_t))
        hn = _rms(h) * gain
        o_ref[pl.ds(r0, chunk), :] = (jax.nn.sigmoid(og_ref[pl.ds(r0, chunk), :]) * hn).astype(o_ref.dtype)
        m_new = jnp.maximum(b_tot + m_prev, m_l)
        c_loc = lax.dot_general((k * ecol_scr[c]).astype(BF16), va, TN_DIMS, preferred_element_type=F32)
        ct_scr[...] = jnp.exp(b_tot + m_prev - m_new) * ct + jnp.exp(m_l - m_new) * c_loc
        return m_new

    lax.fori_loop(0, nc, body, jnp.zeros((1, 1), F32))


def _mlstm_call(main, gates_t, gate_b, conv_w, conv_b, gain):
    bsz, s, _ = main.shape
    h, dk, dv = ML_HEADS, ML_QK_DIM, ML_V_DIM
    chunk = ML_CHUNK
    nc = s // chunk
    vblk0 = ML_QK_COLS // dv

    def gate(off):
        return pl.BlockSpec((None, None, nc, chunk), lambda b, hi: (b, off + hi, 0, 0))

    def bias(off):
        return pl.BlockSpec((None, 1, 1), lambda b, hi: (off + hi, 0, 0))

    return pl.pallas_call(
        _mlstm_kernel,
        out_shape=jax.ShapeDtypeStruct((bsz, s, h * dv), BF16),
        grid=(bsz, h),
        in_specs=[
            pl.BlockSpec((None, s, dk), lambda b, hi: (b, 0, hi)),
            pl.BlockSpec((None, s, dk), lambda b, hi: (b, 0, h + hi)),
            pl.BlockSpec((None, s, dv), lambda b, hi: (b, 0, vblk0 + hi)),
            pl.BlockSpec((None, s, dv), lambda b, hi: (b, 0, vblk0 + h + hi)),
            gate(0), gate(h), bias(0), bias(h),
            pl.BlockSpec((ML_CONV, dk), lambda b, hi: (0, hi)),
            pl.BlockSpec((ML_CONV, dk), lambda b, hi: (0, h + hi)),
            pl.BlockSpec((1, dk), lambda b, hi: (0, hi)),
            pl.BlockSpec((1, dk), lambda b, hi: (0, h + hi)),
            pl.BlockSpec((1, dv), lambda b, hi: (0, hi)),
        ],
        out_specs=pl.BlockSpec((None, s, dv), lambda b, hi: (b, 0, hi)),
        scratch_shapes=[
            pltpu.VMEM((s, dk), BF16),
            pltpu.VMEM((s, dk), F32),
            pltpu.VMEM((s, dv + LANES), BF16),
            pltpu.VMEM((dk, dv + LANES), F32),
            pltpu.VMEM((nc, chunk), F32),
            pltpu.VMEM((nc, chunk), F32),
            pltpu.VMEM((nc, LANES), F32),
            pltpu.VMEM((nc, chunk, LANES), F32),
            pltpu.VMEM((nc, chunk, LANES), F32),
        ],
        compiler_params=_cparams(("parallel", "parallel")),
        name="mlstm",
    )(main, main, main, main, gates_t, gates_t, gate_b, gate_b, conv_w, conv_w, conv_b, conv_b, gain)


def _mlstm_layer(x, mod, ng, w_in, conv_w, conv_b, gate_b, mh_gain, w_out):
    bsz, s, _ = x.shape
    ngate = 2 * ML_HEADS
    w_main = w_in[:, :ML_MAIN_DIM].astype(BF16)
    w_gate = jnp.pad(w_in[:, ML_MAIN_DIM:], ((0, 0), (0, LANES - ngate))).astype(BF16)
    main, gl = _proj_call(x, mod, ng, w_main, w_gate, mrow=3, grow=2)
    gates_t = jnp.swapaxes(gl[:, :, :ngate], 1, 2).reshape(bsz, ngate, s // ML_CHUNK, ML_CHUNK)
    o = _mlstm_call(main, gates_t, gate_b.reshape(ngate, 1, 1), conv_w, conv_b.reshape(1, -1),
                    mh_gain.reshape(1, -1))
    return _oproj_call(o, w_out.astype(BF16), x, mod, ng, mrow=5, grow=3)


def kernel(x, c, mod_w, mod_b, norm_g, ffn_pre_w_in, ffn_pre_w_out, ffn_post_w_in, ffn_post_w_out, nsa_w_in, nsa_cmp_pos, nsa_cmp_w1, nsa_cmp_w2, nsa_w_out, ml_w_in, ml_conv_w, ml_conv_b, ml_gate_b, ml_mh_gain, ml_w_out):
    bsz, s, d = x.shape
    depth = mod_w.shape[0]
    mods = _mod_call(c, mod_w, mod_b).reshape(depth, bsz, N_MOD, d)
    tables = _nsa_tables(s)
    for i in range(depth):
        mod, ng = mods[i], norm_g[i]
        x = _ffn_call(x, mod, ng, ffn_pre_w_in[i].astype(BF16), ffn_pre_w_out[i].astype(BF16), mrow=0, grow=0)
        j = i // 2
        if i % 2 == 0:
            x = _nsa_layer(x, mod, ng, nsa_w_in[j], nsa_cmp_pos[j], nsa_cmp_w1[j], nsa_cmp_w2[j], nsa_w_out[j],
                           tables)
        else:
            x = _mlstm_layer(x, mod, ng, ml_w_in[j], ml_conv_w[j], ml_conv_b[j], ml_gate_b[j], ml_mh_gain[j],
                             ml_w_out[j])
        x = _ffn_call(x, mod, ng, ffn_post_w_in[i].astype(BF16), ffn_post_w_out[i].astype(BF16), mrow=6, grow=4)
    return x
```

```python
import functools

import numpy as np
import jax
import jax.numpy as jnp
from jax import lax
from jax.experimental import pallas as pl
from jax.experimental.pallas import tpu as pltpu

F32 = jnp.float32
BF16 = jnp.bfloat16
HIGHEST = lax.Precision.HIGHEST

D_MODEL = 2048
DEPTH = 4
D_FF = 5632
FFN_RES = 0.5
N_MOD = 9
RMS_EPS = 1e-6
NEG_INF = -1e30

NSA_HEADS = 16
NSA_HEAD_DIM = 128
NSA_KV_GROUPS = 4
NSA_HPG = NSA_HEADS // NSA_KV_GROUPS
NSA_Q_DIM = NSA_HEADS * NSA_HEAD_DIM
NSA_KV_DIM = NSA_KV_GROUPS * NSA_HEAD_DIM
NSA_MAIN_DIM = NSA_Q_DIM + 6 * NSA_KV_DIM
CMP_BLOCK = 32
CMP_STRIDE = 16
CMP_HIDDEN = 512
SEL_BLOCK = 64
SEL_TOPK = 16
N_LOCAL_BLOCKS = 2
FORCE_SCORE = 1e9
WINDOW = 512
ROPE_THETA = 10000.0

ML_HEADS = 8
ML_QK_DIM = 128
ML_V_DIM = 256
ML_CONV = 4
ML_CHUNK = 64
ML_QK_COLS = 2 * ML_HEADS * ML_QK_DIM
ML_MAIN_DIM = ML_QK_COLS + 2 * ML_HEADS * ML_V_DIM

LANES = 128
SUBLANES = 8
VMEM_LIMIT = 56 * 1024 * 1024

NT_DIMS = (((1,), (1,)), ((), ()))
TN_DIMS = (((0,), (0,)), ((), ()))


def _cparams(sem):
    return pltpu.CompilerParams(dimension_semantics=sem, vmem_limit_bytes=VMEM_LIMIT)


def _silu(x):
    return x * jax.nn.sigmoid(x)


def _rms(x):
    return x * lax.rsqrt(jnp.mean(x * x, axis=-1, keepdims=True) + RMS_EPS)


def _dot(a, b):
    return jnp.dot(a, b, preferred_element_type=F32)


def _dot_nt(a, b):
    return lax.dot_general(a, b, NT_DIMS, preferred_element_type=F32)


def _mod_kernel(c_ref, w_ref, b_ref, o_ref):
    ca = _silu(c_ref[...]).astype(BF16)
    o_ref[...] = _dot(ca, w_ref[...].astype(BF16)) + b_ref[...]


def _mod_call(c, mod_w, mod_b):
    depth, d, n = mod_w.shape
    bsz = c.shape[0]
    tn = 1024
    return pl.pallas_call(
        _mod_kernel,
        out_shape=jax.ShapeDtypeStruct((depth, bsz, n), F32),
        grid=(depth, n // tn),
        in_specs=[
            pl.BlockSpec((bsz, d), lambda i, j: (0, 0)),
            pl.BlockSpec((None, d, tn), lambda i, j: (i, 0, j)),
            pl.BlockSpec((None, 1, tn), lambda i, j: (i, 0, j)),
        ],
        out_specs=pl.BlockSpec((None, bsz, tn), lambda i, j: (i, 0, j)),
        compiler_params=_cparams(("parallel", "parallel")),
        name="mod",
    )(c, mod_w, mod_b.reshape(depth, 1, n))


def _prenorm(x, mod_ref, ng_ref, mrow, grow):
    y = _rms(x) * ng_ref[grow:grow + 1, :]
    return y * (1.0 + mod_ref[mrow + 1:mrow + 2, :]) + mod_ref[mrow:mrow + 1, :]


def _ffn_kernel(x_ref, mod_ref, ng_ref, wg_ref, wu_ref, wo_ref, o_ref, xn_ref, acc_ref, *, mrow, grow):
    j = pl.program_id(2)

    @pl.when(j == 0)
    def _():
        xn_ref[...] = _prenorm(x_ref[...], mod_ref, ng_ref, mrow, grow).astype(BF16)
        acc_ref[...] = jnp.zeros_like(acc_ref)

    xn = xn_ref[...]
    g = _dot(xn, wg_ref[...])
    u = _dot(xn, wu_ref[...])
    acc_ref[...] += _dot((_silu(g) * u).astype(BF16), wo_ref[...])

    @pl.when(j == pl.num_programs(2) - 1)
    def _():
        yn = _rms(acc_ref[...]) * ng_ref[grow + 1:grow + 2, :]
        o_ref[...] = x_ref[...] + FFN_RES * mod_ref[mrow + 2:mrow + 3, :] * yn


def _ffn_call(x, mod, ng, w_in, w_out, *, mrow, grow):
    bsz, s, d = x.shape
    dff = w_out.shape[0]
    tm, tf = 512, 512
    nf = dff // tf
    return pl.pallas_call(
        functools.partial(_ffn_kernel, mrow=mrow, grow=grow),
        out_shape=jax.ShapeDtypeStruct(x.shape, F32),
        grid=(bsz, s // tm, nf),
        in_specs=[
            pl.BlockSpec((None, tm, d), lambda b, m, j: (b, m, 0)),
            pl.BlockSpec((None, N_MOD, d), lambda b, m, j: (b, 0, 0)),
            pl.BlockSpec(ng.shape, lambda b, m, j: (0, 0)),
            pl.BlockSpec((d, tf), lambda b, m, j: (0, j)),
            pl.BlockSpec((d, tf), lambda b, m, j: (0, nf + j)),
            pl.BlockSpec((tf, d), lambda b, m, j: (j, 0)),
        ],
        out_specs=pl.BlockSpec((None, tm, d), lambda b, m, j: (b, m, 0)),
        scratch_shapes=[pltpu.VMEM((tm, d), BF16), pltpu.VMEM((tm, d), F32)],
        compiler_params=_cparams(("parallel", "parallel", "arbitrary")),
        name="ffn",
    )(x, mod, ng, w_in, w_in, w_out)


def _proj_kernel(x_ref, mod_ref, ng_ref, w_ref, wgate_ref, o_ref, og_ref, xn_ref, *, mrow, grow):
    @pl.when(pl.program_id(2) == 0)
    def _():
        xn = _prenorm(x_ref[...], mod_ref, ng_ref, mrow, grow).astype(BF16)
        xn_ref[...] = xn
        og_ref[...] = _dot(xn, wgate_ref[...])

    o_ref[...] = _dot(xn_ref[...], w_ref[...])


def _proj_call(x, mod, ng, w_main, w_gate, *, mrow, grow):
    bsz, s, d = x.shape
    n = w_main.shape[1]
    ngate = w_gate.shape[1]
    tm, tn = 1024, 1024
    return pl.pallas_call(
        functools.partial(_proj_kernel, mrow=mrow, grow=grow),
        out_shape=(jax.ShapeDtypeStruct((bsz, s, n), F32), jax.ShapeDtypeStruct((bsz, s, ngate), F32)),
        grid=(bsz, s // tm, n // tn),
        in_specs=[
            pl.BlockSpec((None, tm, d), lambda b, m, j: (b, m, 0)),
            pl.BlockSpec((None, N_MOD, d), lambda b, m, j: (b, 0, 0)),
            pl.BlockSpec(ng.shape, lambda b, m, j: (0, 0)),
            pl.BlockSpec((d, tn), lambda b, m, j: (0, j)),
            pl.BlockSpec((d, ngate), lambda b, m, j: (0, 0)),
        ],
        out_specs=(pl.BlockSpec((None, tm, tn), lambda b, m, j: (b, m, j)),
                   pl.BlockSpec((None, tm, ngate), lambda b, m, j: (b, m, 0))),
        scratch_shapes=[pltpu.VMEM((tm, d), BF16)],
        compiler_params=_cparams(("parallel", "parallel", "arbitrary")),
        name="proj",
    )(x, mod, ng, w_main, w_gate)


def _oproj_kernel(a_ref, w_ref, x_ref, mod_ref, ng_ref, o_ref, *, mrow, grow):
    yn = _rms(_dot(a_ref[...], w_ref[...])) * ng_ref[grow:grow + 1, :]
    o_ref[...] = x_ref[...] + mod_ref[mrow:mrow + 1, :] * yn


def _oproj_call(a, w, x, mod, ng, *, mrow, grow):
    bsz, s, d = x.shape
    k = a.shape[-1]
    tm = 256
    return pl.pallas_call(
        functools.partial(_oproj_kernel, mrow=mrow, grow=grow),
        out_shape=jax.ShapeDtypeStruct(x.shape, F32),
        grid=(bsz, s // tm),
        in_specs=[
            pl.BlockSpec((None, tm, k), lambda b, m: (b, m, 0)),
            pl.BlockSpec((k, d), lambda b, m: (0, 0)),
            pl.BlockSpec((None, tm, d), lambda b, m: (b, m, 0)),
            pl.BlockSpec((None, N_MOD, d), lambda b, m: (b, 0, 0)),
            pl.BlockSpec(ng.shape, lambda b, m: (0, 0)),
        ],
        out_specs=pl.BlockSpec((None, tm, d), lambda b, m: (b, m, 0)),
        compiler_params=_cparams(("parallel", "parallel")),
        name="oproj",
    )(a, w, x, mod, ng)


def _rope(x, cos, sin_signed):
    return x * cos + pltpu.roll(x, NSA_HEAD_DIM // 2, 1) * sin_signed


def _nsa_prep_kernel(kc_ref, vc_ref, ks_ref, vs_ref, kw_ref, vw_ref, pe_ref, w1_ref, w2_ref, cos_ref, sin_ref,
                     kco_ref, vco_ref, kso_ref, vso_ref, kwo_ref, vwo_ref):
    s = kc_ref.shape[0]
    nrow = s // CMP_STRIDE
    half = CMP_BLOCK // CMP_STRIDE

    def compress(t_ref, idx):
        streams = [t_ref[pl.ds(j, nrow, stride=CMP_STRIDE), :] for j in range(CMP_STRIDE)]
        hidden = None
        for h in range(half):
            flat = jnp.concatenate(
                [(streams[j] + pe_ref[idx, h * CMP_STRIDE + j:h * CMP_STRIDE + j + 1, :]).astype(BF16)
                 for j in range(CMP_STRIDE)], axis=1)
            w = w1_ref[idx, h * CMP_STRIDE * NSA_HEAD_DIM:(h + 1) * CMP_STRIDE * NSA_HEAD_DIM, :]
            part = _dot(flat, w)
            if h:
                part = pltpu.roll(part, nrow - h, 0)
            hidden = part if hidden is None else hidden + part
        return _dot(_silu(hidden).astype(BF16), w2_ref[idx])

    kco_ref[...] = compress(kc_ref, 0).astype(kco_ref.dtype)
    vco_ref[...] = compress(vc_ref, 1).astype(vco_ref.dtype)
    cos = cos_ref[...]
    sin = sin_ref[...]
    kso_ref[...] = _rope(ks_ref[...], cos, sin).astype(BF16)
    kwo_ref[...] = _rope(kw_ref[...], cos, sin).astype(BF16)
    vso_ref[...] = vs_ref[...].T.astype(BF16)
    vwo_ref[...] = vw_ref[...].T.astype(BF16)


def _nsa_prep_call(main, pe, w1, w2, cos, sin):
    bsz, s, _ = main.shape
    g, dh = NSA_KV_GROUPS, NSA_HEAD_DIM
    nrow = s // CMP_STRIDE
    q_blocks = NSA_Q_DIM // dh

    def col(k):
        return pl.BlockSpec((None, s, dh), lambda b, gi, k=k: (b, 0, q_blocks + k * g + gi))

    def full(a):
        return pl.BlockSpec(a.shape, lambda b, gi, nd=a.ndim: (0,) * nd)

    def out(rows, cols):
        return (jax.ShapeDtypeStruct((bsz, g, rows, cols), BF16),
                pl.BlockSpec((None, None, rows, cols), lambda b, gi: (b, gi, 0, 0)))

    outs = [out(nrow, dh), out(nrow, dh), out(s, dh), out(dh, s), out(s, dh), out(dh, s)]
    return pl.pallas_call(
        _nsa_prep_kernel,
        out_shape=tuple(o[0] for o in outs),
        grid=(bsz, g),
        in_specs=[col(0), col(1), col(2), col(3), col(4), col(5), full(pe), full(w1), full(w2), full(cos), full(sin)],
        out_specs=tuple(o[1] for o in outs),
        compiler_params=_cparams(("parallel", "parallel")),
        name="nsa_prep",
    )(main, main, main, main, main, main, pe, w1, w2, cos, sin)


def _nsa_attn_kernel(q_ref, gl_ref, cos_ref, sin_ref, kc_ref, vc_ref, ks_ref, vst_ref, kw_ref, vwt_ref,
                     cover_ref, o_ref, qrot_scr, selb_scr, m_scr, l_scr, acc_scr, out_scr, s_scr, p_scr,
                     *, tq, tk, n_sel):
    qi = pl.program_id(2)
    dh = NSA_HEAD_DIM
    scale = dh ** -0.5
    pos_q = qi * tq + lax.broadcasted_iota(jnp.int32, (1, tq), 1)
    cos = cos_ref[...]
    sin = sin_ref[...]
    gates = jax.nn.sigmoid(gl_ref[...].T[:4 * NSA_HPG, :])

    crow = lax.broadcasted_iota(jnp.int32, (LANES, 1), 0)
    cmask = crow * CMP_STRIDE + (CMP_BLOCK - 1) <= pos_q
    kc = kc_ref[...]
    vct = vc_ref[...].astype(F32).T.astype(BF16)
    p_sum = jnp.zeros((LANES, tq), F32)
    for r in range(NSA_HPG):
        qt = q_ref[:, r * dh:(r + 1) * dh].T
        sc = jnp.where(cmask, _dot(kc, qt.astype(BF16)) * scale, NEG_INF)
        e = jnp.exp(sc - jnp.max(sc, axis=0, keepdims=True))
        p = jnp.where(cmask, e * (1.0 / jnp.sum(e, axis=0, keepdims=True)), 0.0)
        p_sum = p_sum + p
        out_scr[r] = gates[3 * r:3 * r + 1, :] * _dot(vct, p.astype(BF16))
        rot = jnp.concatenate([qt[dh // 2:], qt[:dh // 2]], axis=0)
        qrot_scr[r] = (qt * cos + rot * sin).astype(BF16)

    imp = jnp.dot(cover_ref[...], p_sum, precision=HIGHEST, preferred_element_type=F32)[:n_sel]
    jrow = lax.broadcasted_iota(jnp.int32, (n_sel, 1), 0)
    q_blk = pos_q // SEL_BLOCK
    causal = jrow <= q_blk
    forced = (jrow == 0) | (causal & (jrow > q_blk - N_LOCAL_BLOCKS))
    imp = jnp.where(forced, FORCE_SCORE, jnp.where(causal, imp, -1.0))
    rank = jnp.zeros((n_sel, tq), F32)
    for jp in range(n_sel):
        other = imp[jp:jp + 1, :]
        tie = jnp.where(jrow > jp, 1.0, 0.0)
        rank = rank + jnp.where(other > imp, 1.0, jnp.where(other == imp, tie, 0.0))
    keep = (rank < float(min(SEL_TOPK, n_sel))) & (imp >= 0.0)
    selb_scr[...] = jnp.where(keep, 0.0, NEG_INF)

    def reset():
        m_scr[...] = jnp.full(m_scr.shape, NEG_INF, F32)
        l_scr[...] = jnp.zeros(l_scr.shape, F32)
        acc_scr[...] = jnp.zeros(acc_scr.shape, F32)

    def flash_step(k_ref, vt_ref, k0, bias):
        kt = k_ref[pl.ds(k0, tk), :]
        vt = vt_ref[:, pl.ds(k0, tk)]
        for r in range(NSA_HPG):
            s_scr[r] = _dot(kt, qrot_scr[r])
        alphas = []
        for r in range(NSA_HPG):
            sc = s_scr[r] * scale
            if bias is not None:
                sc = sc + bias
            m_old = m_scr[r]
            m_new = jnp.maximum(m_old, jnp.max(sc, axis=0, keepdims=True))
            alpha = jnp.exp(m_old - m_new)
            p = jnp.exp(sc - m_new)
            l_scr[r] = alpha * l_scr[r] + jnp.sum(p, axis=0, keepdims=True)
            p_scr[r] = p.astype(BF16)
            m_scr[r] = m_new
            alphas.append(alpha)
        for r in range(NSA_HPG):
            acc_scr[r] = alphas[r] * acc_scr[r] + _dot(vt, p_scr[r])

    def finish(branch):
        for r in range(NSA_HPG):
            w = gates[3 * r + branch:3 * r + branch + 1, :] / l_scr[r]
            out_scr[r] = out_scr[r] + w * acc_scr[r]

    def sel_bias(kt):
        per_tile = tk // SEL_BLOCK
        rows = [selb_scr[pl.ds(kt * per_tile + j, 1), :] for j in range(per_tile)]
        return jnp.concatenate([jnp.broadcast_to(row, (SEL_BLOCK, tq)) for row in rows], axis=0)

    def key_pos(kt):
        return kt * tk + lax.broadcasted_iota(jnp.int32, (tk, 1), 0)

    diag0 = pl.multiple_of(qi * tk, tk)
    causal_bias = jnp.where(key_pos(qi) <= pos_q, 0.0, NEG_INF)

    reset()

    def sel_body(kt, carry):
        flash_step(ks_ref, vst_ref, pl.multiple_of(kt * tk, tk), sel_bias(kt))
        return carry

    lax.fori_loop(0, qi, sel_body, 0)
    flash_step(ks_ref, vst_ref, diag0, sel_bias(qi) + causal_bias)
    finish(1)

    reset()
    n_back = WINDOW // tk
    for back in range(n_back, 0, -1):
        @pl.when(qi >= back)
        def _(back=back):
            bias = jnp.where(pos_q - key_pos(qi - back) < WINDOW, 0.0, NEG_INF) if back == n_back else None
            flash_step(kw_ref, vwt_ref, pl.multiple_of((qi - back) * tk, tk), bias)
    flash_step(kw_ref, vwt_ref, diag0, causal_bias)
    finish(2)

    for r in range(NSA_HPG):
        o_ref[:, r * dh:(r + 1) * dh] = out_scr[r].T.astype(o_ref.dtype)


def _nsa_attn_call(main, gl, cos_t, sin_t, kc, vc, ks, vst, kw, vwt, cover_t):
    bsz, s, _ = main.shape
    g, r, dh = NSA_KV_GROUPS, NSA_HPG, NSA_HEAD_DIM
    tq = tk = 256
    assert WINDOW % tk == 0 and tk % SEL_BLOCK == 0 and s % tq == 0
    nrow = kc.shape[2]
    n_sel = s // SEL_BLOCK

    def kv(rows, cols):
        return pl.BlockSpec((None, None, rows, cols), lambda b, gi, qi: (b, gi, 0, 0))

    return pl.pallas_call(
        functools.partial(_nsa_attn_kernel, tq=tq, tk=tk, n_sel=n_sel),
        out_shape=jax.ShapeDtypeStruct((bsz, s, NSA_Q_DIM), BF16),
        grid=(bsz, g, s // tq),
        in_specs=[
            pl.BlockSpec((None, tq, r * dh), lambda b, gi, qi: (b, qi, gi)),
            pl.BlockSpec((None, tq, LANES), lambda b, gi, qi: (b, qi, gi)),
            pl.BlockSpec((dh, tq), lambda b, gi, qi: (0, qi)),
            pl.BlockSpec((dh, tq), lambda b, gi, qi: (0, qi)),
            kv(nrow, dh), kv(nrow, dh), kv(s, dh), kv(dh, s), kv(s, dh), kv(dh, s),
            pl.BlockSpec(cover_t.shape, lambda b, gi, qi: (0, 0)),
        ],
        out_specs=pl.BlockSpec((None, tq, r * dh), lambda b, gi, qi: (b, qi, gi)),
        scratch_shapes=[
            pltpu.VMEM((r, dh, tq), BF16),
            pltpu.VMEM((n_sel, tq), F32),
            pltpu.VMEM((r, 1, tq), F32),
            pltpu.VMEM((r, 1, tq), F32),
            pltpu.VMEM((r, dh, tq), F32),
            pltpu.VMEM((r, dh, tq), F32),
            pltpu.VMEM((r, tk, tq), F32),
            pltpu.VMEM((r, tk, tq), BF16),
        ],
        compiler_params=_cparams(("parallel", "parallel", "arbitrary")),
        name="nsa_attn",
    )(main, gl, cos_t, sin_t, kc, vc, ks, vst, kw, vwt, cover_t)


def _nsa_tables(s):
    half = NSA_HEAD_DIM // 2
    freqs = ROPE_THETA ** (-jnp.arange(half, dtype=F32) / half)
    ang = jnp.arange(s).astype(F32)[:, None] * freqs[None, :]
    cos, sin = jnp.cos(ang), jnp.sin(ang)
    cos_full = jnp.concatenate([cos, cos], axis=-1)
    sin_signed = jnp.concatenate([-sin, sin], axis=-1)
    n_cmp = (s - CMP_BLOCK) // CMP_STRIDE + 1
    cs = np.arange(LANES)[None, :] * CMP_STRIDE
    js = np.arange(LANES)[:, None]
    cover_t = ((cs < (js + 1) * SEL_BLOCK) & (cs + CMP_BLOCK > js * SEL_BLOCK)
               & (np.arange(LANES)[None, :] < n_cmp) & (js < s // SEL_BLOCK)).astype(np.float32)
    return cos_full, sin_signed, cos_full.T, sin_signed.T, jnp.asarray(cover_t, F32)


def _nsa_layer(x, mod, ng, w_in, cmp_pos, cmp_w1, cmp_w2, w_out, tables):
    cos, sin, cos_t, sin_t, cover_t = tables
    g, r = NSA_KV_GROUPS, NSA_HPG
    w_main = w_in[:, :NSA_MAIN_DIM].astype(BF16)
    w_gate = w_in[:, NSA_MAIN_DIM:].reshape(D_MODEL, g, 3 * r)
    w_gate = jnp.pad(w_gate, ((0, 0), (0, 0), (0, LANES - 3 * r))).reshape(D_MODEL, g * LANES).astype(BF16)
    main, gl = _proj_call(x, mod, ng, w_main, w_gate, mrow=3, grow=2)
    kc, vc, ks, vst, kw, vwt = _nsa_prep_call(main, cmp_pos, cmp_w1.astype(BF16), cmp_w2.astype(BF16), cos, sin)
    o = _nsa_attn_call(main, gl, cos_t, sin_t, kc, vc, ks, vst, kw, vwt, cover_t)
    return _oproj_call(o, w_out.astype(BF16), x, mod, ng, mrow=5, grow=3)


def _mlstm_kernel(q_ref, k_ref, v_ref, og_ref, gi_ref, gf_ref, bi_ref, bf_ref, cwq_ref, cwk_ref, cbq_ref, cbk_ref,
                  gain_ref, o_ref, qpad_scr, kpad_scr, ct_scr, b_scr, rowi_scr, mloc_scr, bcol_scr, ecol_scr):
    s, dk = q_ref.shape
    dv = v_ref.shape[1]
    chunk = ML_CHUNK
    nc = s // chunk
    pad = qpad_scr.shape[0] - s

    for src, dst in ((q_ref, qpad_scr), (k_ref, kpad_scr)):
        dst[:pad] = jnp.zeros((pad, dk), F32)
        dst[pad:] = src[...]

    def conv_silu(pad_ref, w_ref, b_ref, r0):
        win = pad_ref[pl.ds(r0, chunk + pad), :]
        acc = win[pad:] * w_ref[ML_CONV - 1:ML_CONV, :] + b_ref[...]
        for d in range(1, ML_CONV):
            acc = acc + pltpu.roll(win, d, 0)[pad:] * w_ref[ML_CONV - 1 - d:ML_CONV - d, :]
        return _silu(acc)

    ig = gi_ref[...] + bi_ref[...]
    fg = gf_ref[...] + bf_ref[...]
    logf = jnp.minimum(fg, 0.0) - jnp.log1p(jnp.exp(-jnp.abs(fg)))
    tri_r = lax.broadcasted_iota(jnp.int32, (chunk, chunk), 0)
    tri_c = lax.broadcasted_iota(jnp.int32, (chunk, chunk), 1)
    upper = jnp.where(tri_r <= tri_c, 1.0, 0.0)
    b = jnp.dot(logf, upper, precision=HIGHEST, preferred_element_type=F32)
    w_end = b[:, chunk - 1:chunk] - b + ig
    m_loc = jnp.max(w_end, axis=-1, keepdims=True)
    e_end = jnp.exp(w_end - m_loc)
    b_scr[...] = b
    rowi_scr[...] = ig - b
    mloc_scr[...] = jnp.broadcast_to(m_loc, mloc_scr.shape)
    eye = jnp.where(tri_r == tri_c, 1.0, 0.0)
    cols = lax.dot_general(eye, jnp.concatenate([b, e_end], axis=0), NT_DIMS, precision=HIGHEST,
                           preferred_element_type=F32)
    for c in range(nc):
        bcol_scr[c] = jnp.broadcast_to(cols[:, c:c + 1], (chunk, LANES))
        ecol_scr[c] = jnp.broadcast_to(cols[:, nc + c:nc + c + 1], (chunk, LANES))

    ct_scr[...] = jnp.zeros(ct_scr.shape, F32)
    lower = tri_c <= tri_r
    gain = gain_ref[...]

    def body(c, m_prev):
        r0 = pl.multiple_of(c * chunk, chunk)
        q = conv_silu(qpad_scr, cwq_ref, cbq_ref, r0).astype(BF16)
        k = conv_silu(kpad_scr, cwk_ref, cbk_ref, r0) * (dk ** -0.5)
        va = jnp.concatenate([v_ref[pl.ds(r0, chunk), :].astype(BF16), jnp.ones((chunk, LANES), BF16)], axis=1)
        b_row = b_scr[pl.ds(c, 1), :]
        b_tot = b_row[:, chunk - 1:chunk]
        m_l = mloc_scr[pl.ds(c, 1), :][:, :1]
        bcol = bcol_scr[c]
        log_intra = jnp.where(lower, bcol[:, :chunk] + rowi_scr[pl.ds(c, 1), :], NEG_INF)
        log_inter = bcol[:, :1] + m_prev
        m_t = jnp.maximum(log_inter, jnp.max(log_intra, axis=-1, keepdims=True))
        e_inter = jnp.exp(log_inter - m_t)
        qk = _dot_nt(q, k.astype(BF16)) * jnp.exp(log_intra - m_t)
        ct = ct_scr[...]
        tot = _dot(qk.astype(BF16), va) + e_inter * _dot(q, ct.astype(BF16))
        h = tot[:, :dv] / jnp.maximum(jnp.abs(tot[:, dv:dv + 1]), jnp.exp(-m_t))
        hn = _rms(h) * gain
        o_ref[pl.ds(r0, chunk), :] = (jax.nn.sigmoid(og_ref[pl.ds(r0, chunk), :]) * hn).astype(o_ref.dtype)
        m_new = jnp.maximum(b_tot + m_prev, m_l)
        c_loc = lax.dot_general((k * ecol_scr[c]).astype(BF16), va, TN_DIMS, preferred_element_type=F32)
        ct_scr[...] = jnp.exp(b_tot + m_prev - m_new) * ct + jnp.exp(m_l - m_new) * c_loc
        return m_new

    lax.fori_loop(0, nc, body, jnp.zeros((1, 1), F32), unroll=4)


def _mlstm_call(main, gates_t, gate_b, conv_w, conv_b, gain):
    bsz, s, _ = main.shape
    h, dk, dv = ML_HEADS, ML_QK_DIM, ML_V_DIM
    chunk = ML_CHUNK
    nc = s // chunk
    vblk0 = ML_QK_COLS // dv

    def gate(off):
        return pl.BlockSpec((None, None, nc, chunk), lambda b, hi: (b, off + hi, 0, 0))

    def bias(off):
        return pl.BlockSpec((None, 1, 1), lambda b, hi: (off + hi, 0, 0))

    return pl.pallas_call(
        _mlstm_kernel,
        out_shape=jax.ShapeDtypeStruct((bsz, s, h * dv), BF16),
        grid=(bsz, h),
        in_specs=[
            pl.BlockSpec((None, s, dk), lambda b, hi: (b, 0, hi)),
            pl.BlockSpec((None, s, dk), lambda b, hi: (b, 0, h + hi)),
            pl.BlockSpec((None, s, dv), lambda b, hi: (b, 0, vblk0 + hi)),
            pl.BlockSpec((None, s, dv), lambda b, hi: (b, 0, vblk0 + h + hi)),
            gate(0), gate(h), bias(0), bias(h),
            pl.BlockSpec((ML_CONV, dk), lambda b, hi: (0, hi)),
            pl.BlockSpec((ML_CONV, dk), lambda b, hi: (0, h + hi)),
            pl.BlockSpec((1, dk), lambda b, hi: (0, hi)),
            pl.BlockSpec((1, dk), lambda b, hi: (0, h + hi)),
            pl.BlockSpec((1, dv), lambda b, hi: (0, hi)),
        ],
        out_specs=pl.BlockSpec((None, s, dv), lambda b, hi: (b, 0, hi)),
        scratch_shapes=[
            pltpu.VMEM((s + SUBLANES, dk), F32),
            pltpu.VMEM((s + SUBLANES, dk), F32),
            pltpu.VMEM((dk, dv + LANES), F32),
            pltpu.VMEM((nc, chunk), F32),
            pltpu.VMEM((nc, chunk), F32),
            pltpu.VMEM((nc, LANES), F32),
            pltpu.VMEM((nc, chunk, LANES), F32),
            pltpu.VMEM((nc, chunk, LANES), F32),
        ],
        compiler_params=_cparams(("parallel", "parallel")),
        name="mlstm",
    )(main, main, main, main, gates_t, gates_t, gate_b, gate_b, conv_w, conv_w, conv_b, conv_b, gain)


def _mlstm_layer(x, mod, ng, w_in, conv_w, conv_b, gate_b, mh_gain, w_out):
    bsz, s, _ = x.shape
    ngate = 2 * ML_HEADS
    w_main = w_in[:, :ML_MAIN_DIM].astype(BF16)
    w_gate = jnp.pad(w_in[:, ML_MAIN_DIM:], ((0, 0), (0, LANES - ngate))).astype(BF16)
    main, gl = _proj_call(x, mod, ng, w_main, w_gate, mrow=3, grow=2)
    gates_t = jnp.swapaxes(gl[:, :, :ngate], 1, 2).reshape(bsz, ngate, s // ML_CHUNK, ML_CHUNK)
    o = _mlstm_call(main, gates_t, gate_b.reshape(ngate, 1, 1), conv_w, conv_b.reshape(1, -1),
                    mh_gain.reshape(1, -1))
    return _oproj_call(o, w_out.astype(BF16), x, mod, ng, mrow=5, grow=3)


def kernel(x, c, mod_w, mod_b, norm_g, ffn_pre_w_in, ffn_pre_w_out, ffn_post_w_in, ffn_post_w_out, nsa_w_in, nsa_cmp_pos, nsa_cmp_w1, nsa_cmp_w2, nsa_w_out, ml_w_in, ml_conv_w, ml_conv_b, ml_gate_b, ml_mh_gain, ml_w_out):
    bsz, s, d = x.shape
    depth = mod_w.shape[0]
    mods = _mod_call(c, mod_w, mod_b).reshape(depth, bsz, N_MOD, d)
    tables = _nsa_tables(s)
    for i in range(depth):
        mod, ng = mods[i], norm_g[i]
        x = _ffn_call(x, mod, ng, ffn_pre_w_in[i].astype(BF16), ffn_pre_w_out[i].astype(BF16), mrow=0, grow=0)
        j = i // 2
        if i % 2 == 0:
            x = _nsa_layer(x, mod, ng, nsa_w_in[j], nsa_cmp_pos[j], nsa_cmp_w1[j], nsa_cmp_w2[j], nsa_w_out[j],
                           tables)
        else:
            x = _mlstm_layer(x, mod, ng, ml_w_in[j], ml_conv_w[j], ml_conv_b[j], ml_gate_b[j], ml_mh_gain[j],
                             ml_w_out[j])
        x = _ffn_call(x, mod, ng, ffn_post_w_in[i].astype(BF16), ffn_post_w_out[i].astype(BF16), mrow=6, grow=4)
    return x
```

```python
import functools

import numpy as np
import jax
import jax.numpy as jnp
from jax import lax
from jax.experimental import pallas as pl
from jax.experimental.pallas import tpu as pltpu

F32 = jnp.float32
BF16 = jnp.bfloat16
HIGHEST = lax.Precision.HIGHEST

D_MODEL = 2048
DEPTH = 4
D_FF = 5632
FFN_RES = 0.5
N_MOD = 9
RMS_EPS = 1e-6
NEG_INF = -1e30
LOG2_E = 1.4426950408889634

NSA_HEADS = 16
NSA_HEAD_DIM = 128
NSA_KV_GROUPS = 4
NSA_HPG = NSA_HEADS // NSA_KV_GROUPS
NSA_Q_DIM = NSA_HEADS * NSA_HEAD_DIM
NSA_KV_DIM = NSA_KV_GROUPS * NSA_HEAD_DIM
NSA_MAIN_DIM = NSA_Q_DIM + 6 * NSA_KV_DIM
CMP_BLOCK = 32
CMP_STRIDE = 16
CMP_HIDDEN = 512
SEL_BLOCK = 64
SEL_TOPK = 16
N_LOCAL_BLOCKS = 2
FORCE_SCORE = 1e9
WINDOW = 512
ROPE_THETA = 10000.0

ML_HEADS = 8
ML_QK_DIM = 128
ML_V_DIM = 256
ML_CONV = 4
ML_CHUNK = 64
ML_QK_COLS = 2 * ML_HEADS * ML_QK_DIM
ML_MAIN_DIM = ML_QK_COLS + 2 * ML_HEADS * ML_V_DIM

LANES = 128
SUBLANES = 8
VMEM_LIMIT = 56 * 1024 * 1024

NT_DIMS = (((1,), (1,)), ((), ()))
TN_DIMS = (((0,), (0,)), ((), ()))


def _cparams(sem):
    return pltpu.CompilerParams(dimension_semantics=sem, vmem_limit_bytes=VMEM_LIMIT)


def _silu(x):
    return x * jax.nn.sigmoid(x)


def _rms(x):
    return x * lax.rsqrt(jnp.mean(x * x, axis=-1, keepdims=True) + RMS_EPS)


def _dot(a, b):
    return jnp.dot(a, b, preferred_element_type=F32)


def _dot_nt(a, b):
    return lax.dot_general(a, b, NT_DIMS, preferred_element_type=F32)


def _mod_kernel(c_ref, w_ref, b_ref, o_ref):
    ca = _silu(c_ref[...]).astype(BF16)
    o_ref[...] = _dot(ca, w_ref[...].astype(BF16)) + b_ref[...]


def _mod_call(c, mod_w, mod_b):
    depth, d, n = mod_w.shape
    bsz = c.shape[0]
    tn = 1024
    return pl.pallas_call(
        _mod_kernel,
        out_shape=jax.ShapeDtypeStruct((depth, bsz, n), F32),
        grid=(depth, n // tn),
        in_specs=[
            pl.BlockSpec((bsz, d), lambda i, j: (0, 0)),
            pl.BlockSpec((None, d, tn), lambda i, j: (i, 0, j)),
            pl.BlockSpec((None, 1, tn), lambda i, j: (i, 0, j)),
        ],
        out_specs=pl.BlockSpec((None, bsz, tn), lambda i, j: (i, 0, j)),
        compiler_params=_cparams(("parallel", "parallel")),
        name="mod",
    )(c, mod_w, mod_b.reshape(depth, 1, n))


ROW_CHUNK = 16
ROW_UNROLL = 4


def _row_loop(n_rows, body):
    def step(i, carry):
        body(pl.ds(pl.multiple_of(i * ROW_CHUNK, ROW_CHUNK), ROW_CHUNK))
        return carry
    lax.fori_loop(0, n_rows // ROW_CHUNK, step, 0, unroll=ROW_UNROLL)


def _prenorm_to(xn_ref, x_ref, mod_ref, ng_ref, mrow, grow):
    gain = ng_ref[grow:grow + 1, :] * (1.0 + mod_ref[mrow + 1:mrow + 2, :])
    shift = mod_ref[mrow:mrow + 1, :]

    def body(rows):
        xn_ref[rows, :] = (_rms(x_ref[rows, :]) * gain + shift).astype(xn_ref.dtype)

    _row_loop(x_ref.shape[0], body)


def _postnorm_residual_to(o_ref, y_ref, x_ref, mod_ref, ng_ref, mrow, grow, res):
    gain = ng_ref[grow:grow + 1, :] * (res * mod_ref[mrow:mrow + 1, :])

    def body(rows):
        o_ref[rows, :] = x_ref[rows, :] + _rms(y_ref[rows, :]) * gain

    _row_loop(x_ref.shape[0], body)


def _ffn_kernel(x_ref, mod_ref, ng_ref, wg_ref, wu_ref, wo_ref, o_ref, xn_ref, acc_ref, *, mrow, grow):
    j = pl.program_id(2)

    @pl.when(j == 0)
    def _():
        _prenorm_to(xn_ref, x_ref, mod_ref, ng_ref, mrow, grow)
        acc_ref[...] = jnp.zeros_like(acc_ref)

    xn = xn_ref[...]
    g = _dot(xn, wg_ref[...])
    u = _dot(xn, wu_ref[...])
    acc_ref[...] += _dot((_silu(g) * u).astype(BF16), wo_ref[...])

    @pl.when(j == pl.num_programs(2) - 1)
    def _():
        _postnorm_residual_to(o_ref, acc_ref, x_ref, mod_ref, ng_ref, mrow + 2, grow + 1, FFN_RES)


def _ffn_call(x, mod, ng, w_in, w_out, *, layer, mrow, grow):
    bsz, s, d = x.shape
    dff = w_out.shape[1]
    tm, tf = 512, 512
    nf = dff // tf
    return pl.pallas_call(
        functools.partial(_ffn_kernel, mrow=mrow, grow=grow),
        out_shape=jax.ShapeDtypeStruct(x.shape, F32),
        grid=(bsz, s // tm, nf),
        in_specs=[
            pl.BlockSpec((None, tm, d), lambda b, m, j: (b, m, 0)),
            pl.BlockSpec((None, N_MOD, d), lambda b, m, j: (b, 0, 0)),
            pl.BlockSpec(ng.shape, lambda b, m, j: (0, 0)),
            pl.BlockSpec((None, d, tf), lambda b, m, j: (layer, 0, j)),
            pl.BlockSpec((None, d, tf), lambda b, m, j: (layer, 0, nf + j)),
            pl.BlockSpec((None, tf, d), lambda b, m, j: (layer, j, 0)),
        ],
        out_specs=pl.BlockSpec((None, tm, d), lambda b, m, j: (b, m, 0)),
        scratch_shapes=[pltpu.VMEM((tm, d), BF16), pltpu.VMEM((tm, d), F32)],
        compiler_params=_cparams(("parallel", "parallel", "arbitrary")),
        name="ffn",
    )(x, mod, ng, w_in, w_in, w_out)


def _proj_kernel(x_ref, mod_ref, ng_ref, w_ref, wgate_ref, o_ref, og_ref, xn_ref, *, mrow, grow):
    @pl.when(pl.program_id(2) == 0)
    def _():
        _prenorm_to(xn_ref, x_ref, mod_ref, ng_ref, mrow, grow)
        og_ref[...] = _dot(xn_ref[...], wgate_ref[...])

    o_ref[...] = _dot(xn_ref[...], w_ref[...])


def _proj_call(x, mod, ng, w_in, w_gate, *, layer, n, mrow, grow):
    bsz, s, d = x.shape
    ngate = w_gate.shape[1]
    tm, tn = 1024, 1024
    return pl.pallas_call(
        functools.partial(_proj_kernel, mrow=mrow, grow=grow),
        out_shape=(jax.ShapeDtypeStruct((bsz, s, n), F32), jax.ShapeDtypeStruct((bsz, s, ngate), F32)),
        grid=(bsz, s // tm, n // tn),
        in_specs=[
            pl.BlockSpec((None, tm, d), lambda b, m, j: (b, m, 0)),
            pl.BlockSpec((None, N_MOD, d), lambda b, m, j: (b, 0, 0)),
            pl.BlockSpec(ng.shape, lambda b, m, j: (0, 0)),
            pl.BlockSpec((None, d, tn), lambda b, m, j: (layer, 0, j)),
            pl.BlockSpec((d, ngate), lambda b, m, j: (0, 0)),
        ],
        out_specs=(pl.BlockSpec((None, tm, tn), lambda b, m, j: (b, m, j)),
                   pl.BlockSpec((None, tm, ngate), lambda b, m, j: (b, m, 0))),
        scratch_shapes=[pltpu.VMEM((tm, d), BF16)],
        compiler_params=_cparams(("parallel", "parallel", "arbitrary")),
        name="proj",
    )(x, mod, ng, w_in, w_gate)


def _oproj_kernel(a_ref, w_ref, x_ref, mod_ref, ng_ref, o_ref, y_ref, *, mrow, grow):
    y_ref[...] = _dot(a_ref[...], w_ref[...])
    _postnorm_residual_to(o_ref, y_ref, x_ref, mod_ref, ng_ref, mrow, grow, 1.0)


def _oproj_call(a, w, x, mod, ng, *, layer, mrow, grow):
    bsz, s, d = x.shape
    k = a.shape[-1]
    tm = 256
    return pl.pallas_call(
        functools.partial(_oproj_kernel, mrow=mrow, grow=grow),
        out_shape=jax.ShapeDtypeStruct(x.shape, F32),
        grid=(bsz, s // tm),
        in_specs=[
            pl.BlockSpec((None, tm, k), lambda b, m: (b, m, 0)),
            pl.BlockSpec((None, k, d), lambda b, m: (layer, 0, 0)),
            pl.BlockSpec((None, tm, d), lambda b, m: (b, m, 0)),
            pl.BlockSpec((None, N_MOD, d), lambda b, m: (b, 0, 0)),
            pl.BlockSpec(ng.shape, lambda b, m: (0, 0)),
        ],
        out_specs=pl.BlockSpec((None, tm, d), lambda b, m: (b, m, 0)),
        scratch_shapes=[pltpu.VMEM((tm, d), F32)],
        compiler_params=_cparams(("parallel", "parallel")),
        name="oproj",
    )(a, w, x, mod, ng)


def _rope(x, cos, sin_signed):
    return x * cos + pltpu.roll(x, NSA_HEAD_DIM // 2, 1) * sin_signed


def _nsa_prep_kernel(kc_ref, vc_ref, ks_ref, vs_ref, kw_ref, vw_ref, pe_ref, w1_ref, w2_ref, cos_ref, sin_ref,
                     kco_ref, vco_ref, kso_ref, vso_ref, kwo_ref, vwo_ref):
    s = kc_ref.shape[0]
    nrow = s // CMP_STRIDE
    half = CMP_BLOCK // CMP_STRIDE

    def compress(t_ref, idx):
        streams = [t_ref[pl.ds(j, nrow, stride=CMP_STRIDE), :] for j in range(CMP_STRIDE)]
        hidden = None
        for h in range(half):
            flat = jnp.concatenate(
                [(streams[j] + pe_ref[idx, h * CMP_STRIDE + j:h * CMP_STRIDE + j + 1, :]).astype(BF16)
                 for j in range(CMP_STRIDE)], axis=1)
            w = w1_ref[idx, h * CMP_STRIDE * NSA_HEAD_DIM:(h + 1) * CMP_STRIDE * NSA_HEAD_DIM, :]
            part = _dot(flat, w)
            if h:
                part = pltpu.roll(part, nrow - h, 0)
            hidden = part if hidden is None else hidden + part
        return _dot(_silu(hidden).astype(BF16), w2_ref[idx])

    kco_ref[...] = compress(kc_ref, 0).astype(kco_ref.dtype)
    vco_ref[...] = compress(vc_ref, 1).astype(vco_ref.dtype)
    cos = cos_ref[...]
    sin = sin_ref[...]
    kso_ref[...] = _rope(ks_ref[...], cos, sin).astype(BF16)
    kwo_ref[...] = _rope(kw_ref[...], cos, sin).astype(BF16)
    vso_ref[...] = vs_ref[...].T.astype(BF16)
    vwo_ref[...] = vw_ref[...].T.astype(BF16)


def _nsa_prep_call(main, pe, w1, w2, cos, sin, *, layer):
    bsz, s, _ = main.shape
    g, dh = NSA_KV_GROUPS, NSA_HEAD_DIM
    nrow = s // CMP_STRIDE
    q_blocks = NSA_Q_DIM // dh

    def col(k):
        return pl.BlockSpec((None, s, dh), lambda b, gi, k=k: (b, 0, q_blocks + k * g + gi))

    def full(a):
        return pl.BlockSpec(a.shape, lambda b, gi, nd=a.ndim: (0,) * nd)

    def stacked(a):
        return pl.BlockSpec((None,) + a.shape[1:], lambda b, gi, nd=a.ndim: (layer,) + (0,) * (nd - 1))

    def out(rows, cols):
        return (jax.ShapeDtypeStruct((bsz, g, rows, cols), BF16),
                pl.BlockSpec((None, None, rows, cols), lambda b, gi: (b, gi, 0, 0)))

    outs = [out(nrow, dh), out(nrow, dh), out(s, dh), out(dh, s), out(s, dh), out(dh, s)]
    return pl.pallas_call(
        _nsa_prep_kernel,
        out_shape=tuple(o[0] for o in outs),
        grid=(bsz, g),
        in_specs=[col(0), col(1), col(2), col(3), col(4), col(5), stacked(pe), stacked(w1), stacked(w2),
                  full(cos), full(sin)],
        out_specs=tuple(o[1] for o in outs),
        compiler_params=_cparams(("parallel", "parallel")),
        name="nsa_prep",
    )(main, main, main, main, main, main, pe, w1, w2, cos, sin)


def _nsa_attn_kernel(q_ref, gl_ref, cos_ref, sin_ref, kc_ref, vc_ref, ks_ref, vst_ref, kw_ref, vwt_ref,
                     cover_ref, o_ref, qrot_scr, selb_scr, m_scr, l_scr, acc_scr, out_scr, s_scr, p_scr,
                     *, tq, tk, n_sel):
    qi = pl.program_id(2)
    dh = NSA_HEAD_DIM
    scale = dh ** -0.5 * LOG2_E
    pos_q = qi * tq + lax.broadcasted_iota(jnp.int32, (1, tq), 1)
    cos = cos_ref[...]
    sin = sin_ref[...]
    gates = jax.nn.sigmoid(gl_ref[...].T[:4 * NSA_HPG, :])

    crow = lax.broadcasted_iota(jnp.int32, (LANES, 1), 0)
    cmask = crow * CMP_STRIDE + (CMP_BLOCK - 1) <= pos_q
    kc = kc_ref[...]
    vct = vc_ref[...].astype(F32).T.astype(BF16)
    p_sum = jnp.zeros((LANES, tq), F32)
    for r in range(NSA_HPG):
        qt = q_ref[:, r * dh:(r + 1) * dh].T
        sc = jnp.where(cmask, _dot(kc, qt.astype(BF16)) * scale, NEG_INF)
        e = jnp.exp2(sc - jnp.max(sc, axis=0, keepdims=True))
        p = jnp.where(cmask, e * (1.0 / jnp.sum(e, axis=0, keepdims=True)), 0.0)
        p_sum = p_sum + p
        out_scr[r] = gates[3 * r:3 * r + 1, :] * _dot(vct, p.astype(BF16))
        rot = jnp.concatenate([qt[dh // 2:], qt[:dh // 2]], axis=0)
        qrot_scr[r] = (qt * cos + rot * sin).astype(BF16)

    imp = jnp.dot(cover_ref[...], p_sum, precision=HIGHEST, preferred_element_type=F32)[:n_sel]
    jrow = lax.broadcasted_iota(jnp.int32, (n_sel, 1), 0)
    q_blk = pos_q // SEL_BLOCK
    causal = jrow <= q_blk
    forced = (jrow == 0) | (causal & (jrow > q_blk - N_LOCAL_BLOCKS))
    imp = jnp.where(forced, FORCE_SCORE, jnp.where(causal, imp, -1.0))
    rank = jnp.zeros((n_sel, tq), F32)
    for jp in range(n_sel):
        other = imp[jp:jp + 1, :]
        tie = jnp.where(jrow > jp, 1.0, 0.0)
        rank = rank + jnp.where(other > imp, 1.0, jnp.where(other == imp, tie, 0.0))
    keep = (rank < float(min(SEL_TOPK, n_sel))) & (imp >= 0.0)
    selb_scr[...] = jnp.where(keep, 0.0, NEG_INF)

    def reset():
        m_scr[...] = jnp.full(m_scr.shape, NEG_INF, F32)
        l_scr[...] = jnp.zeros(l_scr.shape, F32)
        acc_scr[...] = jnp.zeros(acc_scr.shape, F32)

    def flash_step(k_ref, vt_ref, k0, bias):
        kt = k_ref[pl.ds(k0, tk), :]
        vt = vt_ref[:, pl.ds(k0, tk)]
        for r in range(NSA_HPG):
            s_scr[r] = _dot(kt, qrot_scr[r])
        alphas = []
        for r in range(NSA_HPG):
            sc = s_scr[r] * scale
            if bias is not None:
                sc = sc + bias
            m_old = m_scr[r]
            m_new = jnp.maximum(m_old, jnp.max(sc, axis=0, keepdims=True))
            alpha = jnp.exp2(m_old - m_new)
            p = jnp.exp2(sc - m_new)
            l_scr[r] = alpha * l_scr[r] + jnp.sum(p, axis=0, keepdims=True)
            p_scr[r] = p.astype(BF16)
            m_scr[r] = m_new
            alphas.append(alpha)
        for r in range(NSA_HPG):
            acc_scr[r] = alphas[r] * acc_scr[r] + _dot(vt, p_scr[r])

    def finish(branch):
        for r in range(NSA_HPG):
            w = gates[3 * r + branch:3 * r + branch + 1, :] / l_scr[r]
            out_scr[r] = out_scr[r] + w * acc_scr[r]

    def sel_bias(kt):
        per_tile = tk // SEL_BLOCK
        rows = [selb_scr[pl.ds(kt * per_tile + j, 1), :] for j in range(per_tile)]
        return jnp.concatenate([jnp.broadcast_to(row, (SEL_BLOCK, tq)) for row in rows], axis=0)

    def key_pos(kt):
        return kt * tk + lax.broadcasted_iota(jnp.int32, (tk, 1), 0)

    diag0 = pl.multiple_of(qi * tk, tk)
    causal_bias = jnp.where(key_pos(qi) <= pos_q, 0.0, NEG_INF)

    reset()

    def sel_body(kt, carry):
        flash_step(ks_ref, vst_ref, pl.multiple_of(kt * tk, tk), sel_bias(kt))
        return carry

    lax.fori_loop(0, qi, sel_body, 0)
    flash_step(ks_ref, vst_ref, diag0, sel_bias(qi) + causal_bias)
    finish(1)

    reset()
    n_back = WINDOW // tk
    for back in range(n_back, 0, -1):
        @pl.when(qi >= back)
        def _(back=back):
            bias = jnp.where(pos_q - key_pos(qi - back) < WINDOW, 0.0, NEG_INF) if back == n_back else None
            flash_step(kw_ref, vwt_ref, pl.multiple_of((qi - back) * tk, tk), bias)
    flash_step(kw_ref, vwt_ref, diag0, causal_bias)
    finish(2)

    for r in range(NSA_HPG):
        o_ref[:, r * dh:(r + 1) * dh] = out_scr[r].T.astype(o_ref.dtype)


def _nsa_attn_call(main, gl, cos_t, sin_t, kc, vc, ks, vst, kw, vwt, cover_t):
    bsz, s, _ = main.shape
    g, r, dh = NSA_KV_GROUPS, NSA_HPG, NSA_HEAD_DIM
    tq = tk = 256
    assert WINDOW % tk == 0 and tk % SEL_BLOCK == 0 and s % tq == 0
    nrow = kc.shape[2]
    n_sel = s // SEL_BLOCK

    def kv(rows, cols):
        return pl.BlockSpec((None, None, rows, cols), lambda b, gi, qi: (b, gi, 0, 0))

    return pl.pallas_call(
        functools.partial(_nsa_attn_kernel, tq=tq, tk=tk, n_sel=n_sel),
        out_shape=jax.ShapeDtypeStruct((bsz, s, NSA_Q_DIM), BF16),
        grid=(bsz, g, s // tq),
        in_specs=[
            pl.BlockSpec((None, tq, r * dh), lambda b, gi, qi: (b, qi, gi)),
            pl.BlockSpec((None, tq, LANES), lambda b, gi, qi: (b, qi, gi)),
            pl.BlockSpec((dh, tq), lambda b, gi, qi: (0, qi)),
            pl.BlockSpec((dh, tq), lambda b, gi, qi: (0, qi)),
            kv(nrow, dh), kv(nrow, dh), kv(s, dh), kv(dh, s), kv(s, dh), kv(dh, s),
            pl.BlockSpec(cover_t.shape, lambda b, gi, qi: (0, 0)),
        ],
        out_specs=pl.BlockSpec((None, tq, r * dh), lambda b, gi, qi: (b, qi, gi)),
        scratch_shapes=[
            pltpu.VMEM((r, dh, tq), BF16),
            pltpu.VMEM((n_sel, tq), F32),
            pltpu.VMEM((r, 1, tq), F32),
            pltpu.VMEM((r, 1, tq), F32),
            pltpu.VMEM((r, dh, tq), F32),
            pltpu.VMEM((r, dh, tq), F32),
            pltpu.VMEM((r, tk, tq), F32),
            pltpu.VMEM((r, tk, tq), BF16),
        ],
        compiler_params=_cparams(("parallel", "parallel", "arbitrary")),
        name="nsa_attn",
    )(main, gl, cos_t, sin_t, kc, vc, ks, vst, kw, vwt, cover_t)


def _nsa_tables(s):
    half = NSA_HEAD_DIM // 2
    freqs = ROPE_THETA ** (-jnp.arange(half, dtype=F32) / half)
    ang = jnp.arange(s).astype(F32)[:, None] * freqs[None, :]
    cos, sin = jnp.cos(ang), jnp.sin(ang)
    cos_full = jnp.concatenate([cos, cos], axis=-1)
    sin_signed = jnp.concatenate([-sin, sin], axis=-1)
    n_cmp = (s - CMP_BLOCK) // CMP_STRIDE + 1
    cs = np.arange(LANES)[None, :] * CMP_STRIDE
    js = np.arange(LANES)[:, None]
    cover_t = ((cs < (js + 1) * SEL_BLOCK) & (cs + CMP_BLOCK > js * SEL_BLOCK)
               & (np.arange(LANES)[None, :] < n_cmp) & (js < s // SEL_BLOCK)).astype(np.float32)
    return cos_full, sin_signed, cos_full.T, sin_signed.T, jnp.asarray(cover_t, F32)


def _nsa_layer(x, mod, ng, w_in, cmp_pos, cmp_w1, cmp_w2, w_out, tables, *, layer):
    cos, sin, cos_t, sin_t, cover_t = tables
    g, r = NSA_KV_GROUPS, NSA_HPG
    w_gate = w_in[layer, :, NSA_MAIN_DIM:].reshape(D_MODEL, g, 3 * r)
    w_gate = jnp.pad(w_gate, ((0, 0), (0, 0), (0, LANES - 3 * r))).reshape(D_MODEL, g * LANES)
    main, gl = _proj_call(x, mod, ng, w_in, w_gate, layer=layer, n=NSA_MAIN_DIM, mrow=3, grow=2)
    kc, vc, ks, vst, kw, vwt = _nsa_prep_call(main, cmp_pos, cmp_w1, cmp_w2, cos, sin, layer=layer)
    o = _nsa_attn_call(main, gl, cos_t, sin_t, kc, vc, ks, vst, kw, vwt, cover_t)
    return _oproj_call(o, w_out, x, mod, ng, layer=layer, mrow=5, grow=3)


def _mlstm_kernel(q_ref, k_ref, v_ref, og_ref, gi_ref, gf_ref, bi_ref, bf_ref, cwq_ref, cwk_ref, cbq_ref, cbk_ref,
                  gain_ref, o_ref, qpad_scr, kpad_scr, ct_scr, b_scr, rowi_scr, mloc_scr, bcol_scr, ecol_scr):
    s, dk = q_ref.shape
    dv = v_ref.shape[1]
    chunk = ML_CHUNK
    nc = s // chunk
    pad = qpad_scr.shape[0] - s

    for src, dst in ((q_ref, qpad_scr), (k_ref, kpad_scr)):
        dst[:pad] = jnp.zeros((pad, dk), F32)
        dst[pad:] = src[...]

    def conv_silu(pad_ref, w_ref, b_ref, r0):
        win = pad_ref[pl.ds(r0, chunk + pad), :]
        acc = win[pad:] * w_ref[ML_CONV - 1:ML_CONV, :] + b_ref[...]
        for d in range(1, ML_CONV):
            acc = acc + pltpu.roll(win, d, 0)[pad:] * w_ref[ML_CONV - 1 - d:ML_CONV - d, :]
        return _silu(acc)

    ig = gi_ref[...] + bi_ref[...]
    fg = gf_ref[...] + bf_ref[...]
    logf = jnp.minimum(fg, 0.0) - jnp.log1p(jnp.exp(-jnp.abs(fg)))
    tri_r = lax.broadcasted_iota(jnp.int32, (chunk, chunk), 0)
    tri_c = lax.broadcasted_iota(jnp.int32, (chunk, chunk), 1)
    upper = jnp.where(tri_r <= tri_c, 1.0, 0.0)
    b = jnp.dot(logf, upper, precision=HIGHEST, preferred_element_type=F32)
    w_end = b[:, chunk - 1:chunk] - b + ig
    m_loc = jnp.max(w_end, axis=-1, keepdims=True)
    e_end = jnp.exp(w_end - m_loc)
    b_scr[...] = b
    rowi_scr[...] = ig - b
    mloc_scr[...] = jnp.broadcast_to(m_loc, mloc_scr.shape)
    eye = jnp.where(tri_r == tri_c, 1.0, 0.0)
    cols = lax.dot_general(eye, jnp.concatenate([b, e_end], axis=0), NT_DIMS, precision=HIGHEST,
                           preferred_element_type=F32)
    for c in range(nc):
        bcol_scr[c] = jnp.broadcast_to(cols[:, c:c + 1], (chunk, LANES))
        ecol_scr[c] = jnp.broadcast_to(cols[:, nc + c:nc + c + 1], (chunk, LANES))

    ct_scr[...] = jnp.zeros(ct_scr.shape, F32)
    lower = tri_c <= tri_r
    gain = gain_ref[...]

    def body(c, m_prev):
        r0 = pl.multiple_of(c * chunk, chunk)
        q = conv_silu(qpad_scr, cwq_ref, cbq_ref, r0).astype(BF16)
        k = conv_silu(kpad_scr, cwk_ref, cbk_ref, r0) * (dk ** -0.5)
        va = jnp.concatenate([v_ref[pl.ds(r0, chunk), :].astype(BF16), jnp.ones((chunk, LANES), BF16)], axis=1)
        b_row = b_scr[pl.ds(c, 1), :]
        b_tot = b_row[:, chunk - 1:chunk]
        m_l = mloc_scr[pl.ds(c, 1), :][:, :1]
        bcol = bcol_scr[c]
        log_intra = jnp.where(lower, bcol[:, :chunk] + rowi_scr[pl.ds(c, 1), :], NEG_INF)
        log_inter = bcol[:, :1] + m_prev
        m_t = jnp.maximum(log_inter, jnp.max(log_intra, axis=-1, keepdims=True))
        e_inter = jnp.exp(log_inter - m_t)
        qk = _dot_nt(q, k.astype(BF16)) * jnp.exp(log_intra - m_t)
        ct = ct_scr[...]
        tot = _dot(qk.astype(BF16), va) + e_inter * _dot(q, ct.astype(BF16))
        h = tot[:, :dv] / jnp.maximum(jnp.abs(tot[:, dv:dv + 1]), jnp.exp(-m_t))
        hn = _rms(h) * gain
        o_ref[pl.ds(r0, chunk), :] = (jax.nn.sigmoid(og_ref[pl.ds(r0, chunk), :]) * hn).astype(o_ref.dtype)
        m_new = jnp.maximum(b_tot + m_prev, m_l)
        c_loc = lax.dot_general((k * ecol_scr[c]).astype(BF16), va, TN_DIMS, preferred_element_type=F32)
        ct_scr[...] = jnp.exp(b_tot + m_prev - m_new) * ct + jnp.exp(m_l - m_new) * c_loc
        return m_new

    lax.fori_loop(0, nc, body, jnp.zeros((1, 1), F32), unroll=4)


def _mlstm_call(main, gates_t, gate_b, conv_w, conv_b, gain):
    bsz, s, _ = main.shape
    h, dk, dv = ML_HEADS, ML_QK_DIM, ML_V_DIM
    chunk = ML_CHUNK
    nc = s // chunk
    vblk0 = ML_QK_COLS // dv

    def gate(off):
        return pl.BlockSpec((None, None, nc, chunk), lambda b, hi: (b, off + hi, 0, 0))

    def bias(off):
        return pl.BlockSpec((None, 1, 1), lambda b, hi: (off + hi, 0, 0))

    return pl.pallas_call(
        _mlstm_kernel,
        out_shape=jax.ShapeDtypeStruct((bsz, s, h * dv), BF16),
        grid=(bsz, h),
        in_specs=[
            pl.BlockSpec((None, s, dk), lambda b, hi: (b, 0, hi)),
            pl.BlockSpec((None, s, dk), lambda b, hi: (b, 0, h + hi)),
            pl.BlockSpec((None, s, dv), lambda b, hi: (b, 0, vblk0 + hi)),
            pl.BlockSpec((None, s, dv), lambda b, hi: (b, 0, vblk0 + h + hi)),
            gate(0), gate(h), bias(0), bias(h),
            pl.BlockSpec((ML_CONV, dk), lambda b, hi: (0, hi)),
            pl.BlockSpec((ML_CONV, dk), lambda b, hi: (0, h + hi)),
            pl.BlockSpec((1, dk), lambda b, hi: (0, hi)),
            pl.BlockSpec((1, dk), lambda b, hi: (0, h + hi)),
            pl.BlockSpec((1, dv), lambda b, hi: (0, hi)),
        ],
        out_specs=pl.BlockSpec((None, s, dv), lambda b, hi: (b, 0, hi)),
        scratch_shapes=[
            pltpu.VMEM((s + SUBLANES, dk), F32),
            pltpu.VMEM((s + SUBLANES, dk), F32),
            pltpu.VMEM((dk, dv + LANES), F32),
            pltpu.VMEM((nc, chunk), F32),
            pltpu.VMEM((nc, chunk), F32),
            pltpu.VMEM((nc, LANES), F32),
            pltpu.VMEM((nc, chunk, LANES), F32),
            pltpu.VMEM((nc, chunk, LANES), F32),
        ],
        compiler_params=_cparams(("parallel", "parallel")),
        name="mlstm",
    )(main, main, main, main, gates_t, gates_t, gate_b, gate_b, conv_w, conv_w, conv_b, conv_b, gain)


def _mlstm_layer(x, mod, ng, w_in, conv_w, conv_b, gate_b, mh_gain, w_out, *, layer):
    bsz, s, _ = x.shape
    ngate = 2 * ML_HEADS
    w_gate = jnp.pad(w_in[layer, :, ML_MAIN_DIM:], ((0, 0), (0, LANES - ngate)))
    main, gl = _proj_call(x, mod, ng, w_in, w_gate, layer=layer, n=ML_MAIN_DIM, mrow=3, grow=2)
    gates_t = jnp.swapaxes(gl[:, :, :ngate], 1, 2).reshape(bsz, ngate, s // ML_CHUNK, ML_CHUNK)
    o = _mlstm_call(main, gates_t, gate_b.reshape(ngate, 1, 1), conv_w, conv_b.reshape(1, -1),
                    mh_gain.reshape(1, -1))
    return _oproj_call(o, w_out, x, mod, ng, layer=layer, mrow=5, grow=3)


def kernel(x, c, mod_w, mod_b, norm_g, ffn_pre_w_in, ffn_pre_w_out, ffn_post_w_in, ffn_post_w_out, nsa_w_in, nsa_cmp_pos, nsa_cmp_w1, nsa_cmp_w2, nsa_w_out, ml_w_in, ml_conv_w, ml_conv_b, ml_gate_b, ml_mh_gain, ml_w_out):
    bsz, s, d = x.shape
    depth = mod_w.shape[0]
    mods = _mod_call(c, mod_w, mod_b).reshape(depth, bsz, N_MOD, d)
    tables = _nsa_tables(s)
    pre_in, pre_out = ffn_pre_w_in.astype(BF16), ffn_pre_w_out.astype(BF16)
    post_in, post_out = ffn_post_w_in.astype(BF16), ffn_post_w_out.astype(BF16)
    nsa_in, nsa_out = nsa_w_in.astype(BF16), nsa_w_out.astype(BF16)
    nsa_w1, nsa_w2 = nsa_cmp_w1.astype(BF16), nsa_cmp_w2.astype(BF16)
    ml_in, ml_out = ml_w_in.astype(BF16), ml_w_out.astype(BF16)
    for i in range(depth):
        mod, ng = mods[i], norm_g[i]
        x = _ffn_call(x, mod, ng, pre_in, pre_out, layer=i, mrow=0, grow=0)
        j = i // 2
        if i % 2 == 0:
            x = _nsa_layer(x, mod, ng, nsa_in, nsa_cmp_pos, nsa_w1, nsa_w2, nsa_out, tables, layer=j)
        else:
            x = _mlstm_layer(x, mod, ng, ml_in, ml_conv_w[j], ml_conv_b[j], ml_gate_b[j], ml_mh_gain[j], ml_out,
                             layer=j)
        x = _ffn_call(x, mod, ng, post_in, post_out, layer=i, mrow=6, grow=4)
    return x
```

```python
import functools

import numpy as np
import jax
import jax.numpy as jnp
from jax import lax
from jax.experimental import pallas as pl
from jax.experimental.pallas import tpu as pltpu

F32 = jnp.float32
BF16 = jnp.bfloat16
HIGHEST = lax.Precision.HIGHEST

D_MODEL = 2048
DEPTH = 4
D_FF = 5632
FFN_RES = 0.5
N_MOD = 9
RMS_EPS = 1e-6
NEG_INF = -1e30
LOG2_E = 1.4426950408889634

NSA_HEADS = 16
NSA_HEAD_DIM = 128
NSA_KV_GROUPS = 4
NSA_HPG = NSA_HEADS // NSA_KV_GROUPS
NSA_Q_DIM = NSA_HEADS * NSA_HEAD_DIM
NSA_KV_DIM = NSA_KV_GROUPS * NSA_HEAD_DIM
NSA_MAIN_DIM = NSA_Q_DIM + 6 * NSA_KV_DIM
CMP_BLOCK = 32
CMP_STRIDE = 16
CMP_HIDDEN = 512
SEL_BLOCK = 64
SEL_TOPK = 16
N_LOCAL_BLOCKS = 2
FORCE_SCORE = 1e9
WINDOW = 512
ROPE_THETA = 10000.0

ML_HEADS = 8
ML_QK_DIM = 128
ML_V_DIM = 256
ML_CONV = 4
ML_CHUNK = 64
ML_QK_COLS = 2 * ML_HEADS * ML_QK_DIM
ML_MAIN_DIM = ML_QK_COLS + 2 * ML_HEADS * ML_V_DIM

LANES = 128
SUBLANES = 8
VMEM_LIMIT = 56 * 1024 * 1024

NT_DIMS = (((1,), (1,)), ((), ()))
TN_DIMS = (((0,), (0,)), ((), ()))


def _cparams(sem):
    return pltpu.CompilerParams(dimension_semantics=sem, vmem_limit_bytes=VMEM_LIMIT)


def _silu(x):
    return x * jax.nn.sigmoid(x)


def _rms(x):
    return x * lax.rsqrt(jnp.mean(x * x, axis=-1, keepdims=True) + RMS_EPS)


def _dot(a, b):
    return jnp.dot(a, b, preferred_element_type=F32)


def _dot_nt(a, b):
    return lax.dot_general(a, b, NT_DIMS, preferred_element_type=F32)


def _mod_kernel(c_ref, w_ref, b_ref, o_ref):
    ca = _silu(c_ref[...]).astype(BF16)
    o_ref[...] = _dot(ca, w_ref[...].astype(BF16)) + b_ref[...]


def _mod_call(c, mod_w, mod_b):
    depth, d, n = mod_w.shape
    bsz = c.shape[0]
    tn = 1024
    return pl.pallas_call(
        _mod_kernel,
        out_shape=jax.ShapeDtypeStruct((depth, bsz, n), F32),
        grid=(depth, n // tn),
        in_specs=[
            pl.BlockSpec((bsz, d), lambda i, j: (0, 0)),
            pl.BlockSpec((None, d, tn), lambda i, j: (i, 0, j)),
            pl.BlockSpec((None, 1, tn), lambda i, j: (i, 0, j)),
        ],
        out_specs=pl.BlockSpec((None, bsz, tn), lambda i, j: (i, 0, j)),
        compiler_params=_cparams(("parallel", "parallel")),
        name="mod",
    )(c, mod_w, mod_b.reshape(depth, 1, n))


def _prenorm(x, mod_ref, ng_ref, mrow, grow):
    gain = ng_ref[grow:grow + 1, :] * (1.0 + mod_ref[mrow + 1:mrow + 2, :])
    return _rms(x) * gain + mod_ref[mrow:mrow + 1, :]


BF16_ROWS = 16


def _ffn_kernel(xnext_ref, xprev_ref, modnext_ref, modprev_ref, ng_ref, wg_ref, wu_ref, wo_ref, o_ref,
                xn_scr, acc_scr, *, mrow, grow, n_tiles, edge_rows):
    t = pl.program_id(0)
    j = pl.program_id(1)
    tm = xnext_ref.shape[0]
    slot = t % 2
    other = 1 - slot
    rows = pl.ds(pl.multiple_of(jnp.minimum(j * edge_rows, tm - edge_rows), BF16_ROWS), edge_rows)

    def finish_prev_rows():
        gain = ng_ref[grow + 1:grow + 2, :] * (FFN_RES * modprev_ref[mrow + 2:mrow + 3, :])
        o_ref[rows, :] = xprev_ref[rows, :] + _rms(acc_scr[other, rows, :]) * gain

    @pl.when((t == 0) & (j == 0))
    def _():
        xn_scr[0] = _prenorm(xprev_ref[...], modprev_ref, ng_ref, mrow, grow).astype(BF16)
        acc_scr[...] = jnp.zeros_like(acc_scr)

    @pl.when(t < n_tiles)
    def _():
        xn = xn_scr[slot]
        g = _dot(xn, wg_ref[...])
        u = _dot(xn, wu_ref[...])
        part = _dot((_silu(g) * u).astype(BF16), wo_ref[...])
        acc_scr[slot] = jnp.where(j == 0, part, acc_scr[slot] + part)
        xn_scr[other, rows, :] = _prenorm(xnext_ref[rows, :], modnext_ref, ng_ref, mrow, grow).astype(BF16)
        finish_prev_rows()

    @pl.when(t == n_tiles)
    def _():
        finish_prev_rows()


def _ffn_call(x, mod, ng, w_in, w_out, *, layer, mrow, grow):
    bsz, s, d = x.shape
    dff = w_out.shape[1]
    tm, tf = 512, 512
    nf = dff // tf
    per_batch = s // tm
    n_tiles = bsz * per_batch
    edge_rows = -(-tm // (nf * BF16_ROWS)) * BF16_ROWS

    def nxt(t):
        return jnp.minimum(t + 1, n_tiles - 1)

    def prv(t):
        return jnp.maximum(t - 1, 0)

    def wj(t, j):
        return jnp.where(t == n_tiles, nf - 1, j)

    out = pl.pallas_call(
        functools.partial(_ffn_kernel, mrow=mrow, grow=grow, n_tiles=n_tiles, edge_rows=edge_rows),
        out_shape=jax.ShapeDtypeStruct((n_tiles * tm, d), F32),
        grid=(n_tiles + 1, nf),
        in_specs=[
            pl.BlockSpec((tm, d), lambda t, j: (nxt(t), 0)),
            pl.BlockSpec((tm, d), lambda t, j: (prv(t), 0)),
            pl.BlockSpec((None, N_MOD, d), lambda t, j: (nxt(t) // per_batch, 0, 0)),
            pl.BlockSpec((None, N_MOD, d), lambda t, j: (prv(t) // per_batch, 0, 0)),
            pl.BlockSpec(ng.shape, lambda t, j: (0, 0)),
            pl.BlockSpec((None, d, tf), lambda t, j: (layer, 0, wj(t, j))),
            pl.BlockSpec((None, d, tf), lambda t, j: (layer, 0, nf + wj(t, j))),
            pl.BlockSpec((None, tf, d), lambda t, j: (layer, wj(t, j), 0)),
        ],
        out_specs=pl.BlockSpec((tm, d), lambda t, j: (prv(t), 0)),
        scratch_shapes=[pltpu.VMEM((2, tm, d), BF16), pltpu.VMEM((2, tm, d), F32)],
        compiler_params=_cparams(("arbitrary", "arbitrary")),
        name="ffn",
    )(x.reshape(n_tiles * tm, d), x.reshape(n_tiles * tm, d), mod, mod, ng, w_in, w_in, w_out)
    return out.reshape(bsz, s, d)


def _proj_kernel(x_ref, mod_ref, ng_ref, w_ref, wgate_ref, o_ref, og_ref, xn_ref, *, mrow, grow):
    @pl.when(pl.program_id(2) == 0)
    def _():
        xn = _prenorm(x_ref[...], mod_ref, ng_ref, mrow, grow).astype(BF16)
        xn_ref[...] = xn
        og_ref[...] = _dot(xn, wgate_ref[...])

    o_ref[...] = _dot(xn_ref[...], w_ref[...])


def _proj_call(x, mod, ng, w_in, w_gate, *, layer, n, mrow, grow):
    bsz, s, d = x.shape
    ngate = w_gate.shape[1]
    tm, tn = 1024, 1024
    return pl.pallas_call(
        functools.partial(_proj_kernel, mrow=mrow, grow=grow),
        out_shape=(jax.ShapeDtypeStruct((bsz, s, n), F32), jax.ShapeDtypeStruct((bsz, s, ngate), F32)),
        grid=(bsz, s // tm, n // tn),
        in_specs=[
            pl.BlockSpec((None, tm, d), lambda b, m, j: (b, m, 0)),
            pl.BlockSpec((None, N_MOD, d), lambda b, m, j: (b, 0, 0)),
            pl.BlockSpec(ng.shape, lambda b, m, j: (0, 0)),
            pl.BlockSpec((None, d, tn), lambda b, m, j: (layer, 0, j)),
            pl.BlockSpec((d, ngate), lambda b, m, j: (0, 0)),
        ],
        out_specs=(pl.BlockSpec((None, tm, tn), lambda b, m, j: (b, m, j)),
                   pl.BlockSpec((None, tm, ngate), lambda b, m, j: (b, m, 0))),
        scratch_shapes=[pltpu.VMEM((tm, d), BF16)],
        compiler_params=_cparams(("parallel", "parallel", "arbitrary")),
        name="proj",
    )(x, mod, ng, w_in, w_gate)


def _oproj_kernel(a_ref, w_ref, x_ref, mod_ref, ng_ref, o_ref, *, mrow, grow):
    yn = _rms(_dot(a_ref[...], w_ref[...])) * ng_ref[grow:grow + 1, :]
    o_ref[...] = x_ref[...] + mod_ref[mrow:mrow + 1, :] * yn


def _oproj_call(a, w, x, mod, ng, *, layer, mrow, grow):
    bsz, s, d = x.shape
    k = a.shape[-1]
    tm = 256
    return pl.pallas_call(
        functools.partial(_oproj_kernel, mrow=mrow, grow=grow),
        out_shape=jax.ShapeDtypeStruct(x.shape, F32),
        grid=(bsz, s // tm),
        in_specs=[
            pl.BlockSpec((None, tm, k), lambda b, m: (b, m, 0)),
            pl.BlockSpec((None, k, d), lambda b, m: (layer, 0, 0)),
            pl.BlockSpec((None, tm, d), lambda b, m: (b, m, 0)),
            pl.BlockSpec((None, N_MOD, d), lambda b, m: (b, 0, 0)),
            pl.BlockSpec(ng.shape, lambda b, m: (0, 0)),
        ],
        out_specs=pl.BlockSpec((None, tm, d), lambda b, m: (b, m, 0)),
        compiler_params=_cparams(("parallel", "parallel")),
        name="oproj",
    )(a, w, x, mod, ng)


def _rope(x, cos, sin_signed):
    return x * cos + pltpu.roll(x, NSA_HEAD_DIM // 2, 1) * sin_signed


def _nsa_prep_kernel(kc_ref, vc_ref, ks_ref, vs_ref, kw_ref, vw_ref, pe_ref, w1_ref, w2_ref, cos_ref, sin_ref,
                     kco_ref, vco_ref, kso_ref, vso_ref, kwo_ref, vwo_ref):
    s = kc_ref.shape[0]
    nrow = s // CMP_STRIDE
    half = CMP_BLOCK // CMP_STRIDE

    def compress(t_ref, idx):
        streams = [t_ref[pl.ds(j, nrow, stride=CMP_STRIDE), :] for j in range(CMP_STRIDE)]
        hidden = None
        for h in range(half):
            flat = jnp.concatenate(
                [(streams[j] + pe_ref[idx, h * CMP_STRIDE + j:h * CMP_STRIDE + j + 1, :]).astype(BF16)
                 for j in range(CMP_STRIDE)], axis=1)
            w = w1_ref[idx, h * CMP_STRIDE * NSA_HEAD_DIM:(h + 1) * CMP_STRIDE * NSA_HEAD_DIM, :]
            part = _dot(flat, w)
            if h:
                part = pltpu.roll(part, nrow - h, 0)
            hidden = part if hidden is None else hidden + part
        return _dot(_silu(hidden).astype(BF16), w2_ref[idx])

    kco_ref[...] = compress(kc_ref, 0).astype(kco_ref.dtype)
    vco_ref[...] = compress(vc_ref, 1).astype(vco_ref.dtype)
    cos = cos_ref[...]
    sin = sin_ref[...]
    kso_ref[...] = _rope(ks_ref[...], cos, sin).astype(BF16)
    kwo_ref[...] = _rope(kw_ref[...], cos, sin).astype(BF16)
    vso_ref[...] = vs_ref[...].T.astype(BF16)
    vwo_ref[...] = vw_ref[...].T.astype(BF16)


def _nsa_prep_call(main, pe, w1, w2, cos, sin, *, layer):
    bsz, s, _ = main.shape
    g, dh = NSA_KV_GROUPS, NSA_HEAD_DIM
    nrow = s // CMP_STRIDE
    q_blocks = NSA_Q_DIM // dh

    def col(k):
        return pl.BlockSpec((None, s, dh), lambda b, gi, k=k: (b, 0, q_blocks + k * g + gi))

    def full(a):
        return pl.BlockSpec(a.shape, lambda b, gi, nd=a.ndim: (0,) * nd)

    def stacked(a):
        return pl.BlockSpec((None,) + a.shape[1:], lambda b, gi, nd=a.ndim: (layer,) + (0,) * (nd - 1))

    def out(rows, cols):
        return (jax.ShapeDtypeStruct((bsz, g, rows, cols), BF16),
                pl.BlockSpec((None, None, rows, cols), lambda b, gi: (b, gi, 0, 0)))

    outs = [out(nrow, dh), out(nrow, dh), out(s, dh), out(dh, s), out(s, dh), out(dh, s)]
    return pl.pallas_call(
        _nsa_prep_kernel,
        out_shape=tuple(o[0] for o in outs),
        grid=(bsz, g),
        in_specs=[col(0), col(1), col(2), col(3), col(4), col(5), stacked(pe), stacked(w1), stacked(w2),
                  full(cos), full(sin)],
        out_specs=tuple(o[1] for o in outs),
        compiler_params=_cparams(("parallel", "parallel")),
        name="nsa_prep",
    )(main, main, main, main, main, main, pe, w1, w2, cos, sin)


def _nsa_attn_kernel(q_ref, gl_ref, cos_ref, sin_ref, kc_ref, vc_ref, ks_ref, vst_ref, kw_ref, vwt_ref,
                     cover_ref, o_ref, qrot_scr, selb_scr, m_scr, l_scr, alpha_scr, acc_scr, out_scr, s_scr, p_scr,
                     *, tq, tk, n_sel):
    qi = pl.program_id(2)
    dh = NSA_HEAD_DIM
    scale = dh ** -0.5 * LOG2_E
    pos_q = qi * tq + lax.broadcasted_iota(jnp.int32, (1, tq), 1)
    cos = cos_ref[...]
    sin = sin_ref[...]
    gates = jax.nn.sigmoid(gl_ref[...].T[:4 * NSA_HPG, :])

    crow = lax.broadcasted_iota(jnp.int32, (LANES, 1), 0)
    cmask = crow * CMP_STRIDE + (CMP_BLOCK - 1) <= pos_q
    kc = kc_ref[...]
    vct = vc_ref[...].astype(F32).T.astype(BF16)
    p_sum = jnp.zeros((LANES, tq), F32)
    for r in range(NSA_HPG):
        qt = q_ref[:, r * dh:(r + 1) * dh].T
        sc = jnp.where(cmask, _dot(kc, qt.astype(BF16)) * scale, NEG_INF)
        e = jnp.exp2(sc - jnp.max(sc, axis=0, keepdims=True))
        p = jnp.where(cmask, e * (1.0 / jnp.sum(e, axis=0, keepdims=True)), 0.0)
        p_sum = p_sum + p
        out_scr[r] = gates[3 * r:3 * r + 1, :] * _dot(vct, p.astype(BF16))
        rot = jnp.concatenate([qt[dh // 2:], qt[:dh // 2]], axis=0)
        qrot_scr[r] = (qt * cos + rot * sin).astype(BF16)

    imp = jnp.dot(cover_ref[...], p_sum, precision=HIGHEST, preferred_element_type=F32)[:n_sel]
    jrow = lax.broadcasted_iota(jnp.int32, (n_sel, 1), 0)
    q_blk = pos_q // SEL_BLOCK
    causal = jrow <= q_blk
    forced = (jrow == 0) | (causal & (jrow > q_blk - N_LOCAL_BLOCKS))
    imp = jnp.where(forced, FORCE_SCORE, jnp.where(causal, imp, -1.0))
    rank = jnp.zeros((n_sel, tq), F32)
    for jp in range(n_sel):
        other = imp[jp:jp + 1, :]
        tie = jnp.where(jrow > jp, 1.0, 0.0)
        rank = rank + jnp.where(other > imp, 1.0, jnp.where(other == imp, tie, 0.0))
    keep = (rank < float(min(SEL_TOPK, n_sel))) & (imp >= 0.0)
    selb_scr[...] = jnp.where(keep, 0.0, NEG_INF)

    def reset():
        m_scr[...] = jnp.full(m_scr.shape, NEG_INF, F32)
        l_scr[...] = jnp.zeros(l_scr.shape, F32)
        acc_scr[...] = jnp.zeros(acc_scr.shape, F32)
        alpha_scr[...] = jnp.ones(alpha_scr.shape, F32)
        p_scr[...] = jnp.zeros(p_scr.shape, BF16)

    def tile0(kt):
        return pl.multiple_of(jnp.maximum(kt, 0) * tk, tk)

    def apply_pending(vt_ref, kt):
        slot = kt & 1
        vt =vt_ref[:, pl.ds(tile0(kt), tk)]
        for r in range(NSA_HPG):
            acc_scr[r] = alpha_scr[slot, r] * acc_scr[r] + _dot(vt, p_scr[slot, r])

    def flash_step(k_ref, vt_ref, kt, bias):
        slot = kt & 1
        k_tile = k_ref[pl.ds(tile0(kt), tk), :]
        for r in range(NSA_HPG):
            s_scr[r] = _dot(k_tile, qrot_scr[r])
        apply_pending(vt_ref, kt - 1)
        for r in range(NSA_HPG):
            sc = s_scr[r] * scale
            if bias is not None:
                sc = sc + bias
            m_old = m_scr[r]
            m_new = jnp.maximum(m_old, jnp.max(sc, axis=0, keepdims=True))
            alpha = jnp.exp2(m_old - m_new)
            p = jnp.exp2(sc - m_new)
            l_scr[r] = alpha * l_scr[r] + jnp.sum(p, axis=0, keepdims=True)
            alpha_scr[slot, r] = alpha
            p_scr[slot, r] = p.astype(BF16)
            m_scr[r] = m_new

    def finish(branch):
        for r in range(NSA_HPG):
            w = gates[3 * r + branch:3 * r + branch + 1, :] / l_scr[r]
            out_scr[r] = out_scr[r] + w * acc_scr[r]

    def sel_bias(kt):
        per_tile = tk // SEL_BLOCK
        rows = [selb_scr[pl.ds(kt * per_tile + j, 1), :] for j in range(per_tile)]
        return jnp.concatenate([jnp.broadcast_to(row, (SEL_BLOCK, tq)) for row in rows], axis=0)

    def key_pos(kt):
        return kt * tk + lax.broadcasted_iota(jnp.int32, (tk, 1), 0)

    causal_bias = jnp.where(key_pos(qi) <= pos_q, 0.0, NEG_INF)

    reset()

    def sel_body(kt, carry):
        flash_step(ks_ref, vst_ref, kt, sel_bias(kt))
        return carry

    lax.fori_loop(0, qi, sel_body, 0)
    flash_step(ks_ref, vst_ref, qi, sel_bias(qi) + causal_bias)
    apply_pending(vst_ref, qi)
    finish(1)

    reset()
    n_back = WINDOW // tk
    for back in range(n_back, 0, -1):
        @pl.when(qi >= back)
        def _(back=back):
            bias = jnp.where(pos_q - key_pos(qi - back) < WINDOW, 0.0, NEG_INF) if back == n_back else None
            flash_step(kw_ref, vwt_ref, qi - back, bias)
    flash_step(kw_ref, vwt_ref, qi, causal_bias)
    apply_pending(vwt_ref, qi)
    finish(2)

    for r in range(NSA_HPG):
        o_ref[:, r * dh:(r + 1) * dh] = out_scr[r].T.astype(o_ref.dtype)


def _nsa_attn_call(main, gl, cos_t, sin_t, kc, vc, ks, vst, kw, vwt, cover_t):
    bsz, s, _ = main.shape
    g, r, dh = NSA_KV_GROUPS, NSA_HPG, NSA_HEAD_DIM
    tq = tk = 256
    assert WINDOW % tk == 0 and tk % SEL_BLOCK == 0 and s % tq == 0
    nrow = kc.shape[2]
    n_sel = s // SEL_BLOCK

    def kv(rows, cols):
        return pl.BlockSpec((None, None, rows, cols), lambda b, gi, qi: (b, gi, 0, 0))

    return pl.pallas_call(
        functools.partial(_nsa_attn_kernel, tq=tq, tk=tk, n_sel=n_sel),
        out_shape=jax.ShapeDtypeStruct((bsz, s, NSA_Q_DIM), BF16),
        grid=(bsz, g, s // tq),
        in_specs=[
            pl.BlockSpec((None, tq, r * dh), lambda b, gi, qi: (b, qi, gi)),
            pl.BlockSpec((None, tq, LANES), lambda b, gi, qi: (b, qi, gi)),
            pl.BlockSpec((dh, tq), lambda b, gi, qi: (0, qi)),
            pl.BlockSpec((dh, tq), lambda b, gi, qi: (0, qi)),
            kv(nrow, dh), kv(nrow, dh), kv(s, dh), kv(dh, s), kv(s, dh), kv(dh, s),
            pl.BlockSpec(cover_t.shape, lambda b, gi, qi: (0, 0)),
        ],
        out_specs=pl.BlockSpec((None, tq, r * dh), lambda b, gi, qi: (b, qi, gi)),
        scratch_shapes=[
            pltpu.VMEM((r, dh, tq), BF16),
            pltpu.VMEM((n_sel, tq), F32),
            pltpu.VMEM((r, 1, tq), F32),
            pltpu.VMEM((r, 1, tq), F32),
            pltpu.VMEM((2, r, 1, tq), F32),
            pltpu.VMEM((r, dh, tq), F32),
            pltpu.VMEM((r, dh, tq), F32),
            pltpu.VMEM((r, tk, tq), F32),
            pltpu.VMEM((2, r, tk, tq), BF16),
        ],
        compiler_params=_cparams(("parallel", "parallel", "arbitrary")),
        name="nsa_attn",
    )(main, gl, cos_t, sin_t, kc, vc, ks, vst, kw, vwt, cover_t)


def _nsa_tables(s):
    half = NSA_HEAD_DIM // 2
    freqs = ROPE_THETA ** (-jnp.arange(half, dtype=F32) / half)
    ang = jnp.arange(s).astype(F32)[:, None] * freqs[None, :]
    cos, sin = jnp.cos(ang), jnp.sin(ang)
    cos_full = jnp.concatenate([cos, cos], axis=-1)
    sin_signed = jnp.concatenate([-sin, sin], axis=-1)
    n_cmp = (s - CMP_BLOCK) // CMP_STRIDE + 1
    cs = np.arange(LANES)[None, :] * CMP_STRIDE
    js = np.arange(LANES)[:, None]
    cover_t = ((cs < (js + 1) * SEL_BLOCK) & (cs + CMP_BLOCK > js * SEL_BLOCK)
               & (np.arange(LANES)[None, :] < n_cmp) & (js < s // SEL_BLOCK)).astype(np.float32)
    return cos_full, sin_signed, cos_full.T, sin_signed.T, jnp.asarray(cover_t, F32)


def _nsa_layer(x, mod, ng, w_in, cmp_pos, cmp_w1, cmp_w2, w_out, tables, *, layer):
    cos, sin, cos_t, sin_t, cover_t = tables
    g, r = NSA_KV_GROUPS, NSA_HPG
    w_gate = w_in[layer, :, NSA_MAIN_DIM:].reshape(D_MODEL, g, 3 * r)
    w_gate = jnp.pad(w_gate, ((0, 0), (0, 0), (0, LANES - 3 * r))).reshape(D_MODEL, g * LANES)
    main, gl = _proj_call(x, mod, ng, w_in, w_gate, layer=layer, n=NSA_MAIN_DIM, mrow=3, grow=2)
    kc, vc, ks, vst, kw, vwt = _nsa_prep_call(main, cmp_pos, cmp_w1, cmp_w2, cos, sin, layer=layer)
    o = _nsa_attn_call(main, gl, cos_t, sin_t, kc, vc, ks, vst, kw, vwt, cover_t)
    return _oproj_call(o, w_out, x, mod, ng, layer=layer, mrow=5, grow=3)


def _mlstm_kernel(q_ref, k_ref, v_ref, og_ref, gi_ref, gf_ref, bi_ref, bf_ref, cwq_ref, cwk_ref, cbq_ref, cbk_ref,
                  gain_ref, o_ref, qpad_scr, kpad_scr, ct_scr, b_scr, rowi_scr, mloc_scr, bcol_scr, ecol_scr):
    s, dk = q_ref.shape
    dv = v_ref.shape[1]
    chunk = ML_CHUNK
    nc = s // chunk
    pad = qpad_scr.shape[0] - s

    for src, dst in ((q_ref, qpad_scr), (k_ref, kpad_scr)):
        dst[:pad] = jnp.zeros((pad, dk), F32)
        dst[pad:] = src[...]

    def conv_silu(pad_ref, w_ref, b_ref, r0):
        win = pad_ref[pl.ds(r0, chunk + pad), :]
        acc = win[pad:] * w_ref[ML_CONV - 1:ML_CONV, :] + b_ref[...]
        for d in range(1, ML_CONV):
            acc = acc + pltpu.roll(win, d, 0)[pad:] * w_ref[ML_CONV - 1 - d:ML_CONV - d, :]
        return _silu(acc)

    ig = gi_ref[...] + bi_ref[...]
    fg = gf_ref[...] + bf_ref[...]
    logf = jnp.minimum(fg, 0.0) - jnp.log1p(jnp.exp(-jnp.abs(fg)))
    tri_r = lax.broadcasted_iota(jnp.int32, (chunk, chunk), 0)
    tri_c = lax.broadcasted_iota(jnp.int32, (chunk, chunk), 1)
    upper = jnp.where(tri_r <= tri_c, 1.0, 0.0)
    b = jnp.dot(logf, upper, precision=HIGHEST, preferred_element_type=F32)
    w_end = b[:, chunk - 1:chunk] - b + ig
    m_loc = jnp.max(w_end, axis=-1, keepdims=True)
    e_end = jnp.exp(w_end - m_loc)
    b_scr[...] = b
    rowi_scr[...] = ig - b
    mloc_scr[...] = jnp.broadcast_to(m_loc, mloc_scr.shape)
    eye = jnp.where(tri_r == tri_c, 1.0, 0.0)
    cols = lax.dot_general(eye, jnp.concatenate([b, e_end], axis=0), NT_DIMS, precision=HIGHEST,
                           preferred_element_type=F32)
    for c in range(nc):
        bcol_scr[c] = jnp.broadcast_to(cols[:, c:c + 1], (chunk, LANES))
        ecol_scr[c] = jnp.broadcast_to(cols[:, nc + c:nc + c + 1], (chunk, LANES))

    ct_scr[...] = jnp.zeros(ct_scr.shape, F32)
    lower = tri_c <= tri_r
    gain = gain_ref[...]

    def body(c, m_prev):
        r0 = pl.multiple_of(c * chunk, chunk)
        q = conv_silu(qpad_scr, cwq_ref, cbq_ref, r0).astype(BF16)
        k = conv_silu(kpad_scr, cwk_ref, cbk_ref, r0) * (dk ** -0.5)
        va = jnp.concatenate([v_ref[pl.ds(r0, chunk), :].astype(BF16), jnp.ones((chunk, LANES), BF16)], axis=1)
        b_row = b_scr[pl.ds(c, 1), :]
        b_tot = b_row[:, chunk - 1:chunk]
        m_l = mloc_scr[pl.ds(c, 1), :][:, :1]
        bcol = bcol_scr[c]
        log_intra = jnp.where(lower, bcol[:, :chunk] + rowi_scr[pl.ds(c, 1), :], NEG_INF)
        log_inter = bcol[:, :1] + m_prev
        m_t = jnp.maximum(log_inter, jnp.max(log_intra, axis=-1, keepdims=True))
        e_inter = jnp.exp(log_inter - m_t)
        qk = _dot_nt(q, k.astype(BF16)) * jnp.exp(log_intra - m_t)
        ct = ct_scr[...]
        tot = _dot(qk.astype(BF16), va) + e_inter * _dot(q, ct.astype(BF16))
        h = tot[:, :dv] / jnp.maximum(jnp.abs(tot[:, dv:dv + 1]), jnp.exp(-m_t))
        hn = _rms(h) * gain
        o_ref[pl.ds(r0, chunk), :] = (jax.nn.sigmoid(og_ref[pl.ds(r0, chunk), :]) * hn).astype(o_ref.dtype)
        m_new = jnp.maximum(b_tot + m_prev, m_l)
        c_loc = lax.dot_general((k * ecol_scr[c]).astype(BF16), va, TN_DIMS, preferred_element_type=F32)
        ct_scr[...] = jnp.exp(b_tot + m_prev - m_new) * ct + jnp.exp(m_l - m_new) * c_loc
        return m_new

    lax.fori_loop(0, nc, body, jnp.zeros((1, 1), F32), unroll=4)


def _mlstm_call(main, gates_t, gate_b, conv_w, conv_b, gain):
    bsz, s, _ = main.shape
    h, dk, dv = ML_HEADS, ML_QK_DIM, ML_V_DIM
    chunk = ML_CHUNK
    nc = s // chunk
    vblk0 = ML_QK_COLS // dv

    def gate(off):
        return pl.BlockSpec((None, None, nc, chunk), lambda b, hi: (b, off + hi, 0, 0))

    def bias(off):
        return pl.BlockSpec((None, 1, 1), lambda b, hi: (off + hi, 0, 0))

    return pl.pallas_call(
        _mlstm_kernel,
        out_shape=jax.ShapeDtypeStruct((bsz, s, h * dv), BF16),
        grid=(bsz, h),
        in_specs=[
            pl.BlockSpec((None, s, dk), lambda b, hi: (b, 0, hi)),
            pl.BlockSpec((None, s, dk), lambda b, hi: (b, 0, h + hi)),
            pl.BlockSpec((None, s, dv), lambda b, hi: (b, 0, vblk0 + hi)),
            pl.BlockSpec((None, s, dv), lambda b, hi: (b, 0, vblk0 + h + hi)),
            gate(0), gate(h), bias(0), bias(h),
            pl.BlockSpec((ML_CONV, dk), lambda b, hi: (0, hi)),
            pl.BlockSpec((ML_CONV, dk), lambda b, hi: (0, h + hi)),
            pl.BlockSpec((1, dk), lambda b, hi: (0, hi)),
            pl.BlockSpec((1, dk), lambda b, hi: (0, h + hi)),
            pl.BlockSpec((1, dv), lambda b, hi: (0, hi)),
        ],
        out_specs=pl.BlockSpec((None, s, dv), lambda b, hi: (b, 0, hi)),
        scratch_shapes=[
            pltpu.VMEM((s + SUBLANES, dk), F32),
            pltpu.VMEM((s + SUBLANES, dk), F32),
            pltpu.VMEM((dk, dv + LANES), F32),
            pltpu.VMEM((nc, chunk), F32),
            pltpu.VMEM((nc, chunk), F32),
            pltpu.VMEM((nc, LANES), F32),
            pltpu.VMEM((nc, chunk, LANES), F32),
            pltpu.VMEM((nc, chunk, LANES), F32),
        ],
        compiler_params=_cparams(("parallel", "parallel")),
        name="mlstm",
    )(main, main, main, main, gates_t, gates_t, gate_b, gate_b, conv_w, conv_w, conv_b, conv_b, gain)


def _mlstm_layer(x, mod, ng, w_in, conv_w, conv_b, gate_b, mh_gain, w_out, *, layer):
    bsz, s, _ = x.shape
    ngate = 2 * ML_HEADS
    w_gate = jnp.pad(w_in[layer, :, ML_MAIN_DIM:], ((0, 0), (0, LANES - ngate)))
    main, gl = _proj_call(x, mod, ng, w_in, w_gate, layer=layer, n=ML_MAIN_DIM, mrow=3, grow=2)
    gates_t = jnp.swapaxes(gl[:, :, :ngate], 1, 2).reshape(bsz, ngate, s // ML_CHUNK, ML_CHUNK)
    o = _mlstm_call(main, gates_t, gate_b.reshape(ngate, 1, 1), conv_w, conv_b.reshape(1, -1),
                    mh_gain.reshape(1, -1))
    return _oproj_call(o, w_out, x, mod, ng, layer=layer, mrow=5, grow=3)


def kernel(x, c, mod_w, mod_b, norm_g, ffn_pre_w_in, ffn_pre_w_out, ffn_post_w_in, ffn_post_w_out, nsa_w_in, nsa_cmp_pos, nsa_cmp_w1, nsa_cmp_w2, nsa_w_out, ml_w_in, ml_conv_w, ml_conv_b, ml_gate_b, ml_mh_gain, ml_w_out):
    bsz, s, d = x.shape
    depth = mod_w.shape[0]
    mods = _mod_call(c, mod_w, mod_b).reshape(depth, bsz, N_MOD, d)
    tables = _nsa_tables(s)
    pre_in, pre_out = ffn_pre_w_in.astype(BF16), ffn_pre_w_out.astype(BF16)
    post_in, post_out = ffn_post_w_in.astype(BF16), ffn_post_w_out.astype(BF16)
    nsa_in, nsa_out = nsa_w_in.astype(BF16), nsa_w_out.astype(BF16)
    nsa_w1, nsa_w2 = nsa_cmp_w1.astype(BF16), nsa_cmp_w2.astype(BF16)
    ml_in, ml_out = ml_w_in.astype(BF16), ml_w_out.astype(BF16)
    for i in range(depth):
        mod, ng = mods[i], norm_g[i]
        x = _ffn_call(x, mod, ng, pre_in, pre_out, layer=i, mrow=0, grow=0)
        j = i // 2
        if i % 2 == 0:
            x = _nsa_layer(x, mod, ng, nsa_in, nsa_cmp_pos, nsa_w1, nsa_w2, nsa_out, tables, layer=j)
        else:
            x = _mlstm_layer(x, mod, ng, ml_in, ml_conv_w[j], ml_conv_b[j], ml_gate_b[j], ml_mh_gain[j], ml_out,
                             layer=j)
        x = _ffn_call(x, mod, ng, post_in, post_out, layer=i, mrow=6, grow=4)
    return x
```

```python
import functools

import numpy as np
import jax
import jax.numpy as jnp
from jax import lax
from jax.experimental import pallas as pl
from jax.experimental.pallas import tpu as pltpu

F32 = jnp.float32
BF16 = jnp.bfloat16
HIGHEST = lax.Precision.HIGHEST

D_MODEL = 2048
DEPTH = 4
D_FF = 5632
FFN_RES = 0.5
N_MOD = 9
RMS_EPS = 1e-6
NEG_INF = -1e30
LOG2_E = 1.4426950408889634

NSA_HEADS = 16
NSA_HEAD_DIM = 128
NSA_KV_GROUPS = 4
NSA_HPG = NSA_HEADS // NSA_KV_GROUPS
NSA_Q_DIM = NSA_HEADS * NSA_HEAD_DIM
NSA_KV_DIM = NSA_KV_GROUPS * NSA_HEAD_DIM
NSA_MAIN_DIM = NSA_Q_DIM + 6 * NSA_KV_DIM
CMP_BLOCK = 32
CMP_STRIDE = 16
CMP_HIDDEN = 512
SEL_BLOCK = 64
SEL_TOPK = 16
N_LOCAL_BLOCKS = 2
FORCE_SCORE = 1e9
WINDOW = 512
ROPE_THETA = 10000.0

ML_HEADS = 8
ML_QK_DIM = 128
ML_V_DIM = 256
ML_CONV = 4
ML_CHUNK = 64
ML_QK_COLS = 2 * ML_HEADS * ML_QK_DIM
ML_MAIN_DIM = ML_QK_COLS + 2 * ML_HEADS * ML_V_DIM

LANES = 128
SUBLANES = 8
VMEM_LIMIT = 56 * 1024 * 1024

NT_DIMS = (((1,), (1,)), ((), ()))
TN_DIMS = (((0,), (0,)), ((), ()))


def _cparams(sem):
    return pltpu.CompilerParams(dimension_semantics=sem, vmem_limit_bytes=VMEM_LIMIT)


def _silu(x):
    return x * jax.nn.sigmoid(x)


def _rms(x):
    return x * lax.rsqrt(jnp.mean(x * x, axis=-1, keepdims=True) + RMS_EPS)


def _dot(a, b):
    return jnp.dot(a, b, preferred_element_type=F32)


def _dot_nt(a, b):
    return lax.dot_general(a, b, NT_DIMS, preferred_element_type=F32)


def _mod_kernel(c_ref, w_ref, b_ref, o_ref):
    ca = _silu(c_ref[...]).astype(BF16)
    o_ref[...] = _dot(ca, w_ref[...].astype(BF16)) + b_ref[...]


def _mod_call(c, mod_w, mod_b):
    depth, d, n = mod_w.shape
    bsz = c.shape[0]
    tn = 1024
    return pl.pallas_call(
        _mod_kernel,
        out_shape=jax.ShapeDtypeStruct((depth, bsz, n), F32),
        grid=(depth, n // tn),
        in_specs=[
            pl.BlockSpec((bsz, d), lambda i, j: (0, 0)),
            pl.BlockSpec((None, d, tn), lambda i, j: (i, 0, j)),
            pl.BlockSpec((None, 1, tn), lambda i, j: (i, 0, j)),
        ],
        out_specs=pl.BlockSpec((None, bsz, tn), lambda i, j: (i, 0, j)),
        compiler_params=_cparams(("parallel", "parallel")),
        name="mod",
    )(c, mod_w, mod_b.reshape(depth, 1, n))


def _prenorm(x, mod_ref, ng_ref, mrow, grow):
    gain = ng_ref[grow:grow + 1, :] * (1.0 + mod_ref[mrow + 1:mrow + 2, :])
    return _rms(x) * gain + mod_ref[mrow:mrow + 1, :]


ROW_CHUNK = 16
ROW_UNROLL = 4


def _row_loop(n_rows, body):
    def step(i, carry):
        body(pl.ds(pl.multiple_of(i * ROW_CHUNK, ROW_CHUNK), ROW_CHUNK))
        return carry
    lax.fori_loop(0, n_rows // ROW_CHUNK, step, 0, unroll=ROW_UNROLL)


def _ffn_kernel(x_ref, mod_ref, ng_ref, wg_ref, wu_ref, wo_ref, o_ref, xn_ref, acc_ref, *, mrow, grow):
    j = pl.program_id(2)

    @pl.when(j == 0)
    def _():
        gain = ng_ref[grow:grow + 1, :] * (1.0 + mod_ref[mrow + 1:mrow + 2, :])
        shift = mod_ref[mrow:mrow + 1, :]

        def prenorm_rows(rows):
            xn_ref[rows, :] = (_rms(x_ref[rows, :]) * gain + shift).astype(BF16)

        _row_loop(x_ref.shape[0], prenorm_rows)
        acc_ref[...] = jnp.zeros_like(acc_ref)

    xn = xn_ref[...]
    g = _dot(xn, wg_ref[...])
    u = _dot(xn, wu_ref[...])
    acc_ref[...] += _dot((_silu(g) * u).astype(BF16), wo_ref[...])

    @pl.when(j == pl.num_programs(2) - 1)
    def _():
        gain = ng_ref[grow + 1:grow + 2, :] * (FFN_RES * mod_ref[mrow + 2:mrow + 3, :])

        def finish_rows(rows):
            o_ref[rows, :] = x_ref[rows, :] + _rms(acc_ref[rows, :]) * gain

        _row_loop(x_ref.shape[0], finish_rows)


def _ffn_call(x, mod, ng, w_in, w_out, *, layer, mrow, grow):
    bsz, s, d = x.shape
    dff = w_out.shape[1]
    tm, tf = 512, 512
    nf = dff // tf
    return pl.pallas_call(
        functools.partial(_ffn_kernel, mrow=mrow, grow=grow),
        out_shape=jax.ShapeDtypeStruct(x.shape, F32),
        grid=(bsz, s // tm, nf),
        in_specs=[
            pl.BlockSpec((None, tm, d), lambda b, m, j: (b, m, 0)),
            pl.BlockSpec((None, N_MOD, d), lambda b, m, j: (b, 0, 0)),
            pl.BlockSpec(ng.shape, lambda b, m, j: (0, 0)),
            pl.BlockSpec((None, d, tf), lambda b, m, j: (layer, 0, j)),
            pl.BlockSpec((None, d, tf), lambda b, m, j: (layer, 0, nf + j)),
            pl.BlockSpec((None, tf, d), lambda b, m, j: (layer, j, 0)),
        ],
        out_specs=pl.BlockSpec((None, tm, d), lambda b, m, j: (b, m, 0)),
        scratch_shapes=[pltpu.VMEM((tm, d), BF16), pltpu.VMEM((tm, d), F32)],
        compiler_params=_cparams(("parallel", "parallel", "arbitrary")),
        name="ffn",
    )(x, mod, ng, w_in, w_in, w_out)


def _proj_kernel(x_ref, mod_ref, ng_ref, w_ref, wgate_ref, o_ref, og_ref, xn_ref, *, mrow, grow):
    @pl.when(pl.program_id(2) == 0)
    def _():
        xn = _prenorm(x_ref[...], mod_ref, ng_ref, mrow, grow).astype(BF16)
        xn_ref[...] = xn
        og_ref[...] = _dot(xn, wgate_ref[...])

    o_ref[...] = _dot(xn_ref[...], w_ref[...])


def _proj_call(x, mod, ng, w_in, w_gate, *, layer, n, mrow, grow):
    bsz, s, d = x.shape
    ngate = w_gate.shape[1]
    tm, tn = 1024, 1024
    return pl.pallas_call(
        functools.partial(_proj_kernel, mrow=mrow, grow=grow),
        out_shape=(jax.ShapeDtypeStruct((bsz, s, n), F32), jax.ShapeDtypeStruct((bsz, s, ngate), F32)),
        grid=(bsz, s // tm, n // tn),
        in_specs=[
            pl.BlockSpec((None, tm, d), lambda b, m, j: (b, m, 0)),
            pl.BlockSpec((None, N_MOD, d), lambda b, m, j: (b, 0, 0)),
            pl.BlockSpec(ng.shape, lambda b, m, j: (0, 0)),
            pl.BlockSpec((None, d, tn), lambda b, m, j: (layer, 0, j)),
            pl.BlockSpec((d, ngate), lambda b, m, j: (0, 0)),
        ],
        out_specs=(pl.BlockSpec((None, tm, tn), lambda b, m, j: (b, m, j)),
                   pl.BlockSpec((None, tm, ngate), lambda b, m, j: (b, m, 0))),
        scratch_shapes=[pltpu.VMEM((tm, d), BF16)],
        compiler_params=_cparams(("parallel", "parallel", "arbitrary")),
        name="proj",
    )(x, mod, ng, w_in, w_gate)


def _oproj_kernel(a_ref, w_ref, x_ref, mod_ref, ng_ref, o_ref, *, mrow, grow):
    yn = _rms(_dot(a_ref[...], w_ref[...])) * ng_ref[grow:grow + 1, :]
    o_ref[...] = x_ref[...] + mod_ref[mrow:mrow + 1, :] * yn


def _oproj_call(a, w, x, mod, ng, *, layer, mrow, grow):
    bsz, s, d = x.shape
    k = a.shape[-1]
    tm = 256
    return pl.pallas_call(
        functools.partial(_oproj_kernel, mrow=mrow, grow=grow),
        out_shape=jax.ShapeDtypeStruct(x.shape, F32),
        grid=(bsz, s // tm),
        in_specs=[
            pl.BlockSpec((None, tm, k), lambda b, m: (b, m, 0)),
            pl.BlockSpec((None, k, d), lambda b, m: (layer, 0, 0)),
            pl.BlockSpec((None, tm, d), lambda b, m: (b, m, 0)),
            pl.BlockSpec((None, N_MOD, d), lambda b, m: (b, 0, 0)),
            pl.BlockSpec(ng.shape, lambda b, m: (0, 0)),
        ],
        out_specs=pl.BlockSpec((None, tm, d), lambda b, m: (b, m, 0)),
        compiler_params=_cparams(("parallel", "parallel")),
        name="oproj",
    )(a, w, x, mod, ng)


def _rope(x, cos, sin_signed):
    return x * cos + pltpu.roll(x, NSA_HEAD_DIM // 2, 1) * sin_signed


def _nsa_prep_kernel(kc_ref, vc_ref, ks_ref, vs_ref, kw_ref, vw_ref, pe_ref, w1_ref, w2_ref, cos_ref, sin_ref,
                     kco_ref, vco_ref, kso_ref, vso_ref, kwo_ref, vwo_ref):
    s = kc_ref.shape[0]
    nrow = s // CMP_STRIDE
    half = CMP_BLOCK // CMP_STRIDE

    def compress(t_ref, idx):
        streams = [t_ref[pl.ds(j, nrow, stride=CMP_STRIDE), :] for j in range(CMP_STRIDE)]
        hidden = None
        for h in range(half):
            flat = jnp.concatenate(
                [(streams[j] + pe_ref[idx, h * CMP_STRIDE + j:h * CMP_STRIDE + j + 1, :]).astype(BF16)
                 for j in range(CMP_STRIDE)], axis=1)
            w = w1_ref[idx, h * CMP_STRIDE * NSA_HEAD_DIM:(h + 1) * CMP_STRIDE * NSA_HEAD_DIM, :]
            part = _dot(flat, w)
            if h:
                part = pltpu.roll(part, nrow - h, 0)
            hidden = part if hidden is None else hidden + part
        return _dot(_silu(hidden).astype(BF16), w2_ref[idx])

    kco_ref[...] = compress(kc_ref, 0).astype(kco_ref.dtype)
    vco_ref[...] = compress(vc_ref, 1).astype(vco_ref.dtype)
    cos = cos_ref[...]
    sin = sin_ref[...]
    kso_ref[...] = _rope(ks_ref[...], cos, sin).astype(BF16)
    kwo_ref[...] = _rope(kw_ref[...], cos, sin).astype(BF16)
    vso_ref[...] = vs_ref[...].T.astype(BF16)
    vwo_ref[...] = vw_ref[...].T.astype(BF16)


def _nsa_prep_call(main, pe, w1, w2, cos, sin, *, layer):
    bsz, s, _ = main.shape
    g, dh = NSA_KV_GROUPS, NSA_HEAD_DIM
    nrow = s // CMP_STRIDE
    q_blocks = NSA_Q_DIM // dh

    def col(k):
        return pl.BlockSpec((None, s, dh), lambda b, gi, k=k: (b, 0, q_blocks + k * g + gi))

    def full(a):
        return pl.BlockSpec(a.shape, lambda b, gi, nd=a.ndim: (0,) * nd)

    def stacked(a):
        return pl.BlockSpec((None,) + a.shape[1:], lambda b, gi, nd=a.ndim: (layer,) + (0,) * (nd - 1))

    def out(rows, cols):
        return (jax.ShapeDtypeStruct((bsz, g, rows, cols), BF16),
                pl.BlockSpec((None, None, rows, cols), lambda b, gi: (b, gi, 0, 0)))

    outs = [out(nrow, dh), out(nrow, dh), out(s, dh), out(dh, s), out(s, dh), out(dh, s)]
    return pl.pallas_call(
        _nsa_prep_kernel,
        out_shape=tuple(o[0] for o in outs),
        grid=(bsz, g),
        in_specs=[col(0), col(1), col(2), col(3), col(4), col(5), stacked(pe), stacked(w1), stacked(w2),
                  full(cos), full(sin)],
        out_specs=tuple(o[1] for o in outs),
        compiler_params=_cparams(("parallel", "parallel")),
        name="nsa_prep",
    )(main, main, main, main, main, main, pe, w1, w2, cos, sin)


def _nsa_attn_kernel(q_ref, gl_ref, cos_ref, sin_ref, kc_ref, vc_ref, ks_ref, vst_ref, kw_ref, vwt_ref,
                     cover_ref, o_ref, qrot_scr, selb_scr, m_scr, l_scr, alpha_scr, acc_scr, out_scr, s_scr, p_scr,
                     *, tq, tk, n_sel):
    qi = pl.program_id(2)
    dh = NSA_HEAD_DIM
    scale = dh ** -0.5 * LOG2_E
    pos_q = qi * tq + lax.broadcasted_iota(jnp.int32, (1, tq), 1)
    cos = cos_ref[...]
    sin = sin_ref[...]
    gates = jax.nn.sigmoid(gl_ref[...].T[:4 * NSA_HPG, :])

    crow = lax.broadcasted_iota(jnp.int32, (LANES, 1), 0)
    cmask = crow * CMP_STRIDE + (CMP_BLOCK - 1) <= pos_q
    kc = kc_ref[...]
    vct = vc_ref[...].astype(F32).T.astype(BF16)
    p_sum = jnp.zeros((LANES, tq), F32)
    for r in range(NSA_HPG):
        qt = q_ref[:, r * dh:(r + 1) * dh].T * scale
        sc = jnp.where(cmask, _dot(kc, qt.astype(BF16)), NEG_INF)
        e = jnp.exp2(sc - jnp.max(sc, axis=0, keepdims=True))
        p = jnp.where(cmask, e * (1.0 / jnp.sum(e, axis=0, keepdims=True)), 0.0)
        p_sum = p_sum + p
        out_scr[r] = gates[3 * r:3 * r + 1, :] * _dot(vct, p.astype(BF16))
        rot = jnp.concatenate([qt[dh // 2:], qt[:dh // 2]], axis=0)
        qrot_scr[r] = (qt * cos + rot * sin).astype(BF16)

    imp = jnp.dot(cover_ref[...], p_sum, precision=HIGHEST, preferred_element_type=F32)[:n_sel]
    jrow = lax.broadcasted_iota(jnp.int32, (n_sel, 1), 0)
    q_blk = pos_q // SEL_BLOCK
    causal = jrow <= q_blk
    forced = (jrow == 0) | (causal & (jrow > q_blk - N_LOCAL_BLOCKS))
    imp = jnp.where(forced, FORCE_SCORE, jnp.where(causal, imp, -1.0))
    selb_scr[...] = imp
    top_k = min(SEL_TOPK, n_sel)
    group = 4
    n_causal = (qi * tq + tq) // SEL_BLOCK
    n_groups = jnp.where(n_causal <= top_k, 0, n_causal // group)

    def rank_body(g, rank):
        for k in range(group):
            jp = g * group + k
            other = selb_scr[pl.ds(jp, 1), :]
            tie = jnp.where(jrow > jp, 1.0, 0.0)
            rank = rank + jnp.where(other > imp, 1.0, jnp.where(other == imp, tie, 0.0))
        return rank

    rank = lax.fori_loop(0, n_groups, rank_body, jnp.zeros((n_sel, tq), F32))
    keep = (rank < float(top_k)) & (imp >= 0.0)
    selb_scr[...] = jnp.where(keep, 0.0, NEG_INF)

    def reset():
        m_scr[...] = jnp.full(m_scr.shape, NEG_INF, F32)
        l_scr[...] = jnp.zeros(l_scr.shape, F32)
        acc_scr[...] = jnp.zeros(acc_scr.shape, F32)
        alpha_scr[...] = jnp.ones(alpha_scr.shape, F32)
        p_scr[...] = jnp.zeros(p_scr.shape, BF16)

    def tile0(kt):
        return pl.multiple_of(jnp.maximum(kt, 0) * tk, tk)

    def apply_pending(vt_ref, kt):
        slot = kt & 1
        vt =vt_ref[:, pl.ds(tile0(kt), tk)]
        for r in range(NSA_HPG):
            acc_scr[r] = alpha_scr[slot, r] * acc_scr[r] + _dot(vt, p_scr[slot, r])

    def flash_step(k_ref, vt_ref, kt, bias):
        slot = kt & 1
        k_tile = k_ref[pl.ds(tile0(kt), tk), :]
        for r in range(NSA_HPG):
            s_scr[r] = _dot(k_tile, qrot_scr[r])
        apply_pending(vt_ref, kt - 1)
        for r in range(NSA_HPG):
            sc = s_scr[r]
            if bias is not None:
                sc = sc + bias
            m_old = m_scr[r]
            m_new = jnp.maximum(m_old, jnp.max(sc, axis=0, keepdims=True))
            alpha = jnp.exp2(m_old - m_new)
            p = jnp.exp2(sc - m_new)
            l_scr[r] = alpha * l_scr[r] + jnp.sum(p, axis=0, keepdims=True)
            alpha_scr[slot, r] = alpha
            p_scr[slot, r] = p.astype(BF16)
            m_scr[r] = m_new

    def finish(branch):
        for r in range(NSA_HPG):
            w = gates[3 * r + branch:3 * r + branch + 1, :] / l_scr[r]
            out_scr[r] = out_scr[r] + w * acc_scr[r]

    def sel_bias(kt):
        per_tile = tk // SEL_BLOCK
        rows = [selb_scr[pl.ds(kt * per_tile + j, 1), :] for j in range(per_tile)]
        return jnp.concatenate([jnp.broadcast_to(row, (SEL_BLOCK, tq)) for row in rows], axis=0)

    def key_pos(kt):
        return kt * tk + lax.broadcasted_iota(jnp.int32, (tk, 1), 0)

    causal_bias = jnp.where(key_pos(qi) <= pos_q, 0.0, NEG_INF)

    reset()

    def sel_body(kt, carry):
        flash_step(ks_ref, vst_ref, kt, sel_bias(kt))
        return carry

    lax.fori_loop(0, qi, sel_body, 0)
    flash_step(ks_ref, vst_ref, qi, sel_bias(qi) + causal_bias)
    apply_pending(vst_ref, qi)
    finish(1)

    reset()
    n_back = WINDOW // tk
    for back in range(n_back, 0, -1):
        @pl.when(qi >= back)
        def _(back=back):
            bias = jnp.where(pos_q - key_pos(qi - back) < WINDOW, 0.0, NEG_INF) if back == n_back else None
            flash_step(kw_ref, vwt_ref, qi - back, bias)
    flash_step(kw_ref, vwt_ref, qi, causal_bias)
    apply_pending(vwt_ref, qi)
    finish(2)

    for r in range(NSA_HPG):
        o_ref[:, r * dh:(r + 1) * dh] = out_scr[r].T.astype(o_ref.dtype)


def _nsa_attn_call(main, gl, cos_t, sin_t, kc, vc, ks, vst, kw, vwt, cover_t):
    bsz, s, _ = main.shape
    g, r, dh = NSA_KV_GROUPS, NSA_HPG, NSA_HEAD_DIM
    tq = tk = 256
    assert WINDOW % tk == 0 and tk % SEL_BLOCK == 0 and s % tq == 0
    nrow = kc.shape[2]
    n_sel = s // SEL_BLOCK

    def kv(rows, cols):
        return pl.BlockSpec((None, None, rows, cols), lambda b, gi, qi: (b, gi, 0, 0))

    return pl.pallas_call(
        functools.partial(_nsa_attn_kernel, tq=tq, tk=tk, n_sel=n_sel),
        out_shape=jax.ShapeDtypeStruct((bsz, s, NSA_Q_DIM), BF16),
        grid=(bsz, g, s // tq),
        in_specs=[
            pl.BlockSpec((None, tq, r * dh), lambda b, gi, qi: (b, qi, gi)),
            pl.BlockSpec((None, tq, LANES), lambda b, gi, qi: (b, qi, gi)),
            pl.BlockSpec((dh, tq), lambda b, gi, qi: (0, qi)),
            pl.BlockSpec((dh, tq), lambda b, gi, qi: (0, qi)),
            kv(nrow, dh), kv(nrow, dh), kv(s, dh), kv(dh, s), kv(s, dh), kv(dh, s),
            pl.BlockSpec(cover_t.shape, lambda b, gi, qi: (0, 0)),
        ],
        out_specs=pl.BlockSpec((None, tq, r * dh), lambda b, gi, qi: (b, qi, gi)),
        scratch_shapes=[
            pltpu.VMEM((r, dh, tq), BF16),
            pltpu.VMEM((n_sel, tq), F32),
            pltpu.VMEM((r, 1, tq), F32),
            pltpu.VMEM((r, 1, tq), F32),
            pltpu.VMEM((2, r, 1, tq), F32),
            pltpu.VMEM((r, dh, tq), F32),
            pltpu.VMEM((r, dh, tq), F32),
            pltpu.VMEM((r, tk, tq), F32),
            pltpu.VMEM((2, r, tk, tq), BF16),
        ],
        compiler_params=_cparams(("parallel", "parallel", "arbitrary")),
        name="nsa_attn",
    )(main, gl, cos_t, sin_t, kc, vc, ks, vst, kw, vwt, cover_t)


def _nsa_tables(s):
    half = NSA_HEAD_DIM // 2
    freqs = ROPE_THETA ** (-jnp.arange(half, dtype=F32) / half)
    ang = jnp.arange(s).astype(F32)[:, None] * freqs[None, :]
    cos, sin = jnp.cos(ang), jnp.sin(ang)
    cos_full = jnp.concatenate([cos, cos], axis=-1)
    sin_signed = jnp.concatenate([-sin, sin], axis=-1)
    n_cmp = (s - CMP_BLOCK) // CMP_STRIDE + 1
    cs = np.arange(LANES)[None, :] * CMP_STRIDE
    js = np.arange(LANES)[:, None]
    cover_t = ((cs < (js + 1) * SEL_BLOCK) & (cs + CMP_BLOCK > js * SEL_BLOCK)
               & (np.arange(LANES)[None, :] < n_cmp) & (js < s // SEL_BLOCK)).astype(np.float32)
    return cos_full, sin_signed, cos_full.T, sin_signed.T, jnp.asarray(cover_t, F32)


def _nsa_layer(x, mod, ng, w_in, cmp_pos, cmp_w1, cmp_w2, w_out, tables, *, layer):
    cos, sin, cos_t, sin_t, cover_t = tables
    g, r = NSA_KV_GROUPS, NSA_HPG
    w_gate = w_in[layer, :, NSA_MAIN_DIM:].reshape(D_MODEL, g, 3 * r)
    w_gate = jnp.pad(w_gate, ((0, 0), (0, 0), (0, LANES - 3 * r))).reshape(D_MODEL, g * LANES)
    main, gl = _proj_call(x, mod, ng, w_in, w_gate, layer=layer, n=NSA_MAIN_DIM, mrow=3, grow=2)
    kc, vc, ks, vst, kw, vwt = _nsa_prep_call(main, cmp_pos, cmp_w1, cmp_w2, cos, sin, layer=layer)
    o = _nsa_attn_call(main, gl, cos_t, sin_t, kc, vc, ks, vst, kw, vwt, cover_t)
    return _oproj_call(o, w_out, x, mod, ng, layer=layer, mrow=5, grow=3)


def _mlstm_kernel(q_ref, k_ref, v_ref, og_ref, gi_ref, gf_ref, bi_ref, bf_ref, cwq_ref, cwk_ref, cbq_ref, cbk_ref,
                  gain_ref, o_ref, qpad_scr, kpad_scr, ct_scr, b_scr, rowi_scr, mloc_scr, bcol_scr, ecol_scr):
    s, dk = q_ref.shape
    dv = v_ref.shape[1]
    chunk = ML_CHUNK
    nc = s // chunk
    pad = qpad_scr.shape[0] - s

    for src, dst in ((q_ref, qpad_scr), (k_ref, kpad_scr)):
        dst[:pad] = jnp.zeros((pad, dk), F32)
        dst[pad:] = src[...]

    def conv_silu(pad_ref, w_ref, b_ref, r0):
        win = pad_ref[pl.ds(r0, chunk + pad), :]
        acc = win[pad:] * w_ref[ML_CONV - 1:ML_CONV, :] + b_ref[...]
        for d in range(1, ML_CONV):
            acc = acc + pltpu.roll(win, d, 0)[pad:] * w_ref[ML_CONV - 1 - d:ML_CONV - d, :]
        return _silu(acc)

    ig = gi_ref[...] + bi_ref[...]
    fg = gf_ref[...] + bf_ref[...]
    logf = jnp.minimum(fg, 0.0) - jnp.log1p(jnp.exp(-jnp.abs(fg)))
    tri_r = lax.broadcasted_iota(jnp.int32, (chunk, chunk), 0)
    tri_c = lax.broadcasted_iota(jnp.int32, (chunk, chunk), 1)
    upper = jnp.where(tri_r <= tri_c, 1.0, 0.0)
    b = jnp.dot(logf, upper, precision=HIGHEST, preferred_element_type=F32)
    w_end = b[:, chunk - 1:chunk] - b + ig
    m_loc = jnp.max(w_end, axis=-1, keepdims=True)
    e_end = jnp.exp(w_end - m_loc)
    b_scr[...] = b
    rowi_scr[...] = ig - b
    mloc_scr[...] = jnp.broadcast_to(m_loc, mloc_scr.shape)
    eye = jnp.where(tri_r == tri_c, 1.0, 0.0)
    cols = lax.dot_general(eye, jnp.concatenate([b, e_end], axis=0), NT_DIMS, precision=HIGHEST,
                           preferred_element_type=F32)
    for c in range(nc):
        bcol_scr[c] = jnp.broadcast_to(cols[:, c:c + 1], (chunk, LANES))
        ecol_scr[c] = jnp.broadcast_to(cols[:, nc + c:nc + c + 1], (chunk, LANES))

    ct_scr[...] = jnp.zeros(ct_scr.shape, F32)
    lower = tri_c <= tri_r
    gain = gain_ref[...]

    def body(c, m_prev):
        r0 = pl.multiple_of(c * chunk, chunk)
        q = conv_silu(qpad_scr, cwq_ref, cbq_ref, r0).astype(BF16)
        k = conv_silu(kpad_scr, cwk_ref, cbk_ref, r0) * (dk ** -0.5)
        va = jnp.concatenate([v_ref[pl.ds(r0, chunk), :].astype(BF16), jnp.ones((chunk, LANES), BF16)], axis=1)
        b_row = b_scr[pl.ds(c, 1), :]
        b_tot = b_row[:, chunk - 1:chunk]
        m_l = mloc_scr[pl.ds(c, 1), :][:, :1]
        bcol = bcol_scr[c]
        log_intra = jnp.where(lower, bcol[:, :chunk] + rowi_scr[pl.ds(c, 1), :], NEG_INF)
        log_inter = bcol[:, :1] + m_prev
        m_t = jnp.maximum(log_inter, jnp.max(log_intra, axis=-1, keepdims=True))
        e_inter = jnp.exp(log_inter - m_t)
        qk = _dot_nt(q, k.astype(BF16)) * jnp.exp(log_intra - m_t)
        ct = ct_scr[...]
        tot = _dot(qk.astype(BF16), va) + e_inter * _dot(q, ct.astype(BF16))
        h = tot[:, :dv] / jnp.maximum(jnp.abs(tot[:, dv:dv + 1]), jnp.exp(-m_t))
        hn = _rms(h) * gain
        o_ref[pl.ds(r0, chunk), :] = (jax.nn.sigmoid(og_ref[pl.ds(r0, chunk), :]) * hn).astype(o_ref.dtype)
        m_new = jnp.maximum(b_tot + m_prev, m_l)
        c_loc = lax.dot_general((k * ecol_scr[c]).astype(BF16), va, TN_DIMS, preferred_element_type=F32)
        ct_scr[...] = jnp.exp(b_tot + m_prev - m_new) * ct + jnp.exp(m_l - m_new) * c_loc
        return m_new

    lax.fori_loop(0, nc, body, jnp.zeros((1, 1), F32), unroll=4)


def _mlstm_call(main, gates_t, gate_b, conv_w, conv_b, gain):
    bsz, s, _ = main.shape
    h, dk, dv = ML_HEADS, ML_QK_DIM, ML_V_DIM
    chunk = ML_CHUNK
    nc = s // chunk
    vblk0 = ML_QK_COLS // dv

    def gate(off):
        return pl.BlockSpec((None, None, nc, chunk), lambda b, hi: (b, off + hi, 0, 0))

    def bias(off):
        return pl.BlockSpec((None, 1, 1), lambda b, hi: (off + hi, 0, 0))

    return pl.pallas_call(
        _mlstm_kernel,
        out_shape=jax.ShapeDtypeStruct((bsz, s, h * dv), BF16),
        grid=(bsz, h),
        in_specs=[
            pl.BlockSpec((None, s, dk), lambda b, hi: (b, 0, hi)),
            pl.BlockSpec((None, s, dk), lambda b, hi: (b, 0, h + hi)),
            pl.BlockSpec((None, s, dv), lambda b, hi: (b, 0, vblk0 + hi)),
            pl.BlockSpec((None, s, dv), lambda b, hi: (b, 0, vblk0 + h + hi)),
            gate(0), gate(h), bias(0), bias(h),
            pl.BlockSpec((ML_CONV, dk), lambda b, hi: (0, hi)),
            pl.BlockSpec((ML_CONV, dk), lambda b, hi: (0, h + hi)),
            pl.BlockSpec((1, dk), lambda b, hi: (0, hi)),
            pl.BlockSpec((1, dk), lambda b, hi: (0, h + hi)),
            pl.BlockSpec((1, dv), lambda b, hi: (0, hi)),
        ],
        out_specs=pl.BlockSpec((None, s, dv), lambda b, hi: (b, 0, hi)),
        scratch_shapes=[
            pltpu.VMEM((s + SUBLANES, dk), F32),
            pltpu.VMEM((s + SUBLANES, dk), F32),
            pltpu.VMEM((dk, dv + LANES), F32),
            pltpu.VMEM((nc, chunk), F32),
            pltpu.VMEM((nc, chunk), F32),
            pltpu.VMEM((nc, LANES), F32),
            pltpu.VMEM((nc, chunk, LANES), F32),
            pltpu.VMEM((nc, chunk, LANES), F32),
        ],
        compiler_params=_cparams(("parallel", "parallel")),
        name="mlstm",
    )(main, main, main, main, gates_t, gates_t, gate_b, gate_b, conv_w, conv_w, conv_b, conv_b, gain)


def _mlstm_layer(x, mod, ng, w_in, conv_w, conv_b, gate_b, mh_gain, w_out, *, layer):
    bsz, s, _ = x.shape
    ngate = 2 * ML_HEADS
    w_gate = jnp.pad(w_in[layer, :, ML_MAIN_DIM:], ((0, 0), (0, LANES - ngate)))
    main, gl = _proj_call(x, mod, ng, w_in, w_gate, layer=layer, n=ML_MAIN_DIM, mrow=3, grow=2)
    gates_t = jnp.swapaxes(gl[:, :, :ngate], 1, 2).reshape(bsz, ngate, s // ML_CHUNK, ML_CHUNK)
    o = _mlstm_call(main, gates_t, gate_b.reshape(ngate, 1, 1), conv_w, conv_b.reshape(1, -1),
                    mh_gain.reshape(1, -1))
    return _oproj_call(o, w_out, x, mod, ng, layer=layer, mrow=5, grow=3)


def kernel(x, c, mod_w, mod_b, norm_g, ffn_pre_w_in, ffn_pre_w_out, ffn_post_w_in, ffn_post_w_out, nsa_w_in, nsa_cmp_pos, nsa_cmp_w1, nsa_cmp_w2, nsa_w_out, ml_w_in, ml_conv_w, ml_conv_b, ml_gate_b, ml_mh_gain, ml_w_out):
    bsz, s, d = x.shape
    depth = mod_w.shape[0]
    mods = _mod_call(c, mod_w, mod_b).reshape(depth, bsz, N_MOD, d)
    tables = _nsa_tables(s)
    pre_in, pre_out = ffn_pre_w_in.astype(BF16), ffn_pre_w_out.astype(BF16)
    post_in, post_out = ffn_post_w_in.astype(BF16), ffn_post_w_out.astype(BF16)
    nsa_in, nsa_out = nsa_w_in.astype(BF16), nsa_w_out.astype(BF16)
    nsa_w1, nsa_w2 = nsa_cmp_w1.astype(BF16), nsa_cmp_w2.astype(BF16)
    ml_in, ml_out = ml_w_in.astype(BF16), ml_w_out.astype(BF16)
    for i in range(depth):
        mod, ng = mods[i], norm_g[i]
        x = _ffn_call(x, mod, ng, pre_in, pre_out, layer=i, mrow=0, grow=0)
        j = i // 2
        if i % 2 == 0:
            x = _nsa_layer(x, mod, ng, nsa_in, nsa_cmp_pos, nsa_w1, nsa_w2, nsa_out, tables, layer=j)
        else:
            x = _mlstm_layer(x, mod, ng, ml_in, ml_conv_w[j], ml_conv_b[j], ml_gate_b[j], ml_mh_gain[j], ml_out,
                             layer=j)
        x = _ffn_call(x, mod, ng, post_in, post_out, layer=i, mrow=6, grow=4)
    return x
```

```python
import functools

import numpy as np
import jax
import jax.numpy as jnp
from jax import lax
from jax.experimental import pallas as pl
from jax.experimental.pallas import tpu as pltpu

F32 = jnp.float32
BF16 = jnp.bfloat16
HIGHEST = lax.Precision.HIGHEST

D_MODEL = 2048
DEPTH = 4
D_FF = 5632
FFN_RES = 0.5
N_MOD = 9
RMS_EPS = 1e-6
NEG_INF = -1e30
LOG2_E = 1.4426950408889634

NSA_HEADS = 16
NSA_HEAD_DIM = 128
NSA_KV_GROUPS = 4
NSA_HPG = NSA_HEADS // NSA_KV_GROUPS
NSA_Q_DIM = NSA_HEADS * NSA_HEAD_DIM
NSA_KV_DIM = NSA_KV_GROUPS * NSA_HEAD_DIM
NSA_MAIN_DIM = NSA_Q_DIM + 6 * NSA_KV_DIM
CMP_BLOCK = 32
CMP_STRIDE = 16
CMP_HIDDEN = 512
SEL_BLOCK = 64
SEL_TOPK = 16
N_LOCAL_BLOCKS = 2
FORCE_SCORE = 1e9
WINDOW = 512
ROPE_THETA = 10000.0

ML_HEADS = 8
ML_QK_DIM = 128
ML_V_DIM = 256
ML_CONV = 4
ML_CHUNK = 64
ML_QK_COLS = 2 * ML_HEADS * ML_QK_DIM
ML_MAIN_DIM = ML_QK_COLS + 2 * ML_HEADS * ML_V_DIM

LANES = 128
SUBLANES = 8
VMEM_LIMIT = 56 * 1024 * 1024

NT_DIMS = (((1,), (1,)), ((), ()))
TN_DIMS = (((0,), (0,)), ((), ()))


def _cparams(sem):
    return pltpu.CompilerParams(dimension_semantics=sem, vmem_limit_bytes=VMEM_LIMIT)


def _silu(x):
    return x * jax.nn.sigmoid(x)


def _rms(x):
    return x * lax.rsqrt(jnp.mean(x * x, axis=-1, keepdims=True) + RMS_EPS)


def _dot(a, b):
    return jnp.dot(a, b, preferred_element_type=F32)


def _dot_nt(a, b):
    return lax.dot_general(a, b, NT_DIMS, preferred_element_type=F32)


def _mod_kernel(c_ref, w_ref, b_ref, o_ref):
    ca = _silu(c_ref[...]).astype(BF16)
    o_ref[...] = _dot(ca, w_ref[...].astype(BF16)) + b_ref[...]


def _mod_call(c, mod_w, mod_b):
    depth, d, n = mod_w.shape
    bsz = c.shape[0]
    tn = 1024
    return pl.pallas_call(
        _mod_kernel,
        out_shape=jax.ShapeDtypeStruct((depth, bsz, n), F32),
        grid=(depth, n // tn),
        in_specs=[
            pl.BlockSpec((bsz, d), lambda i, j: (0, 0)),
            pl.BlockSpec((None, d, tn), lambda i, j: (i, 0, j)),
            pl.BlockSpec((None, 1, tn), lambda i, j: (i, 0, j)),
        ],
        out_specs=pl.BlockSpec((None, bsz, tn), lambda i, j: (i, 0, j)),
        compiler_params=_cparams(("parallel", "parallel")),
        name="mod",
    )(c, mod_w, mod_b.reshape(depth, 1, n))


def _prenorm(x, mod_ref, ng_ref, mrow, grow):
    gain = ng_ref[grow:grow + 1, :] * (1.0 + mod_ref[mrow + 1:mrow + 2, :])
    return _rms(x) * gain + mod_ref[mrow:mrow + 1, :]


ROW_CHUNK = 16
ROW_UNROLL = 4


def _row_loop(n_rows, body):
    def step(i, carry):
        body(pl.ds(pl.multiple_of(i * ROW_CHUNK, ROW_CHUNK), ROW_CHUNK))
        return carry
    lax.fori_loop(0, n_rows // ROW_CHUNK, step, 0, unroll=ROW_UNROLL)


def _ffn_kernel(x_ref, mod_ref, ng_ref, wg_ref, wu_ref, wo_ref, o_ref, xn_ref, acc_ref, *, mrow, grow):
    j = pl.program_id(2)

    @pl.when(j == 0)
    def _():
        gain = ng_ref[grow:grow + 1, :] * (1.0 + mod_ref[mrow + 1:mrow + 2, :])
        shift = mod_ref[mrow:mrow + 1, :]

        def prenorm_rows(rows):
            xn_ref[rows, :] = (_rms(x_ref[rows, :]) * gain + shift).astype(BF16)

        _row_loop(x_ref.shape[0], prenorm_rows)
        acc_ref[...] = jnp.zeros_like(acc_ref)

    xn = xn_ref[...]
    g = _dot(xn, wg_ref[...])
    u = _dot(xn, wu_ref[...])
    acc_ref[...] += _dot((_silu(g) * u).astype(BF16), wo_ref[...])

    @pl.when(j == pl.num_programs(2) - 1)
    def _():
        gain = ng_ref[grow + 1:grow + 2, :] * (FFN_RES * mod_ref[mrow + 2:mrow + 3, :])

        def finish_rows(rows):
            o_ref[rows, :] = x_ref[rows, :] + _rms(acc_ref[rows, :]) * gain

        _row_loop(x_ref.shape[0], finish_rows)


def _ffn_call(x, mod, ng, w_in, w_out, *, layer, mrow, grow):
    bsz, s, d = x.shape
    dff = w_out.shape[1]
    tm, tf = 512, 512
    nf = dff // tf
    return pl.pallas_call(
        functools.partial(_ffn_kernel, mrow=mrow, grow=grow),
        out_shape=jax.ShapeDtypeStruct(x.shape, F32),
        grid=(bsz, s // tm, nf),
        in_specs=[
            pl.BlockSpec((None, tm, d), lambda b, m, j: (b, m, 0)),
            pl.BlockSpec((None, N_MOD, d), lambda b, m, j: (b, 0, 0)),
            pl.BlockSpec(ng.shape, lambda b, m, j: (0, 0)),
            pl.BlockSpec((None, d, tf), lambda b, m, j: (layer, 0, j)),
            pl.BlockSpec((None, d, tf), lambda b, m, j: (layer, 0, nf + j)),
            pl.BlockSpec((None, tf, d), lambda b, m, j: (layer, j, 0)),
        ],
        out_specs=pl.BlockSpec((None, tm, d), lambda b, m, j: (b, m, 0)),
        scratch_shapes=[pltpu.VMEM((tm, d), BF16), pltpu.VMEM((tm, d), F32)],
        compiler_params=_cparams(("parallel", "parallel", "arbitrary")),
        name="ffn",
    )(x, mod, ng, w_in, w_in, w_out)


def _proj_kernel(x_ref, mod_ref, ng_ref, w_ref, wgate_ref, o_ref, og_ref, xn_ref, *, mrow, grow):
    @pl.when(pl.program_id(2) == 0)
    def _():
        xn = _prenorm(x_ref[...], mod_ref, ng_ref, mrow, grow).astype(BF16)
        xn_ref[...] = xn
        og_ref[...] = _dot(xn, wgate_ref[...])

    o_ref[...] = _dot(xn_ref[...], w_ref[...])


def _proj_call(x, mod, ng, w_in, w_gate, *, layer, n, mrow, grow):
    bsz, s, d = x.shape
    ngate = w_gate.shape[1]
    tm, tn = 1024, 1024
    return pl.pallas_call(
        functools.partial(_proj_kernel, mrow=mrow, grow=grow),
        out_shape=(jax.ShapeDtypeStruct((bsz, s, n), F32), jax.ShapeDtypeStruct((bsz, s, ngate), F32)),
        grid=(bsz, s // tm, n // tn),
        in_specs=[
            pl.BlockSpec((None, tm, d), lambda b, m, j: (b, m, 0)),
            pl.BlockSpec((None, N_MOD, d), lambda b, m, j: (b, 0, 0)),
            pl.BlockSpec(ng.shape, lambda b, m, j: (0, 0)),
            pl.BlockSpec((None, d, tn), lambda b, m, j: (layer, 0, j)),
            pl.BlockSpec((d, ngate), lambda b, m, j: (0, 0)),
        ],
        out_specs=(pl.BlockSpec((None, tm, tn), lambda b, m, j: (b, m, j)),
                   pl.BlockSpec((None, tm, ngate), lambda b, m, j: (b, m, 0))),
        scratch_shapes=[pltpu.VMEM((tm, d), BF16)],
        compiler_params=_cparams(("parallel", "parallel", "arbitrary")),
        name="proj",
    )(x, mod, ng, w_in, w_gate)


def _oproj_kernel(a_ref, w_ref, x_ref, mod_ref, ng_ref, o_ref, *, mrow, grow):
    yn = _rms(_dot(a_ref[...], w_ref[...])) * ng_ref[grow:grow + 1, :]
    o_ref[...] = x_ref[...] + mod_ref[mrow:mrow + 1, :] * yn


def _oproj_call(a, w, x, mod, ng, *, layer, mrow, grow):
    bsz, s, d = x.shape
    k = a.shape[-1]
    tm = 512
    return pl.pallas_call(
        functools.partial(_oproj_kernel, mrow=mrow, grow=grow),
        out_shape=jax.ShapeDtypeStruct(x.shape, F32),
        grid=(bsz, s // tm),
        in_specs=[
            pl.BlockSpec((None, tm, k), lambda b, m: (b, m, 0)),
            pl.BlockSpec((None, k, d), lambda b, m: (layer, 0, 0)),
            pl.BlockSpec((None, tm, d), lambda b, m: (b, m, 0)),
            pl.BlockSpec((None, N_MOD, d), lambda b, m: (b, 0, 0)),
            pl.BlockSpec(ng.shape, lambda b, m: (0, 0)),
        ],
        out_specs=pl.BlockSpec((None, tm, d), lambda b, m: (b, m, 0)),
        compiler_params=_cparams(("parallel", "parallel")),
        name="oproj",
    )(a, w, x, mod, ng)


def _rope(x, cos, sin_signed):
    return x * cos + pltpu.roll(x, NSA_HEAD_DIM // 2, 1) * sin_signed


def _nsa_prep_kernel(kc_ref, vc_ref, ks_ref, vs_ref, kw_ref, vw_ref, pe_ref, w1_ref, w2_ref, cos_ref, sin_ref,
                     kco_ref, vco_ref, kso_ref, vso_ref, kwo_ref, vwo_ref):
    s = kc_ref.shape[0]
    nrow = s // CMP_STRIDE
    half = CMP_BLOCK // CMP_STRIDE

    def compress(t_ref, idx):
        streams = [t_ref[pl.ds(j, nrow, stride=CMP_STRIDE), :] for j in range(CMP_STRIDE)]
        hidden = None
        for h in range(half):
            flat = jnp.concatenate(
                [(streams[j] + pe_ref[idx, h * CMP_STRIDE + j:h * CMP_STRIDE + j + 1, :]).astype(BF16)
                 for j in range(CMP_STRIDE)], axis=1)
            w = w1_ref[idx, h * CMP_STRIDE * NSA_HEAD_DIM:(h + 1) * CMP_STRIDE * NSA_HEAD_DIM, :]
            part = _dot(flat, w)
            if h:
                part = pltpu.roll(part, nrow - h, 0)
            hidden = part if hidden is None else hidden + part
        return _dot(_silu(hidden).astype(BF16), w2_ref[idx])

    kco_ref[...] = compress(kc_ref, 0).astype(kco_ref.dtype)
    vco_ref[...] = compress(vc_ref, 1).astype(vco_ref.dtype)
    cos = cos_ref[...]
    sin = sin_ref[...]
    kso_ref[...] = _rope(ks_ref[...], cos, sin).astype(BF16)
    kwo_ref[...] = _rope(kw_ref[...], cos, sin).astype(BF16)
    vso_ref[...] = vs_ref[...].T.astype(BF16)
    vwo_ref[...] = vw_ref[...].T.astype(BF16)


def _nsa_prep_call(main, pe, w1, w2, cos, sin, *, layer):
    bsz, s, _ = main.shape
    g, dh = NSA_KV_GROUPS, NSA_HEAD_DIM
    nrow = s // CMP_STRIDE
    q_blocks = NSA_Q_DIM // dh

    def col(k):
        return pl.BlockSpec((None, s, dh), lambda b, gi, k=k: (b, 0, q_blocks + k * g + gi))

    def full(a):
        return pl.BlockSpec(a.shape, lambda b, gi, nd=a.ndim: (0,) * nd)

    def stacked(a):
        return pl.BlockSpec((None,) + a.shape[1:], lambda b, gi, nd=a.ndim: (layer,) + (0,) * (nd - 1))

    def out(rows, cols):
        return (jax.ShapeDtypeStruct((bsz, g, rows, cols), BF16),
                pl.BlockSpec((None, None, rows, cols), lambda b, gi: (b, gi, 0, 0)))

    outs = [out(nrow, dh), out(nrow, dh), out(s, dh), out(dh, s), out(s, dh), out(dh, s)]
    return pl.pallas_call(
        _nsa_prep_kernel,
        out_shape=tuple(o[0] for o in outs),
        grid=(bsz, g),
        in_specs=[col(0), col(1), col(2), col(3), col(4), col(5), stacked(pe), stacked(w1), stacked(w2),
                  full(cos), full(sin)],
        out_specs=tuple(o[1] for o in outs),
        compiler_params=_cparams(("parallel", "parallel")),
        name="nsa_prep",
    )(main, main, main, main, main, main, pe, w1, w2, cos, sin)


def _nsa_attn_kernel(q_ref, gl_ref, cos_ref, sin_ref, kc_ref, vc_ref, ks_ref, vst_ref, kw_ref, vwt_ref,
                     cover_ref, o_ref, qrot_scr, selb_scr, m_scr, l_scr, alpha_scr, acc_scr, out_scr, s_scr, p_scr,
                     *, tq, tk, n_sel):
    qi = pl.program_id(2)
    dh = NSA_HEAD_DIM
    scale = dh ** -0.5 * LOG2_E
    pos_q = qi * tq + lax.broadcasted_iota(jnp.int32, (1, tq), 1)
    cos = cos_ref[...]
    sin = sin_ref[...]
    gates = jax.nn.sigmoid(gl_ref[...].T[:4 * NSA_HPG, :])

    crow = lax.broadcasted_iota(jnp.int32, (LANES, 1), 0)
    cmask = crow * CMP_STRIDE + (CMP_BLOCK - 1) <= pos_q
    kc = kc_ref[...]
    vct = vc_ref[...].astype(F32).T.astype(BF16)
    p_sum = jnp.zeros((LANES, tq), F32)
    for r in range(NSA_HPG):
        qt = q_ref[:, r * dh:(r + 1) * dh].T * scale
        sc = jnp.where(cmask, _dot(kc, qt.astype(BF16)), NEG_INF)
        e = jnp.exp2(sc - jnp.max(sc, axis=0, keepdims=True))
        p = jnp.where(cmask, e * (1.0 / jnp.sum(e, axis=0, keepdims=True)), 0.0)
        p_sum = p_sum + p
        out_scr[r] = gates[3 * r:3 * r + 1, :] * _dot(vct, p.astype(BF16))
        rot = jnp.concatenate([qt[dh // 2:], qt[:dh // 2]], axis=0)
        qrot_scr[r] = (qt * cos + rot * sin).astype(BF16)

    imp = jnp.dot(cover_ref[...], p_sum, precision=HIGHEST, preferred_element_type=F32)[:n_sel]
    jrow = lax.broadcasted_iota(jnp.int32, (n_sel, 1), 0)
    q_blk = pos_q // SEL_BLOCK
    causal = jrow <= q_blk
    forced = (jrow == 0) | (causal & (jrow > q_blk - N_LOCAL_BLOCKS))
    imp = jnp.where(forced, FORCE_SCORE, jnp.where(causal, imp, -1.0))
    selb_scr[...] = imp
    top_k = min(SEL_TOPK, n_sel)
    group = 4
    n_causal = (qi * tq + tq) // SEL_BLOCK
    n_groups = jnp.where(n_causal <= top_k, 0, n_causal // group)

    def rank_body(g, rank):
        for k in range(group):
            jp = g * group + k
            other = selb_scr[pl.ds(jp, 1), :]
            tie = jnp.where(jrow > jp, 1.0, 0.0)
            rank = rank + jnp.where(other > imp, 1.0, jnp.where(other == imp, tie, 0.0))
        return rank

    rank = lax.fori_loop(0, n_groups, rank_body, jnp.zeros((n_sel, tq), F32))
    keep = (rank < float(top_k)) & (imp >= 0.0)
    selb_scr[...] = jnp.where(keep, 0.0, NEG_INF)

    def reset():
        m_scr[...] = jnp.full(m_scr.shape, NEG_INF, F32)
        l_scr[...] = jnp.zeros(l_scr.shape, F32)
        acc_scr[...] = jnp.zeros(acc_scr.shape, F32)
        alpha_scr[...] = jnp.ones(alpha_scr.shape, F32)
        p_scr[...] = jnp.zeros(p_scr.shape, BF16)

    def tile0(kt):
        return pl.multiple_of(jnp.maximum(kt, 0) * tk, tk)

    def apply_pending(vt_ref, kt):
        slot = kt & 1
        vt = vt_ref[:, pl.ds(tile0(kt), tk)]
        for r in range(NSA_HPG):
            acc_scr[r] = alpha_scr[slot, r] * acc_scr[r] + _dot(vt, p_scr[slot, r])

    def flash_step(k_ref, vt_ref, kt, bias):
        slot = kt & 1
        k_tile = k_ref[pl.ds(tile0(kt), tk), :]
        for r in range(NSA_HPG):
            s_scr[r] = _dot(k_tile, qrot_scr[r])
        apply_pending(vt_ref, kt - 1)
        for r in range(NSA_HPG):
            sc = s_scr[r]
            if bias is not None:
                sc = sc + bias
            m_old = m_scr[r]
            m_new = jnp.maximum(m_old, jnp.max(sc, axis=0, keepdims=True))
            alpha = jnp.exp2(m_old - m_new)
            p = jnp.exp2(sc - m_new)
            l_scr[r] = alpha * l_scr[r] + jnp.sum(p, axis=0, keepdims=True)
            alpha_scr[slot, r] = alpha
            p_scr[slot, r] = p.astype(BF16)
            m_scr[r] = m_new

    def finish(branch):
        for r in range(NSA_HPG):
            w = gates[3 * r + branch:3 * r + branch + 1, :] / l_scr[r]
            out_scr[r] = out_scr[r] + w * acc_scr[r]

    def sel_bias(kt):
        per_tile = tk // SEL_BLOCK
        rows = [selb_scr[pl.ds(kt * per_tile + j, 1), :] for j in range(per_tile)]
        return jnp.concatenate([jnp.broadcast_to(row, (SEL_BLOCK, tq)) for row in rows], axis=0)

    def key_pos(kt):
        return kt * tk + lax.broadcasted_iota(jnp.int32, (tk, 1), 0)

    causal_bias = jnp.where(key_pos(qi) <= pos_q, 0.0, NEG_INF)

    reset()

    def sel_body(kt, carry):
        flash_step(ks_ref, vst_ref, kt, sel_bias(kt))
        return carry

    lax.fori_loop(0, qi, sel_body, 0)
    flash_step(ks_ref, vst_ref, qi, sel_bias(qi) + causal_bias)
    apply_pending(vst_ref, qi)
    finish(1)

    reset()
    n_back = WINDOW // tk
    for back in range(n_back, 0, -1):
        @pl.when(qi >= back)
        def _(back=back):
            bias = jnp.where(pos_q - key_pos(qi - back) < WINDOW, 0.0, NEG_INF) if back == n_back else None
            flash_step(kw_ref, vwt_ref, qi - back, bias)
    flash_step(kw_ref, vwt_ref, qi, causal_bias)
    apply_pending(vwt_ref, qi)
    finish(2)

    for r in range(NSA_HPG):
        o_ref[:, r * dh:(r + 1) * dh] = out_scr[r].T.astype(o_ref.dtype)


def _nsa_attn_call(main, gl, cos_t, sin_t, kc, vc, ks, vst, kw, vwt, cover_t):
    bsz, s, _ = main.shape
    g, r, dh = NSA_KV_GROUPS, NSA_HPG, NSA_HEAD_DIM
    tq = tk = 256
    assert WINDOW % tk == 0 and tk % SEL_BLOCK == 0 and s % tq == 0
    nrow = kc.shape[2]
    n_sel = s // SEL_BLOCK

    def kv(rows, cols):
        return pl.BlockSpec((None, None, rows, cols), lambda b, gi, qi: (b, gi, 0, 0))

    return pl.pallas_call(
        functools.partial(_nsa_attn_kernel, tq=tq, tk=tk, n_sel=n_sel),
        out_shape=jax.ShapeDtypeStruct((bsz, s, NSA_Q_DIM), BF16),
        grid=(bsz, g, s // tq),
        in_specs=[
            pl.BlockSpec((None, tq, r * dh), lambda b, gi, qi: (b, qi, gi)),
            pl.BlockSpec((None, tq, LANES), lambda b, gi, qi: (b, qi, gi)),
            pl.BlockSpec((dh, tq), lambda b, gi, qi: (0, qi)),
            pl.BlockSpec((dh, tq), lambda b, gi, qi: (0, qi)),
            kv(nrow, dh), kv(nrow, dh), kv(s, dh), kv(dh, s), kv(s, dh), kv(dh, s),
            pl.BlockSpec(cover_t.shape, lambda b, gi, qi: (0, 0)),
        ],
        out_specs=pl.BlockSpec((None, tq, r * dh), lambda b, gi, qi: (b, qi, gi)),
        scratch_shapes=[
            pltpu.VMEM((r, dh, tq), BF16),
            pltpu.VMEM((n_sel, tq), F32),
            pltpu.VMEM((r, 1, tq), F32),
            pltpu.VMEM((r, 1, tq), F32),
            pltpu.VMEM((2, r, 1, tq), F32),
            pltpu.VMEM((r, dh, tq), F32),
            pltpu.VMEM((r, dh, tq), F32),
            pltpu.VMEM((r, tk, tq), F32),
            pltpu.VMEM((2, r, tk, tq), BF16),
        ],
        compiler_params=_cparams(("parallel", "parallel", "arbitrary")),
        name="nsa_attn",
    )(main, gl, cos_t, sin_t, kc, vc, ks, vst, kw, vwt, cover_t)


def _nsa_tables(s):
    half = NSA_HEAD_DIM // 2
    freqs = ROPE_THETA ** (-jnp.arange(half, dtype=F32) / half)
    ang = jnp.arange(s).astype(F32)[:, None] * freqs[None, :]
    cos, sin = jnp.cos(ang), jnp.sin(ang)
    cos_full = jnp.concatenate([cos, cos], axis=-1)
    sin_signed = jnp.concatenate([-sin, sin], axis=-1)
    n_cmp = (s - CMP_BLOCK) // CMP_STRIDE + 1
    cs = np.arange(LANES)[None, :] * CMP_STRIDE
    js = np.arange(LANES)[:, None]
    cover_t = ((cs < (js + 1) * SEL_BLOCK) & (cs + CMP_BLOCK > js * SEL_BLOCK)
               & (np.arange(LANES)[None, :] < n_cmp) & (js < s // SEL_BLOCK)).astype(np.float32)
    return cos_full, sin_signed, cos_full.T, sin_signed.T, jnp.asarray(cover_t, F32)


def _nsa_layer(x, mod, ng, w_in, cmp_pos, cmp_w1, cmp_w2, w_out, tables, *, layer):
    cos, sin, cos_t, sin_t, cover_t = tables
    g, r = NSA_KV_GROUPS, NSA_HPG
    w_gate = w_in[layer, :, NSA_MAIN_DIM:].reshape(D_MODEL, g, 3 * r)
    w_gate = jnp.pad(w_gate, ((0, 0), (0, 0), (0, LANES - 3 * r))).reshape(D_MODEL, g * LANES)
    main, gl = _proj_call(x, mod, ng, w_in, w_gate, layer=layer, n=NSA_MAIN_DIM, mrow=3, grow=2)
    kc, vc, ks, vst, kw, vwt = _nsa_prep_call(main, cmp_pos, cmp_w1, cmp_w2, cos, sin, layer=layer)
    o = _nsa_attn_call(main, gl, cos_t, sin_t, kc, vc, ks, vst, kw, vwt, cover_t)
    return _oproj_call(o, w_out, x, mod, ng, layer=layer, mrow=5, grow=3)


def _mlstm_kernel(q_ref, k_ref, v_ref, og_ref, gi_ref, gf_ref, bi_ref, bf_ref, cwq_ref, cwk_ref, cbq_ref, cbk_ref,
                  gain_ref, o_ref, qpad_scr, kpad_scr, ct_scr, rowi_scr, chunk_scr, bcol_scr, ecol_scr):
    s, dk = q_ref.shape
    dv = v_ref.shape[1]
    chunk = ML_CHUNK
    nc = s // chunk
    pad = qpad_scr.shape[0] - s

    for src, dst in ((q_ref, qpad_scr), (k_ref, kpad_scr)):
        dst[:pad] = jnp.zeros((pad, dk), F32)
        dst[pad:] = src[...]

    def conv_silu(pad_ref, w_ref, b_ref, r0):
        win = pad_ref[pl.ds(r0, chunk + pad), :]
        acc = win[pad:] * w_ref[ML_CONV - 1:ML_CONV, :] + b_ref[...]
        for d in range(1, ML_CONV):
            acc = acc + pltpu.roll(win, d, 0)[pad:] * w_ref[ML_CONV - 1 - d:ML_CONV - d, :]
        return _silu(acc)

    ig = gi_ref[...] + bi_ref[...]
    fg = gf_ref[...] + bf_ref[...]
    logf = jnp.minimum(fg, 0.0) - jnp.log1p(jnp.exp(-jnp.abs(fg)))
    tri_r = lax.broadcasted_iota(jnp.int32, (chunk, chunk), 0)
    tri_c = lax.broadcasted_iota(jnp.int32, (chunk, chunk), 1)
    upper = jnp.where(tri_r <= tri_c, 1.0, 0.0)
    b = jnp.dot(logf, upper, precision=HIGHEST, preferred_element_type=F32)
    w_end = b[:, chunk - 1:chunk] - b + ig
    m_loc = jnp.max(w_end, axis=-1, keepdims=True)
    e_end = jnp.exp(w_end - m_loc)
    rowi_scr[...] = ig - b
    b_tot = b[:, chunk - 1:chunk]
    m_run = jnp.zeros((1, 1), F32)
    m_before, m_after = [], []
    for c in range(nc):
        m_before.append(m_run)
        m_run = jnp.maximum(b_tot[c:c + 1, :] + m_run, m_loc[c:c + 1, :])
        m_after.append(m_run)
    m_prev = jnp.concatenate(m_before, axis=0)
    m_next = jnp.concatenate(m_after, axis=0)
    for i, val in enumerate((m_prev, jnp.exp(b_tot + m_prev - m_next), jnp.exp(m_loc - m_next))):
        chunk_scr[i] = jnp.broadcast_to(val, (nc, LANES))
    eye = jnp.where(tri_r == tri_c, 1.0, 0.0)
    cols = lax.dot_general(eye, jnp.concatenate([b, e_end], axis=0), NT_DIMS, precision=HIGHEST,
                           preferred_element_type=F32)
    for c in range(nc):
        bcol_scr[c] = jnp.broadcast_to(cols[:, c:c + 1], (chunk, LANES))
        ecol_scr[c] = jnp.broadcast_to(cols[:, nc + c:nc + c + 1], (chunk, LANES))

    ct_scr[...] = jnp.zeros(ct_scr.shape, F32)
    lower = tri_c <= tri_r
    gain = gain_ref[...]

    n_state_tiles = ct_scr.shape[1] // LANES

    def body(c, carry):
        r0 = pl.multiple_of(c * chunk, chunk)
        q = conv_silu(qpad_scr, cwq_ref, cbq_ref, r0).astype(BF16)
        k = conv_silu(kpad_scr, cwk_ref, cbk_ref, r0) * (dk ** -0.5)
        va = jnp.concatenate([v_ref[pl.ds(r0, chunk), :].astype(BF16), jnp.ones((chunk, LANES), BF16)], axis=1)
        m_prev, decay, inject = (chunk_scr[i, pl.ds(c, 1), :] for i in range(3))
        bcol = bcol_scr[c]
        log_intra = jnp.where(lower, bcol[:, :chunk] + rowi_scr[pl.ds(c, 1), :], NEG_INF)
        log_inter = bcol[:, :1] + m_prev[:, :1]
        m_t = jnp.maximum(log_inter, jnp.max(log_intra, axis=-1, keepdims=True))
        e_inter = jnp.exp(log_inter - m_t)
        qk = _dot_nt(q, k.astype(BF16)) * jnp.exp(log_intra - m_t)
        ct = ct_scr[...]
        tot = _dot(qk.astype(BF16), va) + e_inter * _dot(q, ct.astype(BF16))
        h = tot[:, :dv] / jnp.maximum(jnp.abs(tot[:, dv:dv + 1]), jnp.exp(-m_t))
        hn = _rms(h) * gain
        o_ref[pl.ds(r0, chunk), :] = (jax.nn.sigmoid(og_ref[pl.ds(r0, chunk), :]) * hn).astype(o_ref.dtype)
        c_loc = lax.dot_general((k * ecol_scr[c]).astype(BF16), va, TN_DIMS, preferred_element_type=F32)
        ct_scr[...] = (jnp.concatenate([decay] * n_state_tiles, axis=1) * ct
                       + jnp.concatenate([inject] * n_state_tiles, axis=1) * c_loc)
        return carry

    lax.fori_loop(0, nc, body, 0, unroll=4)


def _mlstm_call(main, gates_t, gate_b, conv_w, conv_b, gain):
    bsz, s, _ = main.shape
    h, dk, dv = ML_HEADS, ML_QK_DIM, ML_V_DIM
    chunk = ML_CHUNK
    nc = s // chunk
    vblk0 = ML_QK_COLS // dv

    def gate(off):
        return pl.BlockSpec((None, None, nc, chunk), lambda b, hi: (b, off + hi, 0, 0))

    def bias(off):
        return pl.BlockSpec((None, 1, 1), lambda b, hi: (off + hi, 0, 0))

    return pl.pallas_call(
        _mlstm_kernel,
        out_shape=jax.ShapeDtypeStruct((bsz, s, h * dv), BF16),
        grid=(bsz, h),
        in_specs=[
            pl.BlockSpec((None, s, dk), lambda b, hi: (b, 0, hi)),
            pl.BlockSpec((None, s, dk), lambda b, hi: (b, 0, h + hi)),
            pl.BlockSpec((None, s, dv), lambda b, hi: (b, 0, vblk0 + hi)),
            pl.BlockSpec((None, s, dv), lambda b, hi: (b, 0, vblk0 + h + hi)),
            gate(0), gate(h), bias(0), bias(h),
            pl.BlockSpec((ML_CONV, dk), lambda b, hi: (0, hi)),
            pl.BlockSpec((ML_CONV, dk), lambda b, hi: (0, h + hi)),
            pl.BlockSpec((1, dk), lambda b, hi: (0, hi)),
            pl.BlockSpec((1, dk), lambda b, hi: (0, h + hi)),
            pl.BlockSpec((1, dv), lambda b, hi: (0, hi)),
        ],
        out_specs=pl.BlockSpec((None, s, dv), lambda b, hi: (b, 0, hi)),
        scratch_shapes=[
            pltpu.VMEM((s + SUBLANES, dk), F32),
            pltpu.VMEM((s + SUBLANES, dk), F32),
            pltpu.VMEM((dk, dv + LANES), F32),
            pltpu.VMEM((nc, chunk), F32),
            pltpu.VMEM((3, nc, LANES), F32),
            pltpu.VMEM((nc, chunk, LANES), F32),
            pltpu.VMEM((nc, chunk, LANES), F32),
        ],
        compiler_params=_cparams(("parallel", "parallel")),
        name="mlstm",
    )(main, main, main, main, gates_t, gates_t, gate_b, gate_b, conv_w, conv_w, conv_b, conv_b, gain)


def _mlstm_layer(x, mod, ng, w_in, conv_w, conv_b, gate_b, mh_gain, w_out, *, layer):
    bsz, s, _ = x.shape
    ngate = 2 * ML_HEADS
    w_gate = jnp.pad(w_in[layer, :, ML_MAIN_DIM:], ((0, 0), (0, LANES - ngate)))
    main, gl = _proj_call(x, mod, ng, w_in, w_gate, layer=layer, n=ML_MAIN_DIM, mrow=3, grow=2)
    gates_t = jnp.swapaxes(gl[:, :, :ngate], 1, 2).reshape(bsz, ngate, s // ML_CHUNK, ML_CHUNK)
    o = _mlstm_call(main, gates_t, gate_b.reshape(ngate, 1, 1), conv_w, conv_b.reshape(1, -1),
                    mh_gain.reshape(1, -1))
    return _oproj_call(o, w_out, x, mod, ng, layer=layer, mrow=5, grow=3)


def kernel(x, c, mod_w, mod_b, norm_g, ffn_pre_w_in, ffn_pre_w_out, ffn_post_w_in, ffn_post_w_out, nsa_w_in, nsa_cmp_pos, nsa_cmp_w1, nsa_cmp_w2, nsa_w_out, ml_w_in, ml_conv_w, ml_conv_b, ml_gate_b, ml_mh_gain, ml_w_out):
    bsz, s, d = x.shape
    depth = mod_w.shape[0]
    mods = _mod_call(c, mod_w, mod_b).reshape(depth, bsz, N_MOD, d)
    tables = _nsa_tables(s)
    pre_in, pre_out = ffn_pre_w_in.astype(BF16), ffn_pre_w_out.astype(BF16)
    post_in, post_out = ffn_post_w_in.astype(BF16), ffn_post_w_out.astype(BF16)
    nsa_in, nsa_out = nsa_w_in.astype(BF16), nsa_w_out.astype(BF16)
    nsa_w1, nsa_w2 = nsa_cmp_w1.astype(BF16), nsa_cmp_w2.astype(BF16)
    ml_in, ml_out = ml_w_in.astype(BF16), ml_w_out.astype(BF16)
    for i in range(depth):
        mod, ng = mods[i], norm_g[i]
        x = _ffn_call(x, mod, ng, pre_in, pre_out, layer=i, mrow=0, grow=0)
        j = i // 2
        if i % 2 == 0:
            x = _nsa_layer(x, mod, ng, nsa_in, nsa_cmp_pos, nsa_w1, nsa_w2, nsa_out, tables, layer=j)
        else:
            x = _mlstm_layer(x, mod, ng, ml_in, ml_conv_w[j], ml_conv_b[j], ml_gate_b[j], ml_mh_gain[j], ml_out,
                             layer=j)
        x = _ffn_call(x, mod, ng, post_in, post_out, layer=i, mrow=6, grow=4)
    return x
```

```python
import functools

import numpy as np
import jax
import jax.numpy as jnp
from jax import lax
from jax.experimental import pallas as pl
from jax.experimental.pallas import tpu as pltpu

F32 = jnp.float32
BF16 = jnp.bfloat16
HIGHEST = lax.Precision.HIGHEST

D_MODEL = 2048
DEPTH = 4
D_FF = 5632
FFN_RES = 0.5
N_MOD = 9
RMS_EPS = 1e-6
NEG_INF = -1e30
LOG2_E = 1.4426950408889634

NSA_HEADS = 16
NSA_HEAD_DIM = 128
NSA_KV_GROUPS = 4
NSA_HPG = NSA_HEADS // NSA_KV_GROUPS
NSA_Q_DIM = NSA_HEADS * NSA_HEAD_DIM
NSA_KV_DIM = NSA_KV_GROUPS * NSA_HEAD_DIM
NSA_MAIN_DIM = NSA_Q_DIM + 6 * NSA_KV_DIM
CMP_BLOCK = 32
CMP_STRIDE = 16
CMP_HIDDEN = 512
SEL_BLOCK = 64
SEL_TOPK = 16
N_LOCAL_BLOCKS = 2
FORCE_SCORE = 1e9
WINDOW = 512
ROPE_THETA = 10000.0

ML_HEADS = 8
ML_QK_DIM = 128
ML_V_DIM = 256
ML_CONV = 4
ML_CHUNK = 64
ML_QK_COLS = 2 * ML_HEADS * ML_QK_DIM
ML_MAIN_DIM = ML_QK_COLS + 2 * ML_HEADS * ML_V_DIM

LANES = 128
SUBLANES = 8
VMEM_LIMIT = 56 * 1024 * 1024

NT_DIMS = (((1,), (1,)), ((), ()))
TN_DIMS = (((0,), (0,)), ((), ()))


def _cparams(sem):
    return pltpu.CompilerParams(dimension_semantics=sem, vmem_limit_bytes=VMEM_LIMIT)


def _silu(x):
    return x * jax.nn.sigmoid(x)


def _rms(x):
    return x * lax.rsqrt(jnp.mean(x * x, axis=-1, keepdims=True) + RMS_EPS)


def _dot(a, b):
    return jnp.dot(a, b, preferred_element_type=F32)


def _dot_nt(a, b):
    return lax.dot_general(a, b, NT_DIMS, preferred_element_type=F32)


def _mod_kernel(c_ref, w_ref, b_ref, o_ref):
    ca = _silu(c_ref[...]).astype(BF16)
    o_ref[...] = _dot(ca, w_ref[...].astype(BF16)) + b_ref[...]


def _mod_call(c, mod_w, mod_b):
    depth, d, n = mod_w.shape
    bsz = c.shape[0]
    tn = 1024
    return pl.pallas_call(
        _mod_kernel,
        out_shape=jax.ShapeDtypeStruct((depth, bsz, n), F32),
        grid=(depth, n // tn),
        in_specs=[
            pl.BlockSpec((bsz, d), lambda i, j: (0, 0)),
            pl.BlockSpec((None, d, tn), lambda i, j: (i, 0, j)),
            pl.BlockSpec((None, 1, tn), lambda i, j: (i, 0, j)),
        ],
        out_specs=pl.BlockSpec((None, bsz, tn), lambda i, j: (i, 0, j)),
        compiler_params=_cparams(("parallel", "parallel")),
        name="mod",
    )(c, mod_w, mod_b.reshape(depth, 1, n))


def _prenorm(x, mod_ref, ng_ref, mrow, grow):
    gain = ng_ref[grow:grow + 1, :] * (1.0 + mod_ref[mrow + 1:mrow + 2, :])
    return _rms(x) * gain + mod_ref[mrow:mrow + 1, :]


ROW_CHUNK = 16
ROW_UNROLL = 4


def _row_loop(n_rows, body):
    def step(i, carry):
        body(pl.ds(pl.multiple_of(i * ROW_CHUNK, ROW_CHUNK), ROW_CHUNK))
        return carry
    lax.fori_loop(0, n_rows // ROW_CHUNK, step, 0, unroll=ROW_UNROLL)


def _ffn_kernel(x_ref, mod_ref, ng_ref, wg_ref, wu_ref, wo_ref, o_ref, xn_ref, acc_ref, *, mrow, grow):
    j = pl.program_id(2)

    @pl.when(j == 0)
    def _():
        gain = ng_ref[grow:grow + 1, :] * (1.0 + mod_ref[mrow + 1:mrow + 2, :])
        shift = mod_ref[mrow:mrow + 1, :]

        def prenorm_rows(rows):
            xn_ref[rows, :] = (_rms(x_ref[rows, :]) * gain + shift).astype(BF16)

        _row_loop(x_ref.shape[0], prenorm_rows)
        acc_ref[...] = jnp.zeros_like(acc_ref)

    xn = xn_ref[...]
    g = _dot(xn, wg_ref[...])
    u = _dot(xn, wu_ref[...])
    acc_ref[...] += _dot((_silu(g) * u).astype(BF16), wo_ref[...])

    @pl.when(j == pl.num_programs(2) - 1)
    def _():
        gain = ng_ref[grow + 1:grow + 2, :] * (FFN_RES * mod_ref[mrow + 2:mrow + 3, :])

        def finish_rows(rows):
            o_ref[rows, :] = x_ref[rows, :] + _rms(acc_ref[rows, :]) * gain

        _row_loop(x_ref.shape[0], finish_rows)


def _ffn_call(x, mod, ng, w_in, w_out, *, layer, mrow, grow):
    bsz, s, d = x.shape
    dff = w_out.shape[1]
    tm, tf = 512, 512
    nf = dff // tf
    return pl.pallas_call(
        functools.partial(_ffn_kernel, mrow=mrow, grow=grow),
        out_shape=jax.ShapeDtypeStruct(x.shape, F32),
        grid=(bsz, s // tm, nf),
        in_specs=[
            pl.BlockSpec((None, tm, d), lambda b, m, j: (b, m, 0)),
            pl.BlockSpec((None, N_MOD, d), lambda b, m, j: (b, 0, 0)),
            pl.BlockSpec(ng.shape, lambda b, m, j: (0, 0)),
            pl.BlockSpec((None, d, tf), lambda b, m, j: (layer, 0, j)),
            pl.BlockSpec((None, d, tf), lambda b, m, j: (layer, 0, nf + j)),
            pl.BlockSpec((None, tf, d), lambda b, m, j: (layer, j, 0)),
        ],
        out_specs=pl.BlockSpec((None, tm, d), lambda b, m, j: (b, m, 0)),
        scratch_shapes=[pltpu.VMEM((tm, d), BF16), pltpu.VMEM((tm, d), F32)],
        compiler_params=_cparams(("parallel", "parallel", "arbitrary")),
        name="ffn",
    )(x, mod, ng, w_in, w_in, w_out)


def _proj_kernel(x_ref, mod_ref, ng_ref, w_ref, wgate_ref, o_ref, og_ref, xn_ref, *, mrow, grow):
    @pl.when(pl.program_id(2) == 0)
    def _():
        xn = _prenorm(x_ref[...], mod_ref, ng_ref, mrow, grow).astype(BF16)
        xn_ref[...] = xn
        og_ref[...] = _dot(xn, wgate_ref[...])

    o_ref[...] = _dot(xn_ref[...], w_ref[...])


def _proj_call(x, mod, ng, w_in, w_gate, *, layer, n, mrow, grow):
    bsz, s, d = x.shape
    ngate = w_gate.shape[1]
    tm, tn = 1024, 1024
    return pl.pallas_call(
        functools.partial(_proj_kernel, mrow=mrow, grow=grow),
        out_shape=(jax.ShapeDtypeStruct((bsz, s, n), F32), jax.ShapeDtypeStruct((bsz, s, ngate), F32)),
        grid=(bsz, s // tm, n // tn),
        in_specs=[
            pl.BlockSpec((None, tm, d), lambda b, m, j: (b, m, 0)),
            pl.BlockSpec((None, N_MOD, d), lambda b, m, j: (b, 0, 0)),
            pl.BlockSpec(ng.shape, lambda b, m, j: (0, 0)),
            pl.BlockSpec((None, d, tn), lambda b, m, j: (layer, 0, j)),
            pl.BlockSpec((d, ngate), lambda b, m, j: (0, 0)),
        ],
        out_specs=(pl.BlockSpec((None, tm, tn), lambda b, m, j: (b, m, j)),
                   pl.BlockSpec((None, tm, ngate), lambda b, m, j: (b, m, 0))),
        scratch_shapes=[pltpu.VMEM((tm, d), BF16)],
        compiler_params=_cparams(("parallel", "parallel", "arbitrary")),
        name="proj",
    )(x, mod, ng, w_in, w_gate)


def _oproj_kernel(a_ref, w_ref, x_ref, mod_ref, ng_ref, o_ref, *, mrow, grow):
    yn = _rms(_dot(a_ref[...], w_ref[...])) * ng_ref[grow:grow + 1, :]
    o_ref[...] = x_ref[...] + mod_ref[mrow:mrow + 1, :] * yn


def _oproj_call(a, w, x, mod, ng, *, layer, mrow, grow):
    bsz, s, d = x.shape
    k = a.shape[-1]
    tm = 512
    return pl.pallas_call(
        functools.partial(_oproj_kernel, mrow=mrow, grow=grow),
        out_shape=jax.ShapeDtypeStruct(x.shape, F32),
        grid=(bsz, s // tm),
        in_specs=[
            pl.BlockSpec((None, tm, k), lambda b, m: (b, m, 0)),
            pl.BlockSpec((None, k, d), lambda b, m: (layer, 0, 0)),
            pl.BlockSpec((None, tm, d), lambda b, m: (b, m, 0)),
            pl.BlockSpec((None, N_MOD, d), lambda b, m: (b, 0, 0)),
            pl.BlockSpec(ng.shape, lambda b, m: (0, 0)),
        ],
        out_specs=pl.BlockSpec((None, tm, d), lambda b, m: (b, m, 0)),
        compiler_params=_cparams(("parallel", "parallel")),
        name="oproj",
    )(a, w, x, mod, ng)


def _rope(x, cos, sin_signed):
    return x * cos + pltpu.roll(x, NSA_HEAD_DIM // 2, 1) * sin_signed


def _nsa_prep_kernel(kc_ref, vc_ref, ks_ref, vs_ref, kw_ref, vw_ref, pe_ref, w1_ref, w2_ref, cos_ref, sin_ref,
                     kco_ref, vco_ref, kso_ref, vso_ref, kwo_ref, vwo_ref):
    s = kc_ref.shape[0]
    nrow = s // CMP_STRIDE
    half = CMP_BLOCK // CMP_STRIDE

    def compress(t_ref, idx):
        streams = [t_ref[pl.ds(j, nrow, stride=CMP_STRIDE), :] for j in range(CMP_STRIDE)]
        hidden = None
        for h in range(half):
            flat = jnp.concatenate(
                [(streams[j] + pe_ref[idx, h * CMP_STRIDE + j:h * CMP_STRIDE + j + 1, :]).astype(BF16)
                 for j in range(CMP_STRIDE)], axis=1)
            w = w1_ref[idx, h * CMP_STRIDE * NSA_HEAD_DIM:(h + 1) * CMP_STRIDE * NSA_HEAD_DIM, :]
            part = _dot(flat, w)
            if h:
                part = pltpu.roll(part, nrow - h, 0)
            hidden = part if hidden is None else hidden + part
        return _dot(_silu(hidden).astype(BF16), w2_ref[idx])

    kco_ref[...] = compress(kc_ref, 0).astype(kco_ref.dtype)
    vco_ref[...] = compress(vc_ref, 1).astype(vco_ref.dtype)
    cos = cos_ref[...]
    sin = sin_ref[...]
    kso_ref[...] = _rope(ks_ref[...], cos, sin).astype(BF16)
    kwo_ref[...] = _rope(kw_ref[...], cos, sin).astype(BF16)
    vso_ref[...] = vs_ref[...].T.astype(BF16)
    vwo_ref[...] = vw_ref[...].T.astype(BF16)


def _nsa_prep_call(main, pe, w1, w2, cos, sin, *, layer):
    bsz, s, _ = main.shape
    g, dh = NSA_KV_GROUPS, NSA_HEAD_DIM
    nrow = s // CMP_STRIDE
    q_blocks = NSA_Q_DIM // dh

    def col(k):
        return pl.BlockSpec((None, s, dh), lambda b, gi, k=k: (b, 0, q_blocks + k * g + gi))

    def full(a):
        return pl.BlockSpec(a.shape, lambda b, gi, nd=a.ndim: (0,) * nd)

    def stacked(a):
        return pl.BlockSpec((None,) + a.shape[1:], lambda b, gi, nd=a.ndim: (layer,) + (0,) * (nd - 1))

    def out(rows, cols):
        return (jax.ShapeDtypeStruct((bsz, g, rows, cols), BF16),
                pl.BlockSpec((None, None, rows, cols), lambda b, gi: (b, gi, 0, 0)))

    outs = [out(nrow, dh), out(nrow, dh), out(s, dh), out(dh, s), out(s, dh), out(dh, s)]
    return pl.pallas_call(
        _nsa_prep_kernel,
        out_shape=tuple(o[0] for o in outs),
        grid=(bsz, g),
        in_specs=[col(0), col(1), col(2), col(3), col(4), col(5), stacked(pe), stacked(w1), stacked(w2),
                  full(cos), full(sin)],
        out_specs=tuple(o[1] for o in outs),
        compiler_params=_cparams(("parallel", "parallel")),
        name="nsa_prep",
    )(main, main, main, main, main, main, pe, w1, w2, cos, sin)


def _nsa_attn_kernel(q_ref, gl_ref, cos_ref, sin_ref, kc_ref, vc_ref, ks_ref, vst_ref, kw_ref, vwt_ref,
                     cover_ref, o_ref, qrot_scr, selb_scr, m_scr, l_scr, alpha_scr, acc_scr, out_scr, s_scr, p_scr,
                     *, tq, tk, n_sel):
    qi = pl.program_id(2)
    dh = NSA_HEAD_DIM
    ng = kc_ref.shape[0]
    heads = [(g, r) for g in range(ng) for r in range(NSA_HPG)]
    scale = dh ** -0.5 * LOG2_E
    pos_q = qi * tq + lax.broadcasted_iota(jnp.int32, (1, tq), 1)
    cos = cos_ref[...]
    sin = sin_ref[...]
    gl_t = gl_ref[...].T
    gates = [jax.nn.sigmoid(gl_t[g * LANES:g * LANES + 4 * NSA_HPG, :]) for g in range(ng)]

    def gate(g, r, branch):
        return gates[g][3 * r + branch:3 * r + branch + 1, :]

    crow = lax.broadcasted_iota(jnp.int32, (LANES, 1), 0)
    cmask = crow * CMP_STRIDE + (CMP_BLOCK - 1) <= pos_q
    jrow = lax.broadcasted_iota(jnp.int32, (n_sel, 1), 0)
    q_blk = pos_q // SEL_BLOCK
    causal = jrow <= q_blk
    forced = (jrow == 0) | (causal & (jrow > q_blk - N_LOCAL_BLOCKS))
    top_k = min(SEL_TOPK, n_sel)
    rank_unroll = 4
    n_causal = (qi * tq + tq) // SEL_BLOCK
    n_rank_steps = jnp.where(n_causal <= top_k, 0, n_causal // rank_unroll)
    for h in range(len(heads)):
        acc_scr[h] = q_ref[:, h * dh:(h + 1) * dh].T * scale
    for h, (g, _) in enumerate(heads):
        s_scr[h, :LANES, :] = _dot(kc_ref[g], acc_scr[h].astype(BF16))
    for h in range(len(heads)):
        qt = acc_scr[h]
        rot = jnp.concatenate([qt[dh // 2:], qt[:dh // 2]], axis=0)
        qrot_scr[h] = (qt * cos + rot * sin).astype(BF16)
    for g in range(ng):
        p_sum = jnp.zeros((LANES, tq), F32)
        for h in range(g * NSA_HPG, (g + 1) * NSA_HPG):
            sc = jnp.where(cmask, s_scr[h, :LANES, :], NEG_INF)
            e = jnp.exp2(sc - jnp.max(sc, axis=0, keepdims=True))
            p = jnp.where(cmask, e * (1.0 / jnp.sum(e, axis=0, keepdims=True)), 0.0)
            p_sum = p_sum + p
            p_scr[0, h, :LANES, :] = p.astype(BF16)
        vct = vc_ref[g].astype(F32).T.astype(BF16)
        for r in range(NSA_HPG):
            h = g * NSA_HPG + r
            out_scr[h] = gate(g, r, 0) * _dot(vct, p_scr[0, h, :LANES, :])

        imp = jnp.dot(cover_ref[...], p_sum, precision=HIGHEST, preferred_element_type=F32)[:n_sel]
        imp = jnp.where(forced, FORCE_SCORE, jnp.where(causal, imp, -1.0))
        selb_scr[g] = imp

        def rank_body(i, rank, g=g, imp=imp):
            for k in range(rank_unroll):
                jp = i * rank_unroll + k
                other = selb_scr[g, pl.ds(jp, 1), :]
                tie = jnp.where(jrow > jp, 1.0, 0.0)
                rank = rank + jnp.where(other > imp, 1.0, jnp.where(other == imp, tie, 0.0))
            return rank

        rank = lax.fori_loop(0, n_rank_steps, rank_body, jnp.zeros((n_sel, tq), F32))
        keep = (rank < float(top_k)) & (imp >= 0.0)
        selb_scr[g] = jnp.where(keep, 0.0, NEG_INF)

    def reset(first_kt):
        m_scr[...] = jnp.full(m_scr.shape, NEG_INF, F32)
        l_scr[...] = jnp.zeros(l_scr.shape, F32)
        acc_scr[...] = jnp.zeros(acc_scr.shape, F32)
        alpha_scr[...] = jnp.ones(alpha_scr.shape, F32)
        p_scr[(first_kt - 1) & 1] = jnp.zeros(p_scr.shape[1:], BF16)

    def tile0(kt):
        return pl.multiple_of(jnp.maximum(kt, 0) * tk, tk)

    def apply_pending(vt_ref, kt):
        slot = kt & 1
        for g in range(ng):
            vt = vt_ref[g, :, pl.ds(tile0(kt), tk)]
            for h in range(g * NSA_HPG, (g + 1) * NSA_HPG):
                acc_scr[h] = alpha_scr[slot, h] * acc_scr[h] + _dot(vt, p_scr[slot, h])

    def flash_step(k_ref, vt_ref, kt, bias):
        slot = kt & 1
        for g in range(ng):
            k_tile = k_ref[g, pl.ds(tile0(kt), tk), :]
            for h in range(g * NSA_HPG, (g + 1) * NSA_HPG):
                s_scr[h] = _dot(k_tile, qrot_scr[h])
        apply_pending(vt_ref, kt - 1)
        for h, (g, _) in enumerate(heads):
            sc = s_scr[h]
            head_bias = bias[g] if isinstance(bias, list) else bias
            if head_bias is not None:
                sc = sc + head_bias
            m_old = m_scr[h]
            m_new = jnp.maximum(m_old, jnp.max(sc, axis=0, keepdims=True))
            alpha = jnp.exp2(m_old - m_new)
            p = jnp.exp2(sc - m_new)
            l_scr[h] = alpha * l_scr[h] + jnp.sum(p, axis=0, keepdims=True)
            alpha_scr[slot, h] = alpha
            p_scr[slot, h] = p.astype(BF16)
            m_scr[h] = m_new

    def finish(branch):
        for h, (g, r) in enumerate(heads):
            w = gate(g, r, branch) / l_scr[h]
            out_scr[h] = out_scr[h] + w * acc_scr[h]

    def sel_bias(kt, extra=None):
        per_tile = tk // SEL_BLOCK
        biases = []
        for g in range(ng):
            rows = [selb_scr[g, pl.ds(kt * per_tile + j, 1), :] for j in range(per_tile)]
            b = jnp.concatenate([jnp.broadcast_to(row, (SEL_BLOCK, tq)) for row in rows], axis=0)
            biases.append(b if extra is None else b + extra)
        return biases

    def key_pos(kt):
        return kt * tk + lax.broadcasted_iota(jnp.int32, (tk, 1), 0)

    causal_bias = jnp.where(key_pos(qi) <= pos_q, 0.0, NEG_INF)

    reset(0)

    def sel_body(kt, carry):
        flash_step(ks_ref, vst_ref, kt, sel_bias(kt))
        return carry

    lax.fori_loop(0, qi, sel_body, 0)
    flash_step(ks_ref, vst_ref, qi, sel_bias(qi, causal_bias))
    apply_pending(vst_ref, qi)
    finish(1)

    n_back = WINDOW // tk
    reset(jnp.maximum(qi - n_back, 0))
    for back in range(n_back, 0, -1):
        @pl.when(qi >= back)
        def _(back=back):
            bias = jnp.where(pos_q - key_pos(qi - back) < WINDOW, 0.0, NEG_INF) if back == n_back else None
            flash_step(kw_ref, vwt_ref, qi - back, bias)
    flash_step(kw_ref, vwt_ref, qi, causal_bias)
    apply_pending(vwt_ref, qi)
    finish(2)

    for h in range(len(heads)):
        o_ref[:, h * dh:(h + 1) * dh] = out_scr[h].T.astype(o_ref.dtype)


def _nsa_attn_call(main, gl, cos_t, sin_t, kc, vc, ks, vst, kw, vwt, cover_t):
    bsz, s, _ = main.shape
    g, r, dh = NSA_KV_GROUPS, NSA_HPG, NSA_HEAD_DIM
    tq = tk = 256
    ng = 1
    nh = ng * r
    assert WINDOW % tk == 0 and tk % SEL_BLOCK == 0 and s % tq == 0 and g % ng == 0
    nrow = kc.shape[2]
    n_sel = s // SEL_BLOCK

    def kv(rows, cols):
        return pl.BlockSpec((None, ng, rows, cols), lambda b, gi, qi: (b, gi, 0, 0))

    return pl.pallas_call(
        functools.partial(_nsa_attn_kernel, tq=tq, tk=tk, n_sel=n_sel),
        out_shape=jax.ShapeDtypeStruct((bsz, s, NSA_Q_DIM), BF16),
        grid=(bsz, g // ng, s // tq),
        in_specs=[
            pl.BlockSpec((None, tq, nh * dh), lambda b, gi, qi: (b, qi, gi)),
            pl.BlockSpec((None, tq, ng * LANES), lambda b, gi, qi: (b, qi, gi)),
            pl.BlockSpec((dh, tq), lambda b, gi, qi: (0, qi)),
            pl.BlockSpec((dh, tq), lambda b, gi, qi: (0, qi)),
            kv(nrow, dh), kv(nrow, dh), kv(s, dh), kv(dh, s), kv(s, dh), kv(dh, s),
            pl.BlockSpec(cover_t.shape, lambda b, gi, qi: (0, 0)),
        ],
        out_specs=pl.BlockSpec((None, tq, nh * dh), lambda b, gi, qi: (b, qi, gi)),
        scratch_shapes=[
            pltpu.VMEM((nh, dh, tq), BF16),
            pltpu.VMEM((ng, n_sel, tq), F32),
            pltpu.VMEM((nh, 1, tq), F32),
            pltpu.VMEM((nh, 1, tq), F32),
            pltpu.VMEM((2, nh, 1, tq), F32),
            pltpu.VMEM((nh, dh, tq), F32),
            pltpu.VMEM((nh, dh, tq), F32),
            pltpu.VMEM((nh, tk, tq), F32),
            pltpu.VMEM((2, nh, tk, tq), BF16),
        ],
        compiler_params=_cparams(("parallel", "parallel", "arbitrary")),
        name="nsa_attn",
    )(main, gl, cos_t, sin_t, kc, vc, ks, vst, kw, vwt, cover_t)


def _nsa_tables(s):
    half = NSA_HEAD_DIM // 2
    freqs = ROPE_THETA ** (-jnp.arange(half, dtype=F32) / half)
    ang = jnp.arange(s).astype(F32)[:, None] * freqs[None, :]
    cos, sin = jnp.cos(ang), jnp.sin(ang)
    cos_full = jnp.concatenate([cos, cos], axis=-1)
    sin_signed = jnp.concatenate([-sin, sin], axis=-1)
    n_cmp = (s - CMP_BLOCK) // CMP_STRIDE + 1
    cs = np.arange(LANES)[None, :] * CMP_STRIDE
    js = np.arange(LANES)[:, None]
    cover_t = ((cs < (js + 1) * SEL_BLOCK) & (cs + CMP_BLOCK > js * SEL_BLOCK)
               & (np.arange(LANES)[None, :] < n_cmp) & (js < s // SEL_BLOCK)).astype(np.float32)
    return cos_full, sin_signed, cos_full.T, sin_signed.T, jnp.asarray(cover_t, F32)


def _nsa_layer(x, mod, ng, w_in, cmp_pos, cmp_w1, cmp_w2, w_out, tables, *, layer):
    cos, sin, cos_t, sin_t, cover_t = tables
    g, r = NSA_KV_GROUPS, NSA_HPG
    w_gate = w_in[layer, :, NSA_MAIN_DIM:].reshape(D_MODEL, g, 3 * r)
    w_gate = jnp.pad(w_gate, ((0, 0), (0, 0), (0, LANES - 3 * r))).reshape(D_MODEL, g * LANES)
    main, gl = _proj_call(x, mod, ng, w_in, w_gate, layer=layer, n=NSA_MAIN_DIM, mrow=3, grow=2)
    kc, vc, ks, vst, kw, vwt = _nsa_prep_call(main, cmp_pos, cmp_w1, cmp_w2, cos, sin, layer=layer)
    o = _nsa_attn_call(main, gl, cos_t, sin_t, kc, vc, ks, vst, kw, vwt, cover_t)
    return _oproj_call(o, w_out, x, mod, ng, layer=layer, mrow=5, grow=3)


def _mlstm_kernel(q_ref, k_ref, v_ref, og_ref, gi_ref, gf_ref, bi_ref, bf_ref, cwq_ref, cwk_ref, cbq_ref, cbk_ref,
                  gain_ref, o_ref, qpad_scr, kpad_scr, ct_scr, rowi_scr, chunk_scr, bcol_scr, ecol_scr):
    s, dk = q_ref.shape
    dv = v_ref.shape[1]
    chunk = ML_CHUNK
    nc = s // chunk
    pad = qpad_scr.shape[0] - s

    for src, dst in ((q_ref, qpad_scr), (k_ref, kpad_scr)):
        dst[:pad] = jnp.zeros((pad, dk), F32)
        dst[pad:] = src[...]

    def conv_silu(pad_ref, w_ref, b_ref, r0):
        win = pad_ref[pl.ds(r0, chunk + pad), :]
        acc = win[pad:] * w_ref[ML_CONV - 1:ML_CONV, :] + b_ref[...]
        for d in range(1, ML_CONV):
            acc = acc + pltpu.roll(win, d, 0)[pad:] * w_ref[ML_CONV - 1 - d:ML_CONV - d, :]
        return _silu(acc)

    ig = gi_ref[...] + bi_ref[...]
    fg = gf_ref[...] + bf_ref[...]
    logf = jnp.minimum(fg, 0.0) - jnp.log1p(jnp.exp(-jnp.abs(fg)))
    tri_r = lax.broadcasted_iota(jnp.int32, (chunk, chunk), 0)
    tri_c = lax.broadcasted_iota(jnp.int32, (chunk, chunk), 1)
    upper = jnp.where(tri_r <= tri_c, 1.0, 0.0)
    b = jnp.dot(logf, upper, precision=HIGHEST, preferred_element_type=F32)
    w_end = b[:, chunk - 1:chunk] - b + ig
    m_loc = jnp.max(w_end, axis=-1, keepdims=True)
    e_end = jnp.exp(w_end - m_loc)
    rowi_scr[...] = ig - b
    b_tot = b[:, chunk - 1:chunk]
    m_run = jnp.zeros((1, 1), F32)
    m_before, m_after = [], []
    for c in range(nc):
        m_before.append(m_run)
        m_run = jnp.maximum(b_tot[c:c + 1, :] + m_run, m_loc[c:c + 1, :])
        m_after.append(m_run)
    m_prev = jnp.concatenate(m_before, axis=0)
    m_next = jnp.concatenate(m_after, axis=0)
    for i, val in enumerate((m_prev, jnp.exp(b_tot + m_prev - m_next), jnp.exp(m_loc - m_next))):
        chunk_scr[i] = jnp.broadcast_to(val, (nc, LANES))
    eye = jnp.where(tri_r == tri_c, 1.0, 0.0)
    cols = lax.dot_general(eye, jnp.concatenate([b, e_end], axis=0), NT_DIMS, precision=HIGHEST,
                           preferred_element_type=F32)
    for c in range(nc):
        bcol_scr[c] = jnp.broadcast_to(cols[:, c:c + 1], (chunk, LANES))
        ecol_scr[c] = jnp.broadcast_to(cols[:, nc + c:nc + c + 1], (chunk, LANES))

    ct_scr[...] = jnp.zeros(ct_scr.shape, F32)
    lower = tri_c <= tri_r
    gain = gain_ref[...]

    n_state_tiles = ct_scr.shape[1] // LANES

    def body(c, carry):
        r0 = pl.multiple_of(c * chunk, chunk)
        q = conv_silu(qpad_scr, cwq_ref, cbq_ref, r0).astype(BF16)
        k = conv_silu(kpad_scr, cwk_ref, cbk_ref, r0) * (dk ** -0.5)
        va = jnp.concatenate([v_ref[pl.ds(r0, chunk), :].astype(BF16), jnp.ones((chunk, LANES), BF16)], axis=1)
        m_prev, decay, inject = (chunk_scr[i, pl.ds(c, 1), :] for i in range(3))
        bcol = bcol_scr[c]
        log_intra = jnp.where(lower, bcol[:, :chunk] + rowi_scr[pl.ds(c, 1), :], NEG_INF)
        log_inter = bcol[:, :1] + m_prev[:, :1]
        m_t = jnp.maximum(log_inter, jnp.max(log_intra, axis=-1, keepdims=True))
        e_inter = jnp.exp(log_inter - m_t)
        qk = _dot_nt(q, k.astype(BF16)) * jnp.exp(log_intra - m_t)
        ct = ct_scr[...]
        tot = _dot(qk.astype(BF16), va) + e_inter * _dot(q, ct.astype(BF16))
        h = tot[:, :dv] / jnp.maximum(jnp.abs(tot[:, dv:dv + 1]), jnp.exp(-m_t))
        hn = _rms(h) * gain
        o_ref[pl.ds(r0, chunk), :] = (jax.nn.sigmoid(og_ref[pl.ds(r0, chunk), :]) * hn).astype(o_ref.dtype)
        c_loc = lax.dot_general((k * ecol_scr[c]).astype(BF16), va, TN_DIMS, preferred_element_type=F32)
        ct_scr[...] = (jnp.concatenate([decay] * n_state_tiles, axis=1) * ct
                       + jnp.concatenate([inject] * n_state_tiles, axis=1) * c_loc)
        return carry

    lax.fori_loop(0, nc, body, 0, unroll=4)


def _mlstm_call(main, gates_t, gate_b, conv_w, conv_b, gain):
    bsz, s, _ = main.shape
    h, dk, dv = ML_HEADS, ML_QK_DIM, ML_V_DIM
    chunk = ML_CHUNK
    nc = s // chunk
    vblk0 = ML_QK_COLS // dv

    def gate(off):
        return pl.BlockSpec((None, None, nc, chunk), lambda b, hi: (b, off + hi, 0, 0))

    def bias(off):
        return pl.BlockSpec((None, 1, 1), lambda b, hi: (off + hi, 0, 0))

    return pl.pallas_call(
        _mlstm_kernel,
        out_shape=jax.ShapeDtypeStruct((bsz, s, h * dv), BF16),
        grid=(bsz, h),
        in_specs=[
            pl.BlockSpec((None, s, dk), lambda b, hi: (b, 0, hi)),
            pl.BlockSpec((None, s, dk), lambda b, hi: (b, 0, h + hi)),
            pl.BlockSpec((None, s, dv), lambda b, hi: (b, 0, vblk0 + hi)),
            pl.BlockSpec((None, s, dv), lambda b, hi: (b, 0, vblk0 + h + hi)),
            gate(0), gate(h), bias(0), bias(h),
            pl.BlockSpec((ML_CONV, dk), lambda b, hi: (0, hi)),
            pl.BlockSpec((ML_CONV, dk), lambda b, hi: (0, h + hi)),
            pl.BlockSpec((1, dk), lambda b, hi: (0, hi)),
            pl.BlockSpec((1, dk), lambda b, hi: (0, h + hi)),
            pl.BlockSpec((1, dv), lambda b, hi: (0, hi)),
        ],
        out_specs=pl.BlockSpec((None, s, dv), lambda b, hi: (b, 0, hi)),
        scratch_shapes=[
            pltpu.VMEM((s + SUBLANES, dk), F32),
            pltpu.VMEM((s + SUBLANES, dk), F32),
            pltpu.VMEM((dk, dv + LANES), F32),
            pltpu.VMEM((nc, chunk), F32),
            pltpu.VMEM((3, nc, LANES), F32),
            pltpu.VMEM((nc, chunk, LANES), F32),
            pltpu.VMEM((nc, chunk, LANES), F32),
        ],
        compiler_params=_cparams(("parallel", "parallel")),
        name="mlstm",
    )(main, main, main, main, gates_t, gates_t, gate_b, gate_b, conv_w, conv_w, conv_b, conv_b, gain)


def _mlstm_layer(x, mod, ng, w_in, conv_w, conv_b, gate_b, mh_gain, w_out, *, layer):
    bsz, s, _ = x.shape
    ngate = 2 * ML_HEADS
    w_gate = jnp.pad(w_in[layer, :, ML_MAIN_DIM:], ((0, 0), (0, LANES - ngate)))
    main, gl = _proj_call(x, mod, ng, w_in, w_gate, layer=layer, n=ML_MAIN_DIM, mrow=3, grow=2)
    gates_t = jnp.swapaxes(gl[:, :, :ngate], 1, 2).reshape(bsz, ngate, s // ML_CHUNK, ML_CHUNK)
    o = _mlstm_call(main, gates_t, gate_b.reshape(ngate, 1, 1), conv_w, conv_b.reshape(1, -1),
                    mh_gain.reshape(1, -1))
    return _oproj_call(o, w_out, x, mod, ng, layer=layer, mrow=5, grow=3)


def kernel(x, c, mod_w, mod_b, norm_g, ffn_pre_w_in, ffn_pre_w_out, ffn_post_w_in, ffn_post_w_out, nsa_w_in, nsa_cmp_pos, nsa_cmp_w1, nsa_cmp_w2, nsa_w_out, ml_w_in, ml_conv_w, ml_conv_b, ml_gate_b, ml_mh_gain, ml_w_out):
    bsz, s, d = x.shape
    depth = mod_w.shape[0]
    mods = _mod_call(c, mod_w, mod_b).reshape(depth, bsz, N_MOD, d)
    tables = _nsa_tables(s)
    pre_in, pre_out = ffn_pre_w_in.astype(BF16), ffn_pre_w_out.astype(BF16)
    post_in, post_out = ffn_post_w_in.astype(BF16), ffn_post_w_out.astype(BF16)
    nsa_in, nsa_out = nsa_w_in.astype(BF16), nsa_w_out.astype(BF16)
    nsa_w1, nsa_w2 = nsa_cmp_w1.astype(BF16), nsa_cmp_w2.astype(BF16)
    ml_in, ml_out = ml_w_in.astype(BF16), ml_w_out.astype(BF16)
    for i in range(depth):
        mod, ng = mods[i], norm_g[i]
        x = _ffn_call(x, mod, ng, pre_in, pre_out, layer=i, mrow=0, grow=0)
        j = i // 2
        if i % 2 == 0:
            x = _nsa_layer(x, mod, ng, nsa_in, nsa_cmp_pos, nsa_w1, nsa_w2, nsa_out, tables, layer=j)
        else:
            x = _mlstm_layer(x, mod, ng, ml_in, ml_conv_w[j], ml_conv_b[j], ml_gate_b[j], ml_mh_gain[j], ml_out,
                             layer=j)
        x = _ffn_call(x, mod, ng, post_in, post_out, layer=i, mrow=6, grow=4)
    return x
```

```python
import functools

import numpy as np
import jax
import jax.numpy as jnp
from jax import lax
from jax.experimental import pallas as pl
from jax.experimental.pallas import tpu as pltpu

F32 = jnp.float32
BF16 = jnp.bfloat16
HIGHEST = lax.Precision.HIGHEST

D_MODEL = 2048
DEPTH = 4
D_FF = 5632
FFN_RES = 0.5
N_MOD = 9
RMS_EPS = 1e-6
NEG_INF = -1e30
LOG2_E = 1.4426950408889634

NSA_HEADS = 16
NSA_HEAD_DIM = 128
NSA_KV_GROUPS = 4
NSA_HPG = NSA_HEADS // NSA_KV_GROUPS
NSA_Q_DIM = NSA_HEADS * NSA_HEAD_DIM
NSA_KV_DIM = NSA_KV_GROUPS * NSA_HEAD_DIM
NSA_MAIN_DIM = NSA_Q_DIM + 6 * NSA_KV_DIM
CMP_BLOCK = 32
CMP_STRIDE = 16
CMP_HIDDEN = 512
SEL_BLOCK = 64
SEL_TOPK = 16
N_LOCAL_BLOCKS = 2
FORCE_SCORE = 1e9
WINDOW = 512
ROPE_THETA = 10000.0

ML_HEADS = 8
ML_QK_DIM = 128
ML_V_DIM = 256
ML_CONV = 4
ML_CHUNK = 64
ML_QK_COLS = 2 * ML_HEADS * ML_QK_DIM
ML_MAIN_DIM = ML_QK_COLS + 2 * ML_HEADS * ML_V_DIM

LANES = 128
SUBLANES = 8
VMEM_LIMIT = 56 * 1024 * 1024

NT_DIMS = (((1,), (1,)), ((), ()))
TN_DIMS = (((0,), (0,)), ((), ()))


def _cparams(sem):
    return pltpu.CompilerParams(dimension_semantics=sem, vmem_limit_bytes=VMEM_LIMIT)


def _silu(x):
    return x * jax.nn.sigmoid(x)


def _rms(x):
    return x * lax.rsqrt(jnp.mean(x * x, axis=-1, keepdims=True) + RMS_EPS)


def _dot(a, b):
    return jnp.dot(a, b, preferred_element_type=F32)


def _dot_nt(a, b):
    return lax.dot_general(a, b, NT_DIMS, preferred_element_type=F32)


def _mod_kernel(c_ref, w_ref, b_ref, o_ref):
    ca = _silu(c_ref[...]).astype(BF16)
    o_ref[...] = _dot(ca, w_ref[...].astype(BF16)) + b_ref[...]


def _mod_call(c, mod_w, mod_b):
    depth, d, n = mod_w.shape
    bsz = c.shape[0]
    tn = 1024
    return pl.pallas_call(
        _mod_kernel,
        out_shape=jax.ShapeDtypeStruct((depth, bsz, n), F32),
        grid=(depth, n // tn),
        in_specs=[
            pl.BlockSpec((bsz, d), lambda i, j: (0, 0)),
            pl.BlockSpec((None, d, tn), lambda i, j: (i, 0, j)),
            pl.BlockSpec((None, 1, tn), lambda i, j: (i, 0, j)),
        ],
        out_specs=pl.BlockSpec((None, bsz, tn), lambda i, j: (i, 0, j)),
        compiler_params=_cparams(("parallel", "parallel")),
        name="mod",
    )(c, mod_w, mod_b.reshape(depth, 1, n))


def _prenorm(x, mod_ref, ng_ref, mrow, grow):
    gain = ng_ref[grow:grow + 1, :] * (1.0 + mod_ref[mrow + 1:mrow + 2, :])
    return _rms(x) * gain + mod_ref[mrow:mrow + 1, :]


ROW_CHUNK = 16
ROW_UNROLL = 4


def _row_loop(n_rows, body):
    def step(i, carry):
        body(pl.ds(pl.multiple_of(i * ROW_CHUNK, ROW_CHUNK), ROW_CHUNK))
        return carry
    lax.fori_loop(0, n_rows // ROW_CHUNK, step, 0, unroll=ROW_UNROLL)


def _ffn_kernel(x_ref, mod_ref, ng_ref, wg_ref, wu_ref, wo_ref, o_ref, xn_ref, acc_ref, *, mrow, grow):
    j = pl.program_id(2)

    @pl.when(j == 0)
    def _():
        gain = ng_ref[grow:grow + 1, :] * (1.0 + mod_ref[mrow + 1:mrow + 2, :])
        shift = mod_ref[mrow:mrow + 1, :]

        def prenorm_rows(rows):
            xn_ref[rows, :] = (_rms(x_ref[rows, :]) * gain + shift).astype(BF16)

        _row_loop(x_ref.shape[0], prenorm_rows)
        acc_ref[...] = jnp.zeros_like(acc_ref)

    xn = xn_ref[...]
    g = _dot(xn, wg_ref[...])
    u = _dot(xn, wu_ref[...])
    acc_ref[...] += _dot((_silu(g) * u).astype(BF16), wo_ref[...])

    @pl.when(j == pl.num_programs(2) - 1)
    def _():
        gain = ng_ref[grow + 1:grow + 2, :] * (FFN_RES * mod_ref[mrow + 2:mrow + 3, :])

        def finish_rows(rows):
            o_ref[rows, :] = x_ref[rows, :] + _rms(acc_ref[rows, :]) * gain

        _row_loop(x_ref.shape[0], finish_rows)


def _ffn_call(x, mod, ng, w_in, w_out, *, layer, mrow, grow):
    bsz, s, d = x.shape
    dff = w_out.shape[1]
    tm, tf = 512, 512
    nf = dff // tf
    return pl.pallas_call(
        functools.partial(_ffn_kernel, mrow=mrow, grow=grow),
        out_shape=jax.ShapeDtypeStruct(x.shape, F32),
        grid=(bsz, s // tm, nf),
        in_specs=[
            pl.BlockSpec((None, tm, d), lambda b, m, j: (b, m, 0)),
            pl.BlockSpec((None, N_MOD, d), lambda b, m, j: (b, 0, 0)),
            pl.BlockSpec(ng.shape, lambda b, m, j: (0, 0)),
            pl.BlockSpec((None, d, tf), lambda b, m, j: (layer, 0, j)),
            pl.BlockSpec((None, d, tf), lambda b, m, j: (layer, 0, nf + j)),
            pl.BlockSpec((None, tf, d), lambda b, m, j: (layer, j, 0)),
        ],
        out_specs=pl.BlockSpec((None, tm, d), lambda b, m, j: (b, m, 0)),
        scratch_shapes=[pltpu.VMEM((tm, d), BF16), pltpu.VMEM((tm, d), F32)],
        compiler_params=_cparams(("parallel", "parallel", "arbitrary")),
        name="ffn",
    )(x, mod, ng, w_in, w_in, w_out)


def _proj_kernel(x_ref, mod_ref, ng_ref, w_ref, wgate_ref, o_ref, og_ref, xn_ref, *, mrow, grow):
    @pl.when(pl.program_id(2) == 0)
    def _():
        xn = _prenorm(x_ref[...], mod_ref, ng_ref, mrow, grow).astype(BF16)
        xn_ref[...] = xn
        og_ref[...] = _dot(xn, wgate_ref[...])

    o_ref[...] = _dot(xn_ref[...], w_ref[...])


def _proj_call(x, mod, ng, w_in, w_gate, *, layer, n, mrow, grow):
    bsz, s, d = x.shape
    ngate = w_gate.shape[1]
    tm, tn = 1024, 1024
    return pl.pallas_call(
        functools.partial(_proj_kernel, mrow=mrow, grow=grow),
        out_shape=(jax.ShapeDtypeStruct((bsz, s, n), F32), jax.ShapeDtypeStruct((bsz, s, ngate), F32)),
        grid=(bsz, s // tm, n // tn),
        in_specs=[
            pl.BlockSpec((None, tm, d), lambda b, m, j: (b, m, 0)),
            pl.BlockSpec((None, N_MOD, d), lambda b, m, j: (b, 0, 0)),
            pl.BlockSpec(ng.shape, lambda b, m, j: (0, 0)),
            pl.BlockSpec((None, d, tn), lambda b, m, j: (layer, 0, j)),
            pl.BlockSpec((d, ngate), lambda b, m, j: (0, 0)),
        ],
        out_specs=(pl.BlockSpec((None, tm, tn), lambda b, m, j: (b, m, j)),
                   pl.BlockSpec((None, tm, ngate), lambda b, m, j: (b, m, 0))),
        scratch_shapes=[pltpu.VMEM((tm, d), BF16)],
        compiler_params=_cparams(("parallel", "parallel", "arbitrary")),
        name="proj",
    )(x, mod, ng, w_in, w_gate)


def _oproj_kernel(a_ref, w_ref, x_ref, mod_ref, ng_ref, o_ref, *, mrow, grow):
    yn = _rms(_dot(a_ref[...], w_ref[...])) * ng_ref[grow:grow + 1, :]
    o_ref[...] = x_ref[...] + mod_ref[mrow:mrow + 1, :] * yn


def _oproj_call(a, w, x, mod, ng, *, layer, mrow, grow):
    bsz, s, d = x.shape
    k = a.shape[-1]
    tm = 512
    return pl.pallas_call(
        functools.partial(_oproj_kernel, mrow=mrow, grow=grow),
        out_shape=jax.ShapeDtypeStruct(x.shape, F32),
        grid=(bsz, s // tm),
        in_specs=[
            pl.BlockSpec((None, tm, k), lambda b, m: (b, m, 0)),
            pl.BlockSpec((None, k, d), lambda b, m: (layer, 0, 0)),
            pl.BlockSpec((None, tm, d), lambda b, m: (b, m, 0)),
            pl.BlockSpec((None, N_MOD, d), lambda b, m: (b, 0, 0)),
            pl.BlockSpec(ng.shape, lambda b, m: (0, 0)),
        ],
        out_specs=pl.BlockSpec((None, tm, d), lambda b, m: (b, m, 0)),
        compiler_params=_cparams(("parallel", "parallel")),
        name="oproj",
    )(a, w, x, mod, ng)


def _rope(x, cos, sin_signed):
    return x * cos + pltpu.roll(x, NSA_HEAD_DIM // 2, 1) * sin_signed


def _nsa_prep_kernel(kc_ref, vc_ref, ks_ref, vs_ref, kw_ref, vw_ref, pe_ref, w1_ref, w2_ref, cos_ref, sin_ref,
                     kco_ref, vco_ref, kso_ref, vso_ref, kwo_ref, vwo_ref):
    s = kc_ref.shape[0]
    nrow = s // CMP_STRIDE
    half = CMP_BLOCK // CMP_STRIDE

    def compress(t_ref, idx):
        streams = [t_ref[pl.ds(j, nrow, stride=CMP_STRIDE), :] for j in range(CMP_STRIDE)]
        hidden = None
        for h in range(half):
            flat = jnp.concatenate(
                [(streams[j] + pe_ref[idx, h * CMP_STRIDE + j:h * CMP_STRIDE + j + 1, :]).astype(BF16)
                 for j in range(CMP_STRIDE)], axis=1)
            w = w1_ref[idx, h * CMP_STRIDE * NSA_HEAD_DIM:(h + 1) * CMP_STRIDE * NSA_HEAD_DIM, :]
            part = _dot(flat, w)
            if h:
                part = pltpu.roll(part, nrow - h, 0)
            hidden = part if hidden is None else hidden + part
        return _dot(_silu(hidden).astype(BF16), w2_ref[idx])

    kco_ref[...] = compress(kc_ref, 0).astype(kco_ref.dtype)
    vco_ref[...] = compress(vc_ref, 1).astype(vco_ref.dtype)
    cos = cos_ref[...]
    sin = sin_ref[...]
    kso_ref[...] = _rope(ks_ref[...], cos, sin).astype(BF16)
    kwo_ref[...] = _rope(kw_ref[...], cos, sin).astype(BF16)
    vso_ref[...] = vs_ref[...].T.astype(BF16)
    vwo_ref[...] = vw_ref[...].T.astype(BF16)


def _nsa_prep_call(main, pe, w1, w2, cos, sin, *, layer):
    bsz, s, _ = main.shape
    g, dh = NSA_KV_GROUPS, NSA_HEAD_DIM
    nrow = s // CMP_STRIDE
    q_blocks = NSA_Q_DIM // dh

    def col(k):
        return pl.BlockSpec((None, s, dh), lambda b, gi, k=k: (b, 0, q_blocks + k * g + gi))

    def full(a):
        return pl.BlockSpec(a.shape, lambda b, gi, nd=a.ndim: (0,) * nd)

    def stacked(a):
        return pl.BlockSpec((None,) + a.shape[1:], lambda b, gi, nd=a.ndim: (layer,) + (0,) * (nd - 1))

    def out(rows, cols):
        return (jax.ShapeDtypeStruct((bsz, g, rows, cols), BF16),
                pl.BlockSpec((None, None, rows, cols), lambda b, gi: (b, gi, 0, 0)))

    outs = [out(nrow, dh), out(nrow, dh), out(s, dh), out(dh, s), out(s, dh), out(dh, s)]
    return pl.pallas_call(
        _nsa_prep_kernel,
        out_shape=tuple(o[0] for o in outs),
        grid=(bsz, g),
        in_specs=[col(0), col(1), col(2), col(3), col(4), col(5), stacked(pe), stacked(w1), stacked(w2),
                  full(cos), full(sin)],
        out_specs=tuple(o[1] for o in outs),
        compiler_params=_cparams(("parallel", "parallel")),
        name="nsa_prep",
    )(main, main, main, main, main, main, pe, w1, w2, cos, sin)


def _nsa_attn_kernel(q_ref, gl_ref, cos_ref, sin_ref, kc_ref, vc_ref, ks_ref, vst_ref, kw_ref, vwt_ref,
                     cover_ref, o_ref, qrot_scr, selb_scr, m_scr, l_scr, alpha_scr, acc_scr, out_scr, s_scr, p_scr,
                     *, tq, tk, n_sel):
    qi = pl.program_id(2)
    dh = NSA_HEAD_DIM
    ng = kc_ref.shape[0]
    heads = [(g, r) for g in range(ng) for r in range(NSA_HPG)]
    scale = dh ** -0.5 * LOG2_E
    pos_q = qi * tq + lax.broadcasted_iota(jnp.int32, (1, tq), 1)
    cos = cos_ref[...]
    sin = sin_ref[...]
    gl_t = gl_ref[...].T
    gates = [jax.nn.sigmoid(gl_t[g * LANES:g * LANES + 4 * NSA_HPG, :]) for g in range(ng)]

    def gate(g, r, branch):
        return gates[g][3 * r + branch:3 * r + branch + 1, :]

    crow = lax.broadcasted_iota(jnp.int32, (LANES, 1), 0)
    cmask = crow * CMP_STRIDE + (CMP_BLOCK - 1) <= pos_q
    jrow = lax.broadcasted_iota(jnp.int32, (n_sel, 1), 0)
    q_blk = pos_q // SEL_BLOCK
    causal = jrow <= q_blk
    forced = (jrow == 0) | (causal & (jrow > q_blk - N_LOCAL_BLOCKS))
    top_k = min(SEL_TOPK, n_sel)
    rank_unroll = 4
    n_causal = (qi * tq + tq) // SEL_BLOCK
    n_rank_steps = jnp.where(n_causal <= top_k, 0, n_causal // rank_unroll)
    for h in range(len(heads)):
        acc_scr[h] = q_ref[:, h * dh:(h + 1) * dh].T * scale
    for h, (g, _) in enumerate(heads):
        s_scr[h, :LANES, :] = _dot(kc_ref[g], acc_scr[h].astype(BF16))
    for h in range(len(heads)):
        qt = acc_scr[h]
        rot = jnp.concatenate([qt[dh // 2:], qt[:dh // 2]], axis=0)
        qrot_scr[h] = (qt * cos + rot * sin).astype(BF16)
    for g in range(ng):
        p_sum = jnp.zeros((LANES, tq), F32)
        for h in range(g * NSA_HPG, (g + 1) * NSA_HPG):
            sc = jnp.where(cmask, s_scr[h, :LANES, :], NEG_INF)
            e = jnp.exp2(sc - jnp.max(sc, axis=0, keepdims=True))
            p = jnp.where(cmask, e * (1.0 / jnp.sum(e, axis=0, keepdims=True)), 0.0)
            p_sum = p_sum + p
            p_scr[0, h, :LANES, :] = p.astype(BF16)
        vct = vc_ref[g].astype(F32).T.astype(BF16)
        for r in range(NSA_HPG):
            h = g * NSA_HPG + r
            out_scr[h] = gate(g, r, 0) * _dot(vct, p_scr[0, h, :LANES, :])

        imp = jnp.dot(cover_ref[...], p_sum, precision=HIGHEST, preferred_element_type=F32)[:n_sel]
        imp = jnp.where(forced, FORCE_SCORE, jnp.where(causal, imp, -1.0))
        selb_scr[g] = imp

        def rank_body(i, rank, g=g, imp=imp):
            for k in range(rank_unroll):
                jp = i * rank_unroll + k
                other = selb_scr[g, pl.ds(jp, 1), :]
                tie = jnp.where(jrow > jp, 1.0, 0.0)
                rank = rank + jnp.where(other > imp, 1.0, jnp.where(other == imp, tie, 0.0))
            return rank

        rank = lax.fori_loop(0, n_rank_steps, rank_body, jnp.zeros((n_sel, tq), F32))
        keep = (rank < float(top_k)) & (imp >= 0.0)
        selb_scr[g] = jnp.where(keep, 0.0, NEG_INF)

    def reset(first_kt):
        m_scr[...] = jnp.full(m_scr.shape, NEG_INF, F32)
        l_scr[...] = jnp.zeros(l_scr.shape, F32)
        acc_scr[...] = jnp.zeros(acc_scr.shape, F32)
        alpha_scr[...] = jnp.ones(alpha_scr.shape, F32)
        p_scr[(first_kt - 1) & 1] = jnp.zeros(p_scr.shape[1:], BF16)

    def tile0(kt):
        return pl.multiple_of(jnp.maximum(kt, 0) * tk, tk)

    def apply_pending(vt_ref, kt):
        slot = kt & 1
        for g in range(ng):
            vt = vt_ref[g, :, pl.ds(tile0(kt), tk)]
            for h in range(g * NSA_HPG, (g + 1) * NSA_HPG):
                acc_scr[h] = alpha_scr[slot, h] * acc_scr[h] + _dot(vt, p_scr[slot, h])

    def flash_step(k_ref, vt_ref, kt, bias):
        slot = kt & 1
        for g in range(ng):
            k_tile = k_ref[g, pl.ds(tile0(kt), tk), :]
            for h in range(g * NSA_HPG, (g + 1) * NSA_HPG):
                s_scr[h] = _dot(k_tile, qrot_scr[h])
        apply_pending(vt_ref, kt - 1)
        for h, (g, _) in enumerate(heads):
            sc = s_scr[h]
            head_bias = bias[g] if isinstance(bias, list) else bias
            if head_bias is not None:
                sc = sc + head_bias
            m_old = m_scr[h]
            m_new = jnp.maximum(m_old, jnp.max(sc, axis=0, keepdims=True))
            alpha = jnp.exp2(m_old - m_new)
            p = jnp.exp2(sc - m_new)
            l_scr[h] = alpha * l_scr[h] + jnp.sum(p, axis=0, keepdims=True)
            alpha_scr[slot, h] = alpha
            p_scr[slot, h] = p.astype(BF16)
            m_scr[h] = m_new

    def finish(branch):
        for h, (g, r) in enumerate(heads):
            w = gate(g, r, branch) / l_scr[h]
            out_scr[h] = out_scr[h] + w * acc_scr[h]

    def sel_bias(kt, extra=None):
        per_tile = tk // SEL_BLOCK
        biases = []
        for g in range(ng):
            rows = [selb_scr[g, pl.ds(kt * per_tile + j, 1), :] for j in range(per_tile)]
            b = jnp.concatenate([jnp.broadcast_to(row, (SEL_BLOCK, tq)) for row in rows], axis=0)
            biases.append(b if extra is None else b + extra)
        return biases

    def key_pos(kt):
        return kt * tk + lax.broadcasted_iota(jnp.int32, (tk, 1), 0)

    causal_bias = jnp.where(key_pos(qi) <= pos_q, 0.0, NEG_INF)

    reset(0)

    def sel_body(kt, carry):
        flash_step(ks_ref, vst_ref, kt, sel_bias(kt))
        return carry

    lax.fori_loop(0, qi, sel_body, 0)
    flash_step(ks_ref, vst_ref, qi, sel_bias(qi, causal_bias))
    apply_pending(vst_ref, qi)
    finish(1)

    n_back = WINDOW // tk
    reset(jnp.maximum(qi - n_back, 0))
    for back in range(n_back, 0, -1):
        @pl.when(qi >= back)
        def _(back=back):
            bias = jnp.where(pos_q - key_pos(qi - back) < WINDOW, 0.0, NEG_INF) if back == n_back else None
            flash_step(kw_ref, vwt_ref, qi - back, bias)
    flash_step(kw_ref, vwt_ref, qi, causal_bias)
    apply_pending(vwt_ref, qi)
    finish(2)

    for h in range(len(heads)):
        o_ref[:, h * dh:(h + 1) * dh] = out_scr[h].T.astype(o_ref.dtype)


def _nsa_attn_call(main, gl, cos_t, sin_t, kc, vc, ks, vst, kw, vwt, cover_t):
    bsz, s, _ = main.shape
    g, r, dh = NSA_KV_GROUPS, NSA_HPG, NSA_HEAD_DIM
    tq = tk = 256
    ng = 1
    nh = ng * r
    assert WINDOW % tk == 0 and tk % SEL_BLOCK == 0 and s % tq == 0 and g % ng == 0
    nrow = kc.shape[2]
    n_sel = s // SEL_BLOCK

    def kv(rows, cols):
        return pl.BlockSpec((None, ng, rows, cols), lambda b, gi, qi: (b, gi, 0, 0))

    return pl.pallas_call(
        functools.partial(_nsa_attn_kernel, tq=tq, tk=tk, n_sel=n_sel),
        out_shape=jax.ShapeDtypeStruct((bsz, s, NSA_Q_DIM), BF16),
        grid=(bsz, g // ng, s // tq),
        in_specs=[
            pl.BlockSpec((None, tq, nh * dh), lambda b, gi, qi: (b, qi, gi)),
            pl.BlockSpec((None, tq, ng * LANES), lambda b, gi, qi: (b, qi, gi)),
            pl.BlockSpec((dh, tq), lambda b, gi, qi: (0, qi)),
            pl.BlockSpec((dh, tq), lambda b, gi, qi: (0, qi)),
            kv(nrow, dh), kv(nrow, dh), kv(s, dh), kv(dh, s), kv(s, dh), kv(dh, s),
            pl.BlockSpec(cover_t.shape, lambda b, gi, qi: (0, 0)),
        ],
        out_specs=pl.BlockSpec((None, tq, nh * dh), lambda b, gi, qi: (b, qi, gi)),
        scratch_shapes=[
            pltpu.VMEM((nh, dh, tq), BF16),
            pltpu.VMEM((ng, n_sel, tq), F32),
            pltpu.VMEM((nh, 1, tq), F32),
            pltpu.VMEM((nh, 1, tq), F32),
            pltpu.VMEM((2, nh, 1, tq), F32),
            pltpu.VMEM((nh, dh, tq), F32),
            pltpu.VMEM((nh, dh, tq), F32),
            pltpu.VMEM((nh, tk, tq), F32),
            pltpu.VMEM((2, nh, tk, tq), BF16),
        ],
        compiler_params=_cparams(("parallel", "parallel", "arbitrary")),
        name="nsa_attn",
    )(main, gl, cos_t, sin_t, kc, vc, ks, vst, kw, vwt, cover_t)


def _nsa_tables(s):
    half = NSA_HEAD_DIM // 2
    freqs = ROPE_THETA ** (-jnp.arange(half, dtype=F32) / half)
    ang = jnp.arange(s).astype(F32)[:, None] * freqs[None, :]
    cos, sin = jnp.cos(ang), jnp.sin(ang)
    cos_full = jnp.concatenate([cos, cos], axis=-1)
    sin_signed = jnp.concatenate([-sin, sin], axis=-1)
    n_cmp = (s - CMP_BLOCK) // CMP_STRIDE + 1
    cs = np.arange(LANES)[None, :] * CMP_STRIDE
    js = np.arange(LANES)[:, None]
    cover_t = ((cs < (js + 1) * SEL_BLOCK) & (cs + CMP_BLOCK > js * SEL_BLOCK)
               & (np.arange(LANES)[None, :] < n_cmp) & (js < s // SEL_BLOCK)).astype(np.float32)
    return cos_full, sin_signed, cos_full.T, sin_signed.T, jnp.asarray(cover_t, F32)


def _nsa_layer(x, mod, ng, w_in, cmp_pos, cmp_w1, cmp_w2, w_out, tables, *, layer):
    cos, sin, cos_t, sin_t, cover_t = tables
    g, r = NSA_KV_GROUPS, NSA_HPG
    w_gate = w_in[layer, :, NSA_MAIN_DIM:].reshape(D_MODEL, g, 3 * r)
    w_gate = jnp.pad(w_gate, ((0, 0), (0, 0), (0, LANES - 3 * r))).reshape(D_MODEL, g * LANES)
    main, gl = _proj_call(x, mod, ng, w_in, w_gate, layer=layer, n=NSA_MAIN_DIM, mrow=3, grow=2)
    kc, vc, ks, vst, kw, vwt = _nsa_prep_call(main, cmp_pos, cmp_w1, cmp_w2, cos, sin, layer=layer)
    o = _nsa_attn_call(main, gl, cos_t, sin_t, kc, vc, ks, vst, kw, vwt, cover_t)
    return _oproj_call(o, w_out, x, mod, ng, layer=layer, mrow=5, grow=3)


def _mlstm_kernel(q_ref, k_ref, v_ref, og_ref, gi_ref, gf_ref, bi_ref, bf_ref, cwq_ref, cwk_ref, cbq_ref, cbk_ref,
                  gain_ref, o_ref, qpad_scr, kpad_scr, ct_scr, rowi_scr, chunk_scr, bcol_scr, ecol_scr):
    s, dk = q_ref.shape
    dv = v_ref.shape[1]
    chunk = ML_CHUNK
    nc = s // chunk
    pad = qpad_scr.shape[0] - s

    for src, dst in ((q_ref, qpad_scr), (k_ref, kpad_scr)):
        dst[:pad] = jnp.zeros((pad, dk), F32)
        dst[pad:] = src[...]

    def conv_silu(pad_ref, w_ref, b_ref, r0):
        win = pad_ref[pl.ds(r0, chunk + pad), :]
        acc = win[pad:] * w_ref[ML_CONV - 1:ML_CONV, :] + b_ref[...]
        for d in range(1, ML_CONV):
            acc = acc + pltpu.roll(win, d, 0)[pad:] * w_ref[ML_CONV - 1 - d:ML_CONV - d, :]
        return _silu(acc)

    ig = gi_ref[...] + bi_ref[...]
    fg = gf_ref[...] + bf_ref[...]
    logf = jnp.minimum(fg, 0.0) - jnp.log1p(jnp.exp(-jnp.abs(fg)))
    tri_r = lax.broadcasted_iota(jnp.int32, (chunk, chunk), 0)
    tri_c = lax.broadcasted_iota(jnp.int32, (chunk, chunk), 1)
    upper = jnp.where(tri_r <= tri_c, 1.0, 0.0)
    b = jnp.dot(logf, upper, precision=HIGHEST, preferred_element_type=F32)
    w_end = b[:, chunk - 1:chunk] - b + ig
    m_loc = jnp.max(w_end, axis=-1, keepdims=True)
    e_end = jnp.exp(w_end - m_loc)
    rowi_scr[...] = ig - b
    b_tot = b[:, chunk - 1:chunk]
    m_run = jnp.zeros((1, 1), F32)
    m_before, m_after = [], []
    for c in range(nc):
        m_before.append(m_run)
        m_run = jnp.maximum(b_tot[c:c + 1, :] + m_run, m_loc[c:c + 1, :])
        m_after.append(m_run)
    m_prev = jnp.concatenate(m_before, axis=0)
    m_next = jnp.concatenate(m_after, axis=0)
    for i, val in enumerate((m_prev, jnp.exp(b_tot + m_prev - m_next), jnp.exp(m_loc - m_next))):
        chunk_scr[i] = jnp.broadcast_to(val, (nc, LANES))
    eye = jnp.where(tri_r == tri_c, 1.0, 0.0)
    cols = lax.dot_general(eye, jnp.concatenate([b, e_end], axis=0), NT_DIMS, precision=HIGHEST,
                           preferred_element_type=F32)
    for c in range(nc):
        bcol_scr[c] = jnp.broadcast_to(cols[:, c:c + 1], (chunk, LANES))
        ecol_scr[c] = jnp.broadcast_to(cols[:, nc + c:nc + c + 1], (chunk, LANES))

    ct_scr[...] = jnp.zeros(ct_scr.shape, F32)
    lower = tri_c <= tri_r
    gain = gain_ref[...]

    n_state_tiles = ct_scr.shape[1] // LANES

    def local_part(c):
        r0 = pl.multiple_of(c * chunk, chunk)
        q = conv_silu(qpad_scr, cwq_ref, cbq_ref, r0).astype(BF16)
        k = conv_silu(kpad_scr, cwk_ref, cbk_ref, r0) * (dk ** -0.5)
        va = jnp.concatenate([v_ref[pl.ds(r0, chunk), :].astype(BF16), jnp.ones((chunk, LANES), BF16)], axis=1)
        m_prev = chunk_scr[0, pl.ds(c, 1), :]
        bcol = bcol_scr[c]
        log_intra = jnp.where(lower, bcol[:, :chunk] + rowi_scr[pl.ds(c, 1), :], NEG_INF)
        log_inter = bcol[:, :1] + m_prev[:, :1]
        m_t = jnp.maximum(log_inter, jnp.max(log_intra, axis=-1, keepdims=True))
        qk = _dot_nt(q, k.astype(BF16)) * jnp.exp(log_intra - m_t)
        intra = _dot(qk.astype(BF16), va)
        c_loc = lax.dot_general((k * ecol_scr[c]).astype(BF16), va, TN_DIMS, preferred_element_type=F32)
        return q, intra, c_loc, jnp.exp(log_inter - m_t), jnp.exp(-m_t)

    def state_part(c, q, intra, c_loc, e_inter, floor):
        r0 = pl.multiple_of(c * chunk, chunk)
        decay, inject = (chunk_scr[i, pl.ds(c, 1), :] for i in (1, 2))
        ct = ct_scr[...]
        tot = intra + e_inter * _dot(q, ct.astype(BF16))
        h = tot[:, :dv] / jnp.maximum(jnp.abs(tot[:, dv:dv + 1]), floor)
        hn = _rms(h) * gain
        o_ref[pl.ds(r0, chunk), :] = (jax.nn.sigmoid(og_ref[pl.ds(r0, chunk), :]) * hn).astype(o_ref.dtype)
        ct_scr[...] = (jnp.concatenate([decay] * n_state_tiles, axis=1) * ct
                       + jnp.concatenate([inject] * n_state_tiles, axis=1) * c_loc)

    group = 4

    def body(i, carry):
        parts = [local_part(i * group + j) for j in range(group)]
        for j in range(group):
            state_part(i * group + j, *parts[j])
        return carry

    lax.fori_loop(0, nc // group, body, 0)


def _mlstm_call(main, gates_t, gate_b, conv_w, conv_b, gain):
    bsz, s, _ = main.shape
    h, dk, dv = ML_HEADS, ML_QK_DIM, ML_V_DIM
    chunk = ML_CHUNK
    nc = s // chunk
    vblk0 = ML_QK_COLS // dv

    def gate(off):
        return pl.BlockSpec((None, None, nc, chunk), lambda b, hi: (b, off + hi, 0, 0))

    def bias(off):
        return pl.BlockSpec((None, 1, 1), lambda b, hi: (off + hi, 0, 0))

    return pl.pallas_call(
        _mlstm_kernel,
        out_shape=jax.ShapeDtypeStruct((bsz, s, h * dv), BF16),
        grid=(bsz, h),
        in_specs=[
            pl.BlockSpec((None, s, dk), lambda b, hi: (b, 0, hi)),
            pl.BlockSpec((None, s, dk), lambda b, hi: (b, 0, h + hi)),
            pl.BlockSpec((None, s, dv), lambda b, hi: (b, 0, vblk0 + hi)),
            pl.BlockSpec((None, s, dv), lambda b, hi: (b, 0, vblk0 + h + hi)),
            gate(0), gate(h), bias(0), bias(h),
            pl.BlockSpec((ML_CONV, dk), lambda b, hi: (0, hi)),
            pl.BlockSpec((ML_CONV, dk), lambda b, hi: (0, h + hi)),
            pl.BlockSpec((1, dk), lambda b, hi: (0, hi)),
            pl.BlockSpec((1, dk), lambda b, hi: (0, h + hi)),
            pl.BlockSpec((1, dv), lambda b, hi: (0, hi)),
        ],
        out_specs=pl.BlockSpec((None, s, dv), lambda b, hi: (b, 0, hi)),
        scratch_shapes=[
            pltpu.VMEM((s + SUBLANES, dk), F32),
            pltpu.VMEM((s + SUBLANES, dk), F32),
            pltpu.VMEM((dk, dv + LANES), F32),
            pltpu.VMEM((nc, chunk), F32),
            pltpu.VMEM((3, nc, LANES), F32),
            pltpu.VMEM((nc, chunk, LANES), F32),
            pltpu.VMEM((nc, chunk, LANES), F32),
        ],
        compiler_params=_cparams(("parallel", "parallel")),
        name="mlstm",
    )(main, main, main, main, gates_t, gates_t, gate_b, gate_b, conv_w, conv_w, conv_b, conv_b, gain)


def _mlstm_layer(x, mod, ng, w_in, conv_w, conv_b, gate_b, mh_gain, w_out, *, layer):
    bsz, s, _ = x.shape
    ngate = 2 * ML_HEADS
    w_gate = jnp.pad(w_in[layer, :, ML_MAIN_DIM:], ((0, 0), (0, LANES - ngate)))
    main, gl = _proj_call(x, mod, ng, w_in, w_gate, layer=layer, n=ML_MAIN_DIM, mrow=3, grow=2)
    gates_t = jnp.swapaxes(gl[:, :, :ngate], 1, 2).reshape(bsz, ngate, s // ML_CHUNK, ML_CHUNK)
    o = _mlstm_call(main, gates_t, gate_b.reshape(ngate, 1, 1), conv_w, conv_b.reshape(1, -1),
                    mh_gain.reshape(1, -1))
    return _oproj_call(o, w_out, x, mod, ng, layer=layer, mrow=5, grow=3)


def kernel(x, c, mod_w, mod_b, norm_g, ffn_pre_w_in, ffn_pre_w_out, ffn_post_w_in, ffn_post_w_out, nsa_w_in, nsa_cmp_pos, nsa_cmp_w1, nsa_cmp_w2, nsa_w_out, ml_w_in, ml_conv_w, ml_conv_b, ml_gate_b, ml_mh_gain, ml_w_out):
    bsz, s, d = x.shape
    depth = mod_w.shape[0]
    mods = _mod_call(c, mod_w, mod_b).reshape(depth, bsz, N_MOD, d)
    tables = _nsa_tables(s)
    pre_in, pre_out = ffn_pre_w_in.astype(BF16), ffn_pre_w_out.astype(BF16)
    post_in, post_out = ffn_post_w_in.astype(BF16), ffn_post_w_out.astype(BF16)
    nsa_in, nsa_out = nsa_w_in.astype(BF16), nsa_w_out.astype(BF16)
    nsa_w1, nsa_w2 = nsa_cmp_w1.astype(BF16), nsa_cmp_w2.astype(BF16)
    ml_in, ml_out = ml_w_in.astype(BF16), ml_w_out.astype(BF16)
    for i in range(depth):
        mod, ng = mods[i], norm_g[i]
        x = _ffn_call(x, mod, ng, pre_in, pre_out, layer=i, mrow=0, grow=0)
        j = i // 2
        if i % 2 == 0:
            x = _nsa_layer(x, mod, ng, nsa_in, nsa_cmp_pos, nsa_w1, nsa_w2, nsa_out, tables, layer=j)
        else:
            x = _mlstm_layer(x, mod, ng, ml_in, ml_conv_w[j], ml_conv_b[j], ml_gate_b[j], ml_mh_gain[j], ml_out,
                             layer=j)
        x = _ffn_call(x, mod, ng, post_in, post_out, layer=i, mrow=6, grow=4)
    return x
```

```python
import functools

import jax
import jax.numpy as jnp
from jax import lax
from jax.experimental import pallas as pl
from jax.experimental.pallas import tpu as pltpu

F32 = jnp.float32
BF16 = jnp.bfloat16
HIGHEST = lax.Precision.HIGHEST

D_MODEL = 2048
DEPTH = 4
D_FF = 5632
FFN_RES = 0.5
N_MOD = 9
RMS_EPS = 1e-6
NEG_INF = -1e30
LOG2_E = 1.4426950408889634

NSA_HEADS = 16
NSA_HEAD_DIM = 128
NSA_KV_GROUPS = 4
NSA_HPG = NSA_HEADS // NSA_KV_GROUPS
NSA_Q_DIM = NSA_HEADS * NSA_HEAD_DIM
NSA_KV_DIM = NSA_KV_GROUPS * NSA_HEAD_DIM
NSA_MAIN_DIM = NSA_Q_DIM + 6 * NSA_KV_DIM
CMP_BLOCK = 32
CMP_STRIDE = 16
CMP_HIDDEN = 512
SEL_BLOCK = 64
SEL_TOPK = 16
N_LOCAL_BLOCKS = 2
FORCE_SCORE = 1e9
WINDOW = 512
ROPE_THETA = 10000.0

ML_HEADS = 8
ML_QK_DIM = 128
ML_V_DIM = 256
ML_CONV = 4
ML_CHUNK = 64
ML_QK_COLS = 2 * ML_HEADS * ML_QK_DIM
ML_MAIN_DIM = ML_QK_COLS + 2 * ML_HEADS * ML_V_DIM

LANES = 128
SUBLANES = 8
VMEM_LIMIT = 56 * 1024 * 1024

NT_DIMS = (((1,), (1,)), ((), ()))
TN_DIMS = (((0,), (0,)), ((), ()))


def _cparams(sem):
    return pltpu.CompilerParams(dimension_semantics=sem, vmem_limit_bytes=VMEM_LIMIT)


def _silu(x):
    return x * jax.nn.sigmoid(x)


def _rms(x):
    return x * lax.rsqrt(jnp.mean(x * x, axis=-1, keepdims=True) + RMS_EPS)


def _dot(a, b):
    return jnp.dot(a, b, preferred_element_type=F32)


def _dot_nt(a, b):
    return lax.dot_general(a, b, NT_DIMS, preferred_element_type=F32)


def _mod_kernel(c_ref, w_ref, b_ref, o_ref):
    ca = _silu(c_ref[...]).astype(BF16)
    o_ref[...] = _dot(ca, w_ref[...].astype(BF16)) + b_ref[...]


def _mod_call(c, mod_w, mod_b):
    depth, d, n = mod_w.shape
    bsz = c.shape[0]
    tn = 1024
    return pl.pallas_call(
        _mod_kernel,
        out_shape=jax.ShapeDtypeStruct((depth, bsz, n), F32),
        grid=(depth, n // tn),
        in_specs=[
            pl.BlockSpec((bsz, d), lambda i, j: (0, 0)),
            pl.BlockSpec((None, d, tn), lambda i, j: (i, 0, j)),
            pl.BlockSpec((None, 1, tn), lambda i, j: (i, 0, j)),
        ],
        out_specs=pl.BlockSpec((None, bsz, tn), lambda i, j: (i, 0, j)),
        compiler_params=_cparams(("parallel", "parallel")),
        name="mod",
    )(c, mod_w, mod_b.reshape(depth, 1, n))


def _prenorm(x, mod_ref, ng_ref, mrow, grow):
    gain = ng_ref[grow:grow + 1, :] * (1.0 + mod_ref[mrow + 1:mrow + 2, :])
    return _rms(x) * gain + mod_ref[mrow:mrow + 1, :]


ROW_CHUNK = 16
ROW_UNROLL = 4


def _inv_rms(x):
    return lax.rsqrt(jnp.mean(x * x, axis=-1, keepdims=True) + RMS_EPS)


def _row_loop(n_rows, stats, apply):
    def step(i, carry):
        slices = [pl.ds(pl.multiple_of((i * ROW_UNROLL + u) * ROW_CHUNK, ROW_CHUNK), ROW_CHUNK)
                  for u in range(ROW_UNROLL)]
        stat = [stats(rows) for rows in slices]
        for rows, st in zip(slices, stat):
            apply(rows, st)
        return carry
    lax.fori_loop(0, n_rows // (ROW_CHUNK * ROW_UNROLL), step, 0)


def _ffn_kernel(x_ref, mod_ref, ng_ref, wg_ref, wu_ref, wo_ref, o_ref, xn_ref, acc_ref, *, mrow, grow):
    j = pl.program_id(2)

    @pl.when(j == 0)
    def _():
        gain = ng_ref[grow:grow + 1, :] * (1.0 + mod_ref[mrow + 1:mrow + 2, :])
        shift = mod_ref[mrow:mrow + 1, :]

        def prenorm_rows(rows, inv):
            xn_ref[rows, :] = (x_ref[rows, :] * inv * gain + shift).astype(BF16)

        _row_loop(x_ref.shape[0], lambda rows: _inv_rms(x_ref[rows, :]), prenorm_rows)
        acc_ref[...] = jnp.zeros_like(acc_ref)

    xn = xn_ref[...]
    g = _dot(xn, wg_ref[...])
    u = _dot(xn, wu_ref[...])
    acc_ref[...] += _dot((_silu(g) * u).astype(BF16), wo_ref[...])

    @pl.when(j == pl.num_programs(2) - 1)
    def _():
        gain = ng_ref[grow + 1:grow + 2, :] * (FFN_RES * mod_ref[mrow + 2:mrow + 3, :])

        def finish_rows(rows, inv):
            o_ref[rows, :] = x_ref[rows, :] + acc_ref[rows, :] * inv * gain

        _row_loop(x_ref.shape[0], lambda rows: _inv_rms(acc_ref[rows, :]), finish_rows)


def _ffn_call(x, mod, ng, w_in, w_out, *, layer, mrow, grow):
    bsz, s, d = x.shape
    dff = w_out.shape[1]
    tm, tf = 512, 512
    nf = dff // tf
    return pl.pallas_call(
        functools.partial(_ffn_kernel, mrow=mrow, grow=grow),
        out_shape=jax.ShapeDtypeStruct(x.shape, F32),
        grid=(bsz, s // tm, nf),
        in_specs=[
            pl.BlockSpec((None, tm, d), lambda b, m, j: (b, m, 0)),
            pl.BlockSpec((None, N_MOD, d), lambda b, m, j: (b, 0, 0)),
            pl.BlockSpec(ng.shape, lambda b, m, j: (0, 0)),
            pl.BlockSpec((None, d, tf), lambda b, m, j: (layer, 0, j)),
            pl.BlockSpec((None, d, tf), lambda b, m, j: (layer, 0, nf + j)),
            pl.BlockSpec((None, tf, d), lambda b, m, j: (layer, j, 0)),
        ],
        out_specs=pl.BlockSpec((None, tm, d), lambda b, m, j: (b, m, 0)),
        scratch_shapes=[pltpu.VMEM((tm, d), BF16), pltpu.VMEM((tm, d), F32)],
        compiler_params=_cparams(("parallel", "parallel", "arbitrary")),
        name="ffn",
    )(x, mod, ng, w_in, w_in, w_out)


def _proj_kernel(x_ref, mod_ref, ng_ref, w_ref, wgate_ref, o_ref, og_ref, xn_ref, *, mrow, grow):
    @pl.when(pl.program_id(2) == 0)
    def _():
        xn = _prenorm(x_ref[...], mod_ref, ng_ref, mrow, grow).astype(BF16)
        xn_ref[...] = xn
        og_ref[...] = _dot(xn, wgate_ref[...])

    o_ref[...] = _dot(xn_ref[...], w_ref[...])


def _proj_call(x, mod, ng, w_in, w_gate, *, layer, n, mrow, grow):
    bsz, s, d = x.shape
    ngate = w_gate.shape[1]
    tm, tn = 1024, 1024
    return pl.pallas_call(
        functools.partial(_proj_kernel, mrow=mrow, grow=grow),
        out_shape=(jax.ShapeDtypeStruct((bsz, s, n), F32), jax.ShapeDtypeStruct((bsz, s, ngate), F32)),
        grid=(bsz, s // tm, n // tn),
        in_specs=[
            pl.BlockSpec((None, tm, d), lambda b, m, j: (b, m, 0)),
            pl.BlockSpec((None, N_MOD, d), lambda b, m, j: (b, 0, 0)),
            pl.BlockSpec(ng.shape, lambda b, m, j: (0, 0)),
            pl.BlockSpec((None, d, tn), lambda b, m, j: (layer, 0, j)),
            pl.BlockSpec((d, ngate), lambda b, m, j: (0, 0)),
        ],
        out_specs=(pl.BlockSpec((None, tm, tn), lambda b, m, j: (b, m, j)),
                   pl.BlockSpec((None, tm, ngate), lambda b, m, j: (b, m, 0))),
        scratch_shapes=[pltpu.VMEM((tm, d), BF16)],
        compiler_params=_cparams(("parallel", "parallel", "arbitrary")),
        name="proj",
    )(x, mod, ng, w_in, w_gate)


def _oproj_kernel(a_ref, w_ref, x_ref, mod_ref, ng_ref, o_ref, *, mrow, grow):
    yn = _rms(_dot(a_ref[...], w_ref[...])) * ng_ref[grow:grow + 1, :]
    o_ref[...] = x_ref[...] + mod_ref[mrow:mrow + 1, :] * yn


def _oproj_call(a, w, x, mod, ng, *, layer, mrow, grow):
    bsz, s, d = x.shape
    k = a.shape[-1]
    tm = 512
    return pl.pallas_call(
        functools.partial(_oproj_kernel, mrow=mrow, grow=grow),
        out_shape=jax.ShapeDtypeStruct(x.shape, F32),
        grid=(bsz, s // tm),
        in_specs=[
            pl.BlockSpec((None, tm, k), lambda b, m: (b, m, 0)),
            pl.BlockSpec((None, k, d), lambda b, m: (layer, 0, 0)),
            pl.BlockSpec((None, tm, d), lambda b, m: (b, m, 0)),
            pl.BlockSpec((None, N_MOD, d), lambda b, m: (b, 0, 0)),
            pl.BlockSpec(ng.shape, lambda b, m: (0, 0)),
        ],
        out_specs=pl.BlockSpec((None, tm, d), lambda b, m: (b, m, 0)),
        compiler_params=_cparams(("parallel", "parallel")),
        name="oproj",
    )(a, w, x, mod, ng)


def _rope(x, cos, sin_signed):
    return x * cos + pltpu.roll(x, NSA_HEAD_DIM // 2, 1) * sin_signed


def _nsa_prep_kernel(kc_ref, vc_ref, ks_ref, vs_ref, kw_ref, vw_ref, pe_ref, w1_ref, w2_ref, cos_ref, sin_ref,
                     kco_ref, vco_ref, kso_ref, vso_ref, kwo_ref, vwo_ref):
    s = kc_ref.shape[0]
    nrow = s // CMP_STRIDE
    half = CMP_BLOCK // CMP_STRIDE

    def compress(t_ref, idx):
        streams = [t_ref[pl.ds(j, nrow, stride=CMP_STRIDE), :] for j in range(CMP_STRIDE)]
        hidden = None
        for h in range(half):
            flat = jnp.concatenate(
                [(streams[j] + pe_ref[idx, h * CMP_STRIDE + j:h * CMP_STRIDE + j + 1, :]).astype(BF16)
                 for j in range(CMP_STRIDE)], axis=1)
            w = w1_ref[idx, h * CMP_STRIDE * NSA_HEAD_DIM:(h + 1) * CMP_STRIDE * NSA_HEAD_DIM, :]
            part = _dot(flat, w)
            if h:
                part = pltpu.roll(part, nrow - h, 0)
            hidden = part if hidden is None else hidden + part
        return _dot(_silu(hidden).astype(BF16), w2_ref[idx])

    kco_ref[...] = compress(kc_ref, 0).astype(kco_ref.dtype)
    vco_ref[...] = compress(vc_ref, 1).astype(vco_ref.dtype)
    cos = cos_ref[...]
    sin = sin_ref[...]
    kso_ref[...] = _rope(ks_ref[...], cos, sin).astype(BF16)
    kwo_ref[...] = _rope(kw_ref[...], cos, sin).astype(BF16)
    vso_ref[...] = vs_ref[...].T.astype(BF16)
    vwo_ref[...] = vw_ref[...].T.astype(BF16)


def _nsa_prep_call(main, pe, w1, w2, cos, sin, *, layer):
    bsz, s, _ = main.shape
    g, dh = NSA_KV_GROUPS, NSA_HEAD_DIM
    nrow = s // CMP_STRIDE
    q_blocks = NSA_Q_DIM // dh

    def col(k):
        return pl.BlockSpec((None, s, dh), lambda b, gi, k=k: (b, 0, q_blocks + k * g + gi))

    def full(a):
        return pl.BlockSpec(a.shape, lambda b, gi, nd=a.ndim: (0,) * nd)

    def stacked(a):
        return pl.BlockSpec((None,) + a.shape[1:], lambda b, gi, nd=a.ndim: (layer,) + (0,) * (nd - 1))

    def out(rows, cols):
        return (jax.ShapeDtypeStruct((bsz, g, rows, cols), BF16),
                pl.BlockSpec((None, None, rows, cols), lambda b, gi: (b, gi, 0, 0)))

    outs = [out(nrow, dh), out(nrow, dh), out(s, dh), out(dh, s), out(s, dh), out(dh, s)]
    return pl.pallas_call(
        _nsa_prep_kernel,
        out_shape=tuple(o[0] for o in outs),
        grid=(bsz, g),
        in_specs=[col(0), col(1), col(2), col(3), col(4), col(5), stacked(pe), stacked(w1), stacked(w2),
                  full(cos), full(sin)],
        out_specs=tuple(o[1] for o in outs),
        compiler_params=_cparams(("parallel", "parallel")),
        name="nsa_prep",
    )(main, main, main, main, main, main, pe, w1, w2, cos, sin)


def _nsa_attn_kernel(q_ref, gl_ref, cos_ref, sin_ref, kc_ref, vc_ref, ks_ref, vst_ref, kw_ref, vwt_ref,
                     o_ref, qrot_scr, selb_scr, m_scr, l_scr, alpha_scr, acc_scr, out_scr, s_scr, p_scr, psum_scr,
                     *, tq, tk, n_sel):
    qi = pl.program_id(2)
    dh = NSA_HEAD_DIM
    ng = kc_ref.shape[0]
    heads = [(g, r) for g in range(ng) for r in range(NSA_HPG)]
    scale = dh ** -0.5 * LOG2_E
    pos_q = qi * tq + lax.broadcasted_iota(jnp.int32, (1, tq), 1)
    cos = cos_ref[...]
    sin = sin_ref[...]
    gl_t = gl_ref[...].T
    gates = [jax.nn.sigmoid(gl_t[g * LANES:g * LANES + 4 * NSA_HPG, :]) for g in range(ng)]

    def gate(g, r, branch):
        return gates[g][3 * r + branch:3 * r + branch + 1, :]

    crow = lax.broadcasted_iota(jnp.int32, (LANES, 1), 0)
    cmask = crow * CMP_STRIDE + (CMP_BLOCK - 1) <= pos_q
    jrow = lax.broadcasted_iota(jnp.int32, (n_sel, 1), 0)
    q_blk = pos_q // SEL_BLOCK
    causal = jrow <= q_blk
    forced = (jrow == 0) | (causal & (jrow > q_blk - N_LOCAL_BLOCKS))
    top_k = min(SEL_TOPK, n_sel)
    rank_unroll = 4
    n_causal = (qi * tq + tq) // SEL_BLOCK
    n_rank_steps = jnp.where(n_causal <= top_k, 0, n_causal // rank_unroll)
    for h in range(len(heads)):
        acc_scr[h] = q_ref[:, h * dh:(h + 1) * dh].T * scale
    for h, (g, _) in enumerate(heads):
        s_scr[h, :LANES, :] = _dot(kc_ref[g], acc_scr[h].astype(BF16))
    for h in range(len(heads)):
        qt = acc_scr[h]
        rot = jnp.concatenate([qt[dh // 2:], qt[:dh // 2]], axis=0)
        qrot_scr[h] = (qt * cos + rot * sin).astype(BF16)
    for g in range(ng):
        p_sum = jnp.zeros((LANES, tq), F32)
        for h in range(g * NSA_HPG, (g + 1) * NSA_HPG):
            sc = jnp.where(cmask, s_scr[h, :LANES, :], NEG_INF)
            e = jnp.exp2(sc - jnp.max(sc, axis=0, keepdims=True))
            p = jnp.where(cmask, e * (1.0 / jnp.sum(e, axis=0, keepdims=True)), 0.0)
            p_sum = p_sum + p
            p_scr[0, h, :LANES, :] = p.astype(BF16)
        vct = vc_ref[g].astype(F32).T.astype(BF16)
        for r in range(NSA_HPG):
            h = g * NSA_HPG + r
            out_scr[h] = gate(g, r, 0) * _dot(vct, p_scr[0, h, :LANES, :])

        per = SEL_BLOCK // CMP_STRIDE
        back = CMP_BLOCK // CMP_STRIDE - 1
        for t in range(tq // LANES):
            psum_scr[t] = p_sum[:, t * LANES:(t + 1) * LANES]

        def block_rows(first):
            return jnp.concatenate([psum_scr[t, pl.ds(first, n_sel, stride=per), :] for t in range(tq // LANES)],
                                   axis=1)

        imp = block_rows(0)
        for k in range(1, per):
            imp = imp + block_rows(k)
        for k in range(1, back + 1):
            before = block_rows(per - k)
            imp = imp + jnp.where(jrow >= 1, pltpu.roll(before, 1, 0), 0.0)
        imp = jnp.where(forced, FORCE_SCORE, jnp.where(causal, imp, -1.0))
        selb_scr[g] = imp

        def rank_body(i, rank, g=g, imp=imp):
            for k in range(rank_unroll):
                jp = i * rank_unroll + k
                other = selb_scr[g, pl.ds(jp, 1), :]
                tie = jnp.where(jrow > jp, 1.0, 0.0)
                rank = rank + jnp.where(other > imp, 1.0, jnp.where(other == imp, tie, 0.0))
            return rank

        rank = lax.fori_loop(0, n_rank_steps, rank_body, jnp.zeros((n_sel, tq), F32))
        keep = (rank < float(top_k)) & (imp >= 0.0)
        selb_scr[g] = jnp.where(keep, 0.0, NEG_INF)

    def reset(first_kt):
        m_scr[...] = jnp.full(m_scr.shape, NEG_INF, F32)
        l_scr[...] = jnp.zeros(l_scr.shape, F32)
        acc_scr[...] = jnp.zeros(acc_scr.shape, F32)
        alpha_scr[...] = jnp.ones(alpha_scr.shape, F32)
        p_scr[(first_kt - 1) & 1] = jnp.zeros(p_scr.shape[1:], BF16)

    def tile0(kt):
        return pl.multiple_of(jnp.maximum(kt, 0) * tk, tk)

    def apply_pending(vt_ref, kt):
        slot = kt & 1
        for g in range(ng):
            vt = vt_ref[g, :, pl.ds(tile0(kt), tk)]
            for h in range(g * NSA_HPG, (g + 1) * NSA_HPG):
                acc_scr[h] = alpha_scr[slot, h] * acc_scr[h] + _dot(vt, p_scr[slot, h])

    def flash_step(k_ref, vt_ref, kt, bias):
        slot = kt & 1
        for g in range(ng):
            k_tile = k_ref[g, pl.ds(tile0(kt), tk), :]
            for h in range(g * NSA_HPG, (g + 1) * NSA_HPG):
                s_scr[h] = _dot(k_tile, qrot_scr[h])
        apply_pending(vt_ref, kt - 1)

        for h, (g, _) in enumerate(heads):
            sc = s_scr[h]
            head_bias = bias[g] if isinstance(bias, list) else bias
            if head_bias is not None:
                sc = sc + head_bias
            m_old = m_scr[h]
            m_new = jnp.maximum(m_old, jnp.max(sc, axis=0, keepdims=True))
            alpha = jnp.exp2(m_old - m_new)
            p = jnp.exp2(sc - m_new)
            l_scr[h] = alpha * l_scr[h] + jnp.sum(p, axis=0, keepdims=True)
            alpha_scr[slot, h] = alpha
            p_scr[slot, h] = p.astype(BF16)
            m_scr[h] = m_new

    def finish(branch):
        for h, (g, r) in enumerate(heads):
            w = gate(g, r, branch) / l_scr[h]
            out_scr[h] = out_scr[h] + w * acc_scr[h]

    def sel_bias(kt, extra=None):
        per_tile = tk // SEL_BLOCK
        biases = []
        for g in range(ng):
            rows = [selb_scr[g, pl.ds(kt * per_tile + j, 1), :] for j in range(per_tile)]
            b = jnp.concatenate([jnp.broadcast_to(row, (SEL_BLOCK, tq)) for row in rows], axis=0)
            biases.append(b if extra is None else b + extra)
        return biases

    def key_pos(kt):
        return kt * tk + lax.broadcasted_iota(jnp.int32, (tk, 1), 0)

    causal_bias = jnp.where(key_pos(qi) <= pos_q, 0.0, NEG_INF)

    reset(0)

    def sel_body(kt, carry):
        flash_step(ks_ref, vst_ref, kt, sel_bias(kt))
        return carry

    lax.fori_loop(0, qi, sel_body, 0)
    flash_step(ks_ref, vst_ref, qi, sel_bias(qi, causal_bias))
    apply_pending(vst_ref, qi)
    finish(1)

    n_back = WINDOW // tk
    reset(jnp.maximum(qi - n_back, 0))
    for back in range(n_back, 0, -1):
        @pl.when(qi >= back)
        def _(back=back):
            bias = jnp.where(pos_q - key_pos(qi - back) < WINDOW, 0.0, NEG_INF) if back == n_back else None
            flash_step(kw_ref, vwt_ref, qi - back, bias)
    flash_step(kw_ref, vwt_ref, qi, causal_bias)
    apply_pending(vwt_ref, qi)
    finish(2)

    for h in range(len(heads)):
        o_ref[:, h * dh:(h + 1) * dh] = out_scr[h].T.astype(o_ref.dtype)


def _nsa_attn_call(main, gl, cos_t, sin_t, kc, vc, ks, vst, kw, vwt):
    bsz, s, _ = main.shape
    g, r, dh = NSA_KV_GROUPS, NSA_HPG, NSA_HEAD_DIM
    tq = tk = 256
    ng = 1
    nh = ng * r
    assert WINDOW % tk == 0 and tk % SEL_BLOCK == 0 and s % tq == 0 and g % ng == 0
    nrow = kc.shape[2]
    n_sel = s // SEL_BLOCK

    def kv(rows, cols):
        return pl.BlockSpec((None, ng, rows, cols), lambda b, gi, qi: (b, gi, 0, 0))

    return pl.pallas_call(
        functools.partial(_nsa_attn_kernel, tq=tq, tk=tk, n_sel=n_sel),
        out_shape=jax.ShapeDtypeStruct((bsz, s, NSA_Q_DIM), BF16),
        grid=(bsz, g // ng, s // tq),
        in_specs=[
            pl.BlockSpec((None, tq, nh * dh), lambda b, gi, qi: (b, qi, gi)),
            pl.BlockSpec((None, tq, ng * LANES), lambda b, gi, qi: (b, qi, gi)),
            pl.BlockSpec((dh, tq), lambda b, gi, qi: (0, qi)),
            pl.BlockSpec((dh, tq), lambda b, gi, qi: (0, qi)),
            kv(nrow, dh), kv(nrow, dh), kv(s, dh), kv(dh, s), kv(s, dh), kv(dh, s),
        ],
        out_specs=pl.BlockSpec((None, tq, nh * dh), lambda b, gi, qi: (b, qi, gi)),
        scratch_shapes=[
            pltpu.VMEM((nh, dh, tq), BF16),
            pltpu.VMEM((ng, n_sel, tq), F32),
            pltpu.VMEM((nh, 1, tq), F32),
            pltpu.VMEM((nh, 1, tq), F32),
            pltpu.VMEM((2, nh, 1, tq), F32),
            pltpu.VMEM((nh, dh, tq), F32),
            pltpu.VMEM((nh, dh, tq), F32),
            pltpu.VMEM((nh, tk, tq), F32),
            pltpu.VMEM((2, nh, tk, tq), BF16),
            pltpu.VMEM((tq // LANES, LANES, LANES), F32),
        ],
        compiler_params=_cparams(("parallel", "parallel", "arbitrary")),
        name="nsa_attn",
    )(main, gl, cos_t, sin_t, kc, vc, ks, vst, kw, vwt)


def _nsa_tables(s):
    half = NSA_HEAD_DIM // 2
    freqs = ROPE_THETA ** (-jnp.arange(half, dtype=F32) / half)
    ang = jnp.arange(s).astype(F32)[:, None] * freqs[None, :]
    cos, sin = jnp.cos(ang), jnp.sin(ang)
    cos_full = jnp.concatenate([cos, cos], axis=-1)
    sin_signed = jnp.concatenate([-sin, sin], axis=-1)
    return cos_full, sin_signed, cos_full.T, sin_signed.T


def _nsa_layer(x, mod, ng, w_in, cmp_pos, cmp_w1, cmp_w2, w_out, tables, *, layer):
    cos, sin, cos_t, sin_t = tables
    g, r = NSA_KV_GROUPS, NSA_HPG
    w_gate = w_in[layer, :, NSA_MAIN_DIM:].reshape(D_MODEL, g, 3 * r)
    w_gate = jnp.pad(w_gate, ((0, 0), (0, 0), (0, LANES - 3 * r))).reshape(D_MODEL, g * LANES)
    main, gl = _proj_call(x, mod, ng, w_in, w_gate, layer=layer, n=NSA_MAIN_DIM, mrow=3, grow=2)
    kc, vc, ks, vst, kw, vwt = _nsa_prep_call(main, cmp_pos, cmp_w1, cmp_w2, cos, sin, layer=layer)
    o = _nsa_attn_call(main, gl, cos_t, sin_t, kc, vc, ks, vst, kw, vwt)
    return _oproj_call(o, w_out, x, mod, ng, layer=layer, mrow=5, grow=3)


def _mlstm_kernel(q_ref, k_ref, v_ref, og_ref, gi_ref, gf_ref, bi_ref, bf_ref, cwq_ref, cwk_ref, cbq_ref, cbk_ref,
                  gain_ref, o_ref, qpad_scr, kpad_scr, ct_scr, rowi_scr, chunk_scr, bcol_scr, ecol_scr):
    s, dk = q_ref.shape
    dv = v_ref.shape[1]
    chunk = ML_CHUNK
    nc = s // chunk
    pad = qpad_scr.shape[0] - s

    for src, dst in ((q_ref, qpad_scr), (k_ref, kpad_scr)):
        dst[:pad] = jnp.zeros((pad, dk), F32)
        dst[pad:] = src[...]

    def conv_silu(pad_ref, w_ref, b_ref, r0):
        win = pad_ref[pl.ds(r0, chunk + pad), :]
        acc = win[pad:] * w_ref[ML_CONV - 1:ML_CONV, :] + b_ref[...]
        for d in range(1, ML_CONV):
            acc = acc + pltpu.roll(win, d, 0)[pad:] * w_ref[ML_CONV - 1 - d:ML_CONV - d, :]
        return _silu(acc)

    ig = gi_ref[...] + bi_ref[...]
    fg = gf_ref[...] + bf_ref[...]
    logf = jnp.minimum(fg, 0.0) - jnp.log1p(jnp.exp(-jnp.abs(fg)))
    tri_r = lax.broadcasted_iota(jnp.int32, (chunk, chunk), 0)
    tri_c = lax.broadcasted_iota(jnp.int32, (chunk, chunk), 1)
    upper = jnp.where(tri_r <= tri_c, 1.0, 0.0)
    b = jnp.dot(logf, upper, precision=HIGHEST, preferred_element_type=F32)
    w_end = b[:, chunk - 1:chunk] - b + ig
    m_loc = jnp.max(w_end, axis=-1, keepdims=True)
    e_end = jnp.exp(w_end - m_loc)
    rowi_scr[...] = ig - b
    b_tot = b[:, chunk - 1:chunk]
    m_run = jnp.zeros((1, 1), F32)
    m_before, m_after = [], []
    for c in range(nc):
        m_before.append(m_run)
        m_run = jnp.maximum(b_tot[c:c + 1, :] + m_run, m_loc[c:c + 1, :])
        m_after.append(m_run)
    m_prev = jnp.concatenate(m_before, axis=0)
    m_next = jnp.concatenate(m_after, axis=0)
    for i, val in enumerate((m_prev, jnp.exp(b_tot + m_prev - m_next), jnp.exp(m_loc - m_next))):
        chunk_scr[i] = jnp.broadcast_to(val, (nc, LANES))
    eye = jnp.where(tri_r == tri_c, 1.0, 0.0)
    cols = lax.dot_general(eye, jnp.concatenate([b, e_end], axis=0), NT_DIMS, precision=HIGHEST,
                           preferred_element_type=F32)
    for c in range(nc):
        bcol_scr[c] = jnp.broadcast_to(cols[:, c:c + 1], (chunk, LANES))
        ecol_scr[c] = jnp.broadcast_to(cols[:, nc + c:nc + c + 1], (chunk, LANES))

    ct_scr[...] = jnp.zeros(ct_scr.shape, F32)
    lower = tri_c <= tri_r
    gain = gain_ref[...]

    n_state_tiles = ct_scr.shape[1] // LANES

    def local_part(c):
        r0 = pl.multiple_of(c * chunk, chunk)
        q = conv_silu(qpad_scr, cwq_ref, cbq_ref, r0).astype(BF16)
        k = conv_silu(kpad_scr, cwk_ref, cbk_ref, r0) * (dk ** -0.5)
        va = jnp.concatenate([v_ref[pl.ds(r0, chunk), :].astype(BF16), jnp.ones((chunk, LANES), BF16)], axis=1)
        m_prev = chunk_scr[0, pl.ds(c, 1), :]
        bcol = bcol_scr[c]
        log_intra = jnp.where(lower, bcol[:, :chunk] + rowi_scr[pl.ds(c, 1), :], NEG_INF)
        log_inter = bcol[:, :1] + m_prev[:, :1]
        m_t = jnp.maximum(log_inter, jnp.max(log_intra, axis=-1, keepdims=True))
        qk = _dot_nt(q, k.astype(BF16)) * jnp.exp(log_intra - m_t)
        intra = _dot(qk.astype(BF16), va)
        c_loc = lax.dot_general((k * ecol_scr[c]).astype(BF16), va, TN_DIMS, preferred_element_type=F32)
        return q, intra, c_loc, jnp.exp(log_inter - m_t), jnp.exp(-m_t)

    def state_part(c, q, intra, c_loc, e_inter, floor):
        r0 = pl.multiple_of(c * chunk, chunk)
        decay, inject = (chunk_scr[i, pl.ds(c, 1), :] for i in (1, 2))
        ct = ct_scr[...]
        tot = intra + e_inter * _dot(q, ct.astype(BF16))
        h = tot[:, :dv] / jnp.maximum(jnp.abs(tot[:, dv:dv + 1]), floor)
        hn = _rms(h) * gain
        o_ref[pl.ds(r0, chunk), :] = (jax.nn.sigmoid(og_ref[pl.ds(r0, chunk), :]) * hn).astype(o_ref.dtype)
        ct_scr[...] = (jnp.concatenate([decay] * n_state_tiles, axis=1) * ct
                       + jnp.concatenate([inject] * n_state_tiles, axis=1) * c_loc)

    group = 4

    def body(i, carry):
        parts = [local_part(i * group + j) for j in range(group)]
        for j in range(group):
            state_part(i * group + j, *parts[j])
        return carry

    lax.fori_loop(0, nc // group, body, 0)


def _mlstm_call(main, gates_t, gate_b, conv_w, conv_b, gain):
    bsz, s, _ = main.shape
    h, dk, dv = ML_HEADS, ML_QK_DIM, ML_V_DIM
    chunk = ML_CHUNK
    nc = s // chunk
    vblk0 = ML_QK_COLS // dv

    def gate(off):
        return pl.BlockSpec((None, None, nc, chunk), lambda b, hi: (b, off + hi, 0, 0))

    def bias(off):
        return pl.BlockSpec((None, 1, 1), lambda b, hi: (off + hi, 0, 0))

    return pl.pallas_call(
        _mlstm_kernel,
        out_shape=jax.ShapeDtypeStruct((bsz, s, h * dv), BF16),
        grid=(bsz, h),
        in_specs=[
            pl.BlockSpec((None, s, dk), lambda b, hi: (b, 0, hi)),
            pl.BlockSpec((None, s, dk), lambda b, hi: (b, 0, h + hi)),
            pl.BlockSpec((None, s, dv), lambda b, hi: (b, 0, vblk0 + hi)),
            pl.BlockSpec((None, s, dv), lambda b, hi: (b, 0, vblk0 + h + hi)),
            gate(0), gate(h), bias(0), bias(h),
            pl.BlockSpec((ML_CONV, dk), lambda b, hi: (0, hi)),
            pl.BlockSpec((ML_CONV, dk), lambda b, hi: (0, h + hi)),
            pl.BlockSpec((1, dk), lambda b, hi: (0, hi)),
            pl.BlockSpec((1, dk), lambda b, hi: (0, h + hi)),
            pl.BlockSpec((1, dv), lambda b, hi: (0, hi)),
        ],
        out_specs=pl.BlockSpec((None, s, dv), lambda b, hi: (b, 0, hi)),
        scratch_shapes=[
            pltpu.VMEM((s + SUBLANES, dk), F32),
            pltpu.VMEM((s + SUBLANES, dk), F32),
            pltpu.VMEM((dk, dv + LANES), F32),
            pltpu.VMEM((nc, chunk), F32),
            pltpu.VMEM((3, nc, LANES), F32),
            pltpu.VMEM((nc, chunk, LANES), F32),
            pltpu.VMEM((nc, chunk, LANES), F32),
        ],
        compiler_params=_cparams(("parallel", "parallel")),
        name="mlstm",
    )(main, main, main, main, gates_t, gates_t, gate_b, gate_b, conv_w, conv_w, conv_b, conv_b, gain)


def _mlstm_layer(x, mod, ng, w_in, conv_w, conv_b, gate_b, mh_gain, w_out, *, layer):
    bsz, s, _ = x.shape
    ngate = 2 * ML_HEADS
    w_gate = jnp.pad(w_in[layer, :, ML_MAIN_DIM:], ((0, 0), (0, LANES - ngate)))
    main, gl = _proj_call(x, mod, ng, w_in, w_gate, layer=layer, n=ML_MAIN_DIM, mrow=3, grow=2)
    gates_t = jnp.swapaxes(gl[:, :, :ngate], 1, 2).reshape(bsz, ngate, s // ML_CHUNK, ML_CHUNK)
    o = _mlstm_call(main, gates_t, gate_b.reshape(ngate, 1, 1), conv_w, conv_b.reshape(1, -1),
                    mh_gain.reshape(1, -1))
    return _oproj_call(o, w_out, x, mod, ng, layer=layer, mrow=5, grow=3)


def kernel(x, c, mod_w, mod_b, norm_g, ffn_pre_w_in, ffn_pre_w_out, ffn_post_w_in, ffn_post_w_out, nsa_w_in, nsa_cmp_pos, nsa_cmp_w1, nsa_cmp_w2, nsa_w_out, ml_w_in, ml_conv_w, ml_conv_b, ml_gate_b, ml_mh_gain, ml_w_out):
    bsz, s, d = x.shape
    depth = mod_w.shape[0]
    mods = _mod_call(c, mod_w, mod_b).reshape(depth, bsz, N_MOD, d)
    tables = _nsa_tables(s)
    pre_in, pre_out = ffn_pre_w_in.astype(BF16), ffn_pre_w_out.astype(BF16)
    post_in, post_out = ffn_post_w_in.astype(BF16), ffn_post_w_out.astype(BF16)
    nsa_in, nsa_out = nsa_w_in.astype(BF16), nsa_w_out.astype(BF16)
    nsa_w1, nsa_w2 = nsa_cmp_w1.astype(BF16), nsa_cmp_w2.astype(BF16)
    ml_in, ml_out = ml_w_in.astype(BF16), ml_w_out.astype(BF16)
    for i in range(depth):
        mod, ng = mods[i], norm_g[i]
        x = _ffn_call(x, mod, ng, pre_in, pre_out, layer=i, mrow=0, grow=0)
        j = i // 2
        if i % 2 == 0:
            x = _nsa_layer(x, mod, ng, nsa_in, nsa_cmp_pos, nsa_w1, nsa_w2, nsa_out, tables, layer=j)
        else:
            x = _mlstm_layer(x, mod, ng, ml_in, ml_conv_w[j], ml_conv_b[j], ml_gate_b[j], ml_mh_gain[j], ml_out,
                             layer=j)
        x = _ffn_call(x, mod, ng, post_in, post_out, layer=i, mrow=6, grow=4)
    return x
```

```python
import functools

import jax
import jax.numpy as jnp
from jax import lax
from jax.experimental import pallas as pl
from jax.experimental.pallas import tpu as pltpu

F32 = jnp.float32
BF16 = jnp.bfloat16
HIGHEST = lax.Precision.HIGHEST

D_MODEL = 2048
DEPTH = 4
D_FF = 5632
FFN_RES = 0.5
N_MOD = 9
RMS_EPS = 1e-6
NEG_INF = -1e30
LOG2_E = 1.4426950408889634

NSA_HEADS = 16
NSA_HEAD_DIM = 128
NSA_KV_GROUPS = 4
NSA_HPG = NSA_HEADS // NSA_KV_GROUPS
NSA_Q_DIM = NSA_HEADS * NSA_HEAD_DIM
NSA_KV_DIM = NSA_KV_GROUPS * NSA_HEAD_DIM
NSA_MAIN_DIM = NSA_Q_DIM + 6 * NSA_KV_DIM
CMP_BLOCK = 32
CMP_STRIDE = 16
CMP_HIDDEN = 512
SEL_BLOCK = 64
SEL_TOPK = 16
N_LOCAL_BLOCKS = 2
FORCE_SCORE = 1e9
WINDOW = 512
ROPE_THETA = 10000.0

ML_HEADS = 8
ML_QK_DIM = 128
ML_V_DIM = 256
ML_CONV = 4
ML_CHUNK = 64
ML_QK_COLS = 2 * ML_HEADS * ML_QK_DIM
ML_MAIN_DIM = ML_QK_COLS + 2 * ML_HEADS * ML_V_DIM

LANES = 128
SUBLANES = 8
VMEM_LIMIT = 56 * 1024 * 1024

NT_DIMS = (((1,), (1,)), ((), ()))
TN_DIMS = (((0,), (0,)), ((), ()))


def _cparams(sem):
    return pltpu.CompilerParams(dimension_semantics=sem, vmem_limit_bytes=VMEM_LIMIT)


def _silu(x):
    return x * jax.nn.sigmoid(x)


def _rms(x):
    return x * lax.rsqrt(jnp.mean(x * x, axis=-1, keepdims=True) + RMS_EPS)


def _dot(a, b):
    return jnp.dot(a, b, preferred_element_type=F32)


def _dot_nt(a, b):
    return lax.dot_general(a, b, NT_DIMS, preferred_element_type=F32)


def _mod_kernel(c_ref, w_ref, b_ref, o_ref):
    ca = _silu(c_ref[...]).astype(BF16)
    o_ref[...] = _dot(ca, w_ref[...].astype(BF16)) + b_ref[...]


def _mod_call(c, mod_w, mod_b):
    depth, d, n = mod_w.shape
    bsz = c.shape[0]
    tn = 1024
    return pl.pallas_call(
        _mod_kernel,
        out_shape=jax.ShapeDtypeStruct((depth, bsz, n), F32),
        grid=(depth, n // tn),
        in_specs=[
            pl.BlockSpec((bsz, d), lambda i, j: (0, 0)),
            pl.BlockSpec((None, d, tn), lambda i, j: (i, 0, j)),
            pl.BlockSpec((None, 1, tn), lambda i, j: (i, 0, j)),
        ],
        out_specs=pl.BlockSpec((None, bsz, tn), lambda i, j: (i, 0, j)),
        compiler_params=_cparams(("parallel", "parallel")),
        name="mod",
    )(c, mod_w, mod_b.reshape(depth, 1, n))


def _prenorm(x, mod_ref, ng_ref, mrow, grow):
    gain = ng_ref[grow:grow + 1, :] * (1.0 + mod_ref[mrow + 1:mrow + 2, :])
    return _rms(x) * gain + mod_ref[mrow:mrow + 1, :]


ROW_CHUNK = 16
ROW_UNROLL = 4


def _inv_rms(x):
    return lax.rsqrt(jnp.mean(x * x, axis=-1, keepdims=True) + RMS_EPS)


def _row_loop(n_rows, stats, apply):
    def step(i, carry):
        slices = [pl.ds(pl.multiple_of((i * ROW_UNROLL + u) * ROW_CHUNK, ROW_CHUNK), ROW_CHUNK)
                  for u in range(ROW_UNROLL)]
        stat = [stats(rows) for rows in slices]
        for rows, st in zip(slices, stat):
            apply(rows, st)
        return carry
    lax.fori_loop(0, n_rows // (ROW_CHUNK * ROW_UNROLL), step, 0)


def _ffn_kernel(x_ref, mod_ref, ng_ref, wg_ref, wu_ref, wo_ref, o_ref, xn_ref, acc_ref, *, mrow, grow):
    j = pl.program_id(2)

    @pl.when(j == 0)
    def _():
        gain = ng_ref[grow:grow + 1, :] * (1.0 + mod_ref[mrow + 1:mrow + 2, :])
        shift = mod_ref[mrow:mrow + 1, :]

        def prenorm_rows(rows, inv):
            xn_ref[rows, :] = (x_ref[rows, :] * inv * gain + shift).astype(BF16)

        _row_loop(x_ref.shape[0], lambda rows: _inv_rms(x_ref[rows, :]), prenorm_rows)

    def partial_out():
        xn = xn_ref[...]
        g = _dot(xn, wg_ref[...])
        u = _dot(xn, wu_ref[...])
        return _dot((_silu(g) * u).astype(BF16), wo_ref[...])

    @pl.when(j == 0)
    def _():
        acc_ref[...] = partial_out()

    @pl.when(j > 0)
    def _():
        acc_ref[...] += partial_out()

    @pl.when(j == pl.num_programs(2) - 1)
    def _():
        gain = ng_ref[grow + 1:grow + 2, :] * (FFN_RES * mod_ref[mrow + 2:mrow + 3, :])

        def finish_rows(rows, inv):
            o_ref[rows, :] = x_ref[rows, :] + acc_ref[rows, :] * inv * gain

        _row_loop(x_ref.shape[0], lambda rows: _inv_rms(acc_ref[rows, :]), finish_rows)


def _ffn_call(x, mod, ng, w_in, w_out, *, layer, mrow, grow):
    bsz, s, d = x.shape
    dff = w_out.shape[1]
    tm, tf = 512, 512
    nf = dff // tf
    return pl.pallas_call(
        functools.partial(_ffn_kernel, mrow=mrow, grow=grow),
        out_shape=jax.ShapeDtypeStruct(x.shape, F32),
        grid=(bsz, s // tm, nf),
        in_specs=[
            pl.BlockSpec((None, tm, d), lambda b, m, j: (b, m, 0)),
            pl.BlockSpec((None, N_MOD, d), lambda b, m, j: (b, 0, 0)),
            pl.BlockSpec(ng.shape, lambda b, m, j: (0, 0)),
            pl.BlockSpec((None, d, tf), lambda b, m, j: (layer, 0, j)),
            pl.BlockSpec((None, d, tf), lambda b, m, j: (layer, 0, nf + j)),
            pl.BlockSpec((None, tf, d), lambda b, m, j: (layer, j, 0)),
        ],
        out_specs=pl.BlockSpec((None, tm, d), lambda b, m, j: (b, m, 0)),
        scratch_shapes=[pltpu.VMEM((tm, d), BF16), pltpu.VMEM((tm, d), F32)],
        compiler_params=_cparams(("parallel", "parallel", "arbitrary")),
        name="ffn",
    )(x, mod, ng, w_in, w_in, w_out)


def _proj_kernel(x_ref, mod_ref, ng_ref, w_ref, wgate_ref, o_ref, og_ref, xn_ref, *, mrow, grow):
    @pl.when(pl.program_id(2) == 0)
    def _():
        xn = _prenorm(x_ref[...], mod_ref, ng_ref, mrow, grow).astype(BF16)
        xn_ref[...] = xn
        og_ref[...] = _dot(xn, wgate_ref[...])

    o_ref[...] = _dot(xn_ref[...], w_ref[...])


def _proj_call(x, mod, ng, w_in, w_gate, *, layer, n, mrow, grow):
    bsz, s, d = x.shape
    ngate = w_gate.shape[1]
    tm, tn = 1024, 1024
    return pl.pallas_call(
        functools.partial(_proj_kernel, mrow=mrow, grow=grow),
        out_shape=(jax.ShapeDtypeStruct((bsz, s, n), F32), jax.ShapeDtypeStruct((bsz, s, ngate), F32)),
        grid=(bsz, s // tm, n // tn),
        in_specs=[
            pl.BlockSpec((None, tm, d), lambda b, m, j: (b, m, 0)),
            pl.BlockSpec((None, N_MOD, d), lambda b, m, j: (b, 0, 0)),
            pl.BlockSpec(ng.shape, lambda b, m, j: (0, 0)),
            pl.BlockSpec((None, d, tn), lambda b, m, j: (layer, 0, j)),
            pl.BlockSpec((d, ngate), lambda b, m, j: (0, 0)),
        ],
        out_specs=(pl.BlockSpec((None, tm, tn), lambda b, m, j: (b, m, j)),
                   pl.BlockSpec((None, tm, ngate), lambda b, m, j: (b, m, 0))),
        scratch_shapes=[pltpu.VMEM((tm, d), BF16)],
        compiler_params=_cparams(("parallel", "parallel", "arbitrary")),
        name="proj",
    )(x, mod, ng, w_in, w_gate)


def _oproj_kernel(a_ref, w_ref, x_ref, mod_ref, ng_ref, o_ref, *, mrow, grow):
    yn = _rms(_dot(a_ref[...], w_ref[...])) * ng_ref[grow:grow + 1, :]
    o_ref[...] = x_ref[...] + mod_ref[mrow:mrow + 1, :] * yn


def _oproj_call(a, w, x, mod, ng, *, layer, mrow, grow):
    bsz, s, d = x.shape
    k = a.shape[-1]
    tm = 512
    return pl.pallas_call(
        functools.partial(_oproj_kernel, mrow=mrow, grow=grow),
        out_shape=jax.ShapeDtypeStruct(x.shape, F32),
        grid=(bsz, s // tm),
        in_specs=[
            pl.BlockSpec((None, tm, k), lambda b, m: (b, m, 0)),
            pl.BlockSpec((None, k, d), lambda b, m: (layer, 0, 0)),
            pl.BlockSpec((None, tm, d), lambda b, m: (b, m, 0)),
            pl.BlockSpec((None, N_MOD, d), lambda b, m: (b, 0, 0)),
            pl.BlockSpec(ng.shape, lambda b, m: (0, 0)),
        ],
        out_specs=pl.BlockSpec((None, tm, d), lambda b, m: (b, m, 0)),
        compiler_params=_cparams(("parallel", "parallel")),
        name="oproj",
    )(a, w, x, mod, ng)


def _rope(x, cos, sin_signed):
    return x * cos + pltpu.roll(x, NSA_HEAD_DIM // 2, 1) * sin_signed


def _nsa_prep_kernel(kc_ref, vc_ref, ks_ref, vs_ref, kw_ref, vw_ref, pe_ref, w1_ref, w2_ref, cos_ref, sin_ref,
                     kco_ref, vco_ref, kso_ref, vso_ref, kwo_ref, vwo_ref):
    s = kc_ref.shape[0]
    nrow = s // CMP_STRIDE
    half = CMP_BLOCK // CMP_STRIDE

    def compress(t_ref, idx):
        streams = [t_ref[pl.ds(j, nrow, stride=CMP_STRIDE), :] for j in range(CMP_STRIDE)]
        hidden = None
        for h in range(half):
            flat = jnp.concatenate(
                [(streams[j] + pe_ref[idx, h * CMP_STRIDE + j:h * CMP_STRIDE + j + 1, :]).astype(BF16)
                 for j in range(CMP_STRIDE)], axis=1)
            w = w1_ref[idx, h * CMP_STRIDE * NSA_HEAD_DIM:(h + 1) * CMP_STRIDE * NSA_HEAD_DIM, :]
            part = _dot(flat, w)
            if h:
                part = pltpu.roll(part, nrow - h, 0)
            hidden = part if hidden is None else hidden + part
        return _dot(_silu(hidden).astype(BF16), w2_ref[idx])

    kco_ref[...] = compress(kc_ref, 0).astype(kco_ref.dtype)
    vco_ref[...] = compress(vc_ref, 1).astype(vco_ref.dtype)
    cos = cos_ref[...]
    sin = sin_ref[...]
    kso_ref[...] = _rope(ks_ref[...], cos, sin).astype(BF16)
    kwo_ref[...] = _rope(kw_ref[...], cos, sin).astype(BF16)
    vso_ref[...] = vs_ref[...].T.astype(BF16)
    vwo_ref[...] = vw_ref[...].T.astype(BF16)


def _nsa_prep_call(main, pe, w1, w2, cos, sin, *, layer):
    bsz, s, _ = main.shape
    g, dh = NSA_KV_GROUPS, NSA_HEAD_DIM
    nrow = s // CMP_STRIDE
    q_blocks = NSA_Q_DIM // dh

    def col(k):
        return pl.BlockSpec((None, s, dh), lambda b, gi, k=k: (b, 0, q_blocks + k * g + gi))

    def full(a):
        return pl.BlockSpec(a.shape, lambda b, gi, nd=a.ndim: (0,) * nd)

    def stacked(a):
        return pl.BlockSpec((None,) + a.shape[1:], lambda b, gi, nd=a.ndim: (layer,) + (0,) * (nd - 1))

    def out(rows, cols):
        return (jax.ShapeDtypeStruct((bsz, g, rows, cols), BF16),
                pl.BlockSpec((None, None, rows, cols), lambda b, gi: (b, gi, 0, 0)))

    outs = [out(nrow, dh), out(nrow, dh), out(s, dh), out(dh, s), out(s, dh), out(dh, s)]
    return pl.pallas_call(
        _nsa_prep_kernel,
        out_shape=tuple(o[0] for o in outs),
        grid=(bsz, g),
        in_specs=[col(0), col(1), col(2), col(3), col(4), col(5), stacked(pe), stacked(w1), stacked(w2),
                  full(cos), full(sin)],
        out_specs=tuple(o[1] for o in outs),
        compiler_params=_cparams(("parallel", "parallel")),
        name="nsa_prep",
    )(main, main, main, main, main, main, pe, w1, w2, cos, sin)


def _nsa_attn_kernel(q_ref, gl_ref, cos_ref, sin_ref, kc_ref, vc_ref, ks_ref, vst_ref, kw_ref, vwt_ref,
                     o_ref, qrot_scr, selb_scr, m_scr, l_scr, alpha_scr, acc_scr, out_scr, s_scr, p_scr, psum_scr,
                     *, tq, tk, n_sel):
    qi = pl.program_id(2)
    dh = NSA_HEAD_DIM
    ng = kc_ref.shape[0]
    heads = [(g, r) for g in range(ng) for r in range(NSA_HPG)]
    scale = dh ** -0.5 * LOG2_E
    pos_q = qi * tq + lax.broadcasted_iota(jnp.int32, (1, tq), 1)
    cos = cos_ref[...]
    sin = sin_ref[...]
    gl_t = gl_ref[...].T
    gates = [jax.nn.sigmoid(gl_t[g * LANES:g * LANES + 4 * NSA_HPG, :]) for g in range(ng)]

    def gate(g, r, branch):
        return gates[g][3 * r + branch:3 * r + branch + 1, :]

    crow = lax.broadcasted_iota(jnp.int32, (LANES, 1), 0)
    cmask = crow * CMP_STRIDE + (CMP_BLOCK - 1) <= pos_q
    jrow = lax.broadcasted_iota(jnp.int32, (n_sel, 1), 0)
    q_blk = pos_q // SEL_BLOCK
    causal = jrow <= q_blk
    forced = (jrow == 0) | (causal & (jrow > q_blk - N_LOCAL_BLOCKS))
    top_k = min(SEL_TOPK, n_sel)
    rank_unroll = 4
    n_causal = (qi * tq + tq) // SEL_BLOCK
    n_rank_steps = jnp.where(n_causal <= top_k, 0, n_causal // rank_unroll)
    for h in range(len(heads)):
        acc_scr[h] = q_ref[:, h * dh:(h + 1) * dh].T * scale
    for h, (g, _) in enumerate(heads):
        s_scr[h, :LANES, :] = _dot(kc_ref[g], acc_scr[h].astype(BF16))
    for h in range(len(heads)):
        qt = acc_scr[h]
        rot = jnp.concatenate([qt[dh // 2:], qt[:dh // 2]], axis=0)
        qrot_scr[h] = (qt * cos + rot * sin).astype(BF16)
    for g in range(ng):
        p_sum = jnp.zeros((LANES, tq), F32)
        for h in range(g * NSA_HPG, (g + 1) * NSA_HPG):
            sc = jnp.where(cmask, s_scr[h, :LANES, :], NEG_INF)
            e = jnp.exp2(sc - jnp.max(sc, axis=0, keepdims=True))
            p = jnp.where(cmask, e * (1.0 / jnp.sum(e, axis=0, keepdims=True)), 0.0)
            p_sum = p_sum + p
            p_scr[0, h, :LANES, :] = p.astype(BF16)
        vct = vc_ref[g].astype(F32).T.astype(BF16)
        for r in range(NSA_HPG):
            h = g * NSA_HPG + r
            out_scr[h] = gate(g, r, 0) * _dot(vct, p_scr[0, h, :LANES, :])

        per = SEL_BLOCK // CMP_STRIDE
        back = CMP_BLOCK // CMP_STRIDE - 1
        for t in range(tq // LANES):
            psum_scr[t] = p_sum[:, t * LANES:(t + 1) * LANES]

        def block_rows(first):
            return jnp.concatenate([psum_scr[t, pl.ds(first, n_sel, stride=per), :] for t in range(tq // LANES)],
                                   axis=1)

        imp = block_rows(0)
        for k in range(1, per):
            imp = imp + block_rows(k)
        for k in range(1, back + 1):
            before = block_rows(per - k)
            imp = imp + jnp.where(jrow >= 1, pltpu.roll(before, 1, 0), 0.0)
        imp = jnp.where(forced, FORCE_SCORE, jnp.where(causal, imp, -1.0))
        selb_scr[g] = imp

        def rank_body(i, rank, g=g, imp=imp):
            for k in range(rank_unroll):
                jp = i * rank_unroll + k
                other = selb_scr[g, pl.ds(jp, 1), :]
                tie = jnp.where(jrow > jp, 1.0, 0.0)
                rank = rank + jnp.where(other > imp, 1.0, jnp.where(other == imp, tie, 0.0))
            return rank

        rank = lax.fori_loop(0, n_rank_steps, rank_body, jnp.zeros((n_sel, tq), F32))
        keep = (rank < float(top_k)) & (imp >= 0.0)
        selb_scr[g] = jnp.where(keep, 0.0, NEG_INF)

    def reset(first_kt):
        m_scr[...] = jnp.full(m_scr.shape, NEG_INF, F32)
        l_scr[...] = jnp.zeros(l_scr.shape, F32)
        acc_scr[...] = jnp.zeros(acc_scr.shape, F32)
        alpha_scr[...] = jnp.ones(alpha_scr.shape, F32)
        p_scr[(first_kt - 1) & 1] = jnp.zeros(p_scr.shape[1:], BF16)

    def tile0(kt):
        return pl.multiple_of(jnp.maximum(kt, 0) * tk, tk)

    def apply_pending(vt_ref, kt):
        slot = kt & 1
        for g in range(ng):
            vt = vt_ref[g, :, pl.ds(tile0(kt), tk)]
            for h in range(g * NSA_HPG, (g + 1) * NSA_HPG):
                acc_scr[h] = alpha_scr[slot, h] * acc_scr[h] + _dot(vt, p_scr[slot, h])

    def flash_step(k_ref, vt_ref, kt, bias):
        slot = kt & 1
        for g in range(ng):
            k_tile = k_ref[g, pl.ds(tile0(kt), tk), :]
            for h in range(g * NSA_HPG, (g + 1) * NSA_HPG):
                s_scr[h] = _dot(k_tile, qrot_scr[h])
        apply_pending(vt_ref, kt - 1)

        for h, (g, _) in enumerate(heads):
            sc = s_scr[h]
            head_bias = bias[g] if isinstance(bias, list) else bias
            if head_bias is not None:
                sc = sc + head_bias
            m_old = m_scr[h]
            m_new = jnp.maximum(m_old, jnp.max(sc, axis=0, keepdims=True))
            alpha = jnp.exp2(m_old - m_new)
            p = jnp.exp2(sc - m_new)
            l_scr[h] = alpha * l_scr[h] + jnp.sum(p, axis=0, keepdims=True)
            alpha_scr[slot, h] = alpha
            p_scr[slot, h] = p.astype(BF16)
            m_scr[h] = m_new

    def finish(branch):
        for h, (g, r) in enumerate(heads):
            w = gate(g, r, branch) / l_scr[h]
            out_scr[h] = out_scr[h] + w * acc_scr[h]

    def sel_bias(kt, extra=None):
        per_tile = tk // SEL_BLOCK
        biases = []
        for g in range(ng):
            rows = [selb_scr[g, pl.ds(kt * per_tile + j, 1), :] for j in range(per_tile)]
            b = jnp.concatenate([jnp.broadcast_to(row, (SEL_BLOCK, tq)) for row in rows], axis=0)
            biases.append(b if extra is None else b + extra)
        return biases

    def key_pos(kt):
        return kt * tk + lax.broadcasted_iota(jnp.int32, (tk, 1), 0)

    causal_bias = jnp.where(key_pos(qi) <= pos_q, 0.0, NEG_INF)

    reset(0)

    def sel_body(kt, carry):
        flash_step(ks_ref, vst_ref, kt, sel_bias(kt))
        return carry

    lax.fori_loop(0, qi, sel_body, 0)
    flash_step(ks_ref, vst_ref, qi, sel_bias(qi, causal_bias))
    apply_pending(vst_ref, qi)
    finish(1)

    n_back = WINDOW // tk
    reset(jnp.maximum(qi - n_back, 0))
    for back in range(n_back, 0, -1):
        @pl.when(qi >= back)
        def _(back=back):
            bias = jnp.where(pos_q - key_pos(qi - back) < WINDOW, 0.0, NEG_INF) if back == n_back else None
            flash_step(kw_ref, vwt_ref, qi - back, bias)
    flash_step(kw_ref, vwt_ref, qi, causal_bias)
    apply_pending(vwt_ref, qi)
    finish(2)

    for h in range(len(heads)):
        o_ref[:, h * dh:(h + 1) * dh] = out_scr[h].T.astype(o_ref.dtype)


def _nsa_attn_call(main, gl, cos_t, sin_t, kc, vc, ks, vst, kw, vwt):
    bsz, s, _ = main.shape
    g, r, dh = NSA_KV_GROUPS, NSA_HPG, NSA_HEAD_DIM
    tq = tk = 256
    ng = 1
    nh = ng * r
    assert WINDOW % tk == 0 and tk % SEL_BLOCK == 0 and s % tq == 0 and g % ng == 0
    nrow = kc.shape[2]
    n_sel = s // SEL_BLOCK

    def kv(rows, cols):
        return pl.BlockSpec((None, ng, rows, cols), lambda b, gi, qi: (b, gi, 0, 0))

    return pl.pallas_call(
        functools.partial(_nsa_attn_kernel, tq=tq, tk=tk, n_sel=n_sel),
        out_shape=jax.ShapeDtypeStruct((bsz, s, NSA_Q_DIM), BF16),
        grid=(bsz, g // ng, s // tq),
        in_specs=[
            pl.BlockSpec((None, tq, nh * dh), lambda b, gi, qi: (b, qi, gi)),
            pl.BlockSpec((None, tq, ng * LANES), lambda b, gi, qi: (b, qi, gi)),
            pl.BlockSpec((dh, tq), lambda b, gi, qi: (0, qi)),
            pl.BlockSpec((dh, tq), lambda b, gi, qi: (0, qi)),
            kv(nrow, dh), kv(nrow, dh), kv(s, dh), kv(dh, s), kv(s, dh), kv(dh, s),
        ],
        out_specs=pl.BlockSpec((None, tq, nh * dh), lambda b, gi, qi: (b, qi, gi)),
        scratch_shapes=[
            pltpu.VMEM((nh, dh, tq), BF16),
            pltpu.VMEM((ng, n_sel, tq), F32),
            pltpu.VMEM((nh, 1, tq), F32),
            pltpu.VMEM((nh, 1, tq), F32),
            pltpu.VMEM((2, nh, 1, tq), F32),
            pltpu.VMEM((nh, dh, tq), F32),
            pltpu.VMEM((nh, dh, tq), F32),
            pltpu.VMEM((nh, tk, tq), F32),
            pltpu.VMEM((2, nh, tk, tq), BF16),
            pltpu.VMEM((tq // LANES, LANES, LANES), F32),
        ],
        compiler_params=_cparams(("parallel", "parallel", "arbitrary")),
        name="nsa_attn",
    )(main, gl, cos_t, sin_t, kc, vc, ks, vst, kw, vwt)


def _nsa_tables(s):
    half = NSA_HEAD_DIM // 2
    freqs = ROPE_THETA ** (-jnp.arange(half, dtype=F32) / half)
    ang = jnp.arange(s).astype(F32)[:, None] * freqs[None, :]
    cos, sin = jnp.cos(ang), jnp.sin(ang)
    cos_full = jnp.concatenate([cos, cos], axis=-1)
    sin_signed = jnp.concatenate([-sin, sin], axis=-1)
    return cos_full, sin_signed, cos_full.T, sin_signed.T


def _nsa_layer(x, mod, ng, w_in, cmp_pos, cmp_w1, cmp_w2, w_out, tables, *, layer):
    cos, sin, cos_t, sin_t = tables
    g, r = NSA_KV_GROUPS, NSA_HPG
    w_gate = w_in[layer, :, NSA_MAIN_DIM:].reshape(D_MODEL, g, 3 * r)
    w_gate = jnp.pad(w_gate, ((0, 0), (0, 0), (0, LANES - 3 * r))).reshape(D_MODEL, g * LANES)
    main, gl = _proj_call(x, mod, ng, w_in, w_gate, layer=layer, n=NSA_MAIN_DIM, mrow=3, grow=2)
    kc, vc, ks, vst, kw, vwt = _nsa_prep_call(main, cmp_pos, cmp_w1, cmp_w2, cos, sin, layer=layer)
    o = _nsa_attn_call(main, gl, cos_t, sin_t, kc, vc, ks, vst, kw, vwt)
    return _oproj_call(o, w_out, x, mod, ng, layer=layer, mrow=5, grow=3)


def _mlstm_kernel(q_ref, k_ref, v_ref, og_ref, gi_ref, gf_ref, bi_ref, bf_ref, cwq_ref, cwk_ref, cbq_ref, cbk_ref,
                  gain_ref, o_ref, qpad_scr, kpad_scr, ct_scr, rowi_scr, chunk_scr, bcol_scr, ecol_scr):
    s, dk = q_ref.shape
    dv = v_ref.shape[1]
    chunk = ML_CHUNK
    nc = s // chunk
    pad = qpad_scr.shape[0] - s

    for src, dst in ((q_ref, qpad_scr), (k_ref, kpad_scr)):
        dst[:pad] = jnp.zeros((pad, dk), F32)
        dst[pad:] = src[...]

    def conv_silu(pad_ref, w_ref, b_ref, r0):
        win = pad_ref[pl.ds(r0, chunk + pad), :]
        acc = win[pad:] * w_ref[ML_CONV - 1:ML_CONV, :] + b_ref[...]
        for d in range(1, ML_CONV):
            acc = acc + pltpu.roll(win, d, 0)[pad:] * w_ref[ML_CONV - 1 - d:ML_CONV - d, :]
        return _silu(acc)

    ig = gi_ref[...] + bi_ref[...]
    fg = gf_ref[...] + bf_ref[...]
    logf = jnp.minimum(fg, 0.0) - jnp.log1p(jnp.exp(-jnp.abs(fg)))
    tri_r = lax.broadcasted_iota(jnp.int32, (chunk, chunk), 0)
    tri_c = lax.broadcasted_iota(jnp.int32, (chunk, chunk), 1)
    upper = jnp.where(tri_r <= tri_c, 1.0, 0.0)
    b = jnp.dot(logf, upper, precision=HIGHEST, preferred_element_type=F32)
    w_end = b[:, chunk - 1:chunk] - b + ig
    m_loc = jnp.max(w_end, axis=-1, keepdims=True)
    e_end = jnp.exp(w_end - m_loc)
    rowi_scr[...] = ig - b
    b_tot = b[:, chunk - 1:chunk]
    m_run = jnp.zeros((1, 1), F32)
    m_before, m_after = [], []
    for c in range(nc):
        m_before.append(m_run)
        m_run = jnp.maximum(b_tot[c:c + 1, :] + m_run, m_loc[c:c + 1, :])
        m_after.append(m_run)
    m_prev = jnp.concatenate(m_before, axis=0)
    m_next = jnp.concatenate(m_after, axis=0)
    for i, val in enumerate((m_prev, jnp.exp(b_tot + m_prev - m_next), jnp.exp(m_loc - m_next))):
        chunk_scr[i] = jnp.broadcast_to(val, (nc, LANES))
    eye = jnp.where(tri_r == tri_c, 1.0, 0.0)
    cols = lax.dot_general(eye, jnp.concatenate([b, e_end], axis=0), NT_DIMS, precision=HIGHEST,
                           preferred_element_type=F32)
    for c in range(nc):
        bcol_scr[c] = jnp.broadcast_to(cols[:, c:c + 1], (chunk, LANES))
        ecol_scr[c] = jnp.broadcast_to(cols[:, nc + c:nc + c + 1], (chunk, LANES))

    ct_scr[...] = jnp.zeros(ct_scr.shape, F32)
    lower = tri_c <= tri_r
    gain = gain_ref[...]

    n_state_tiles = ct_scr.shape[1] // LANES

    def local_part(c):
        r0 = pl.multiple_of(c * chunk, chunk)
        q = conv_silu(qpad_scr, cwq_ref, cbq_ref, r0).astype(BF16)
        k = conv_silu(kpad_scr, cwk_ref, cbk_ref, r0) * (dk ** -0.5)
        va = jnp.concatenate([v_ref[pl.ds(r0, chunk), :].astype(BF16), jnp.ones((chunk, LANES), BF16)], axis=1)
        m_prev = chunk_scr[0, pl.ds(c, 1), :]
        bcol = bcol_scr[c]
        log_intra = jnp.where(lower, bcol[:, :chunk] + rowi_scr[pl.ds(c, 1), :], NEG_INF)
        log_inter = bcol[:, :1] + m_prev[:, :1]
        m_t = jnp.maximum(log_inter, jnp.max(log_intra, axis=-1, keepdims=True))
        qk = _dot_nt(q, k.astype(BF16)) * jnp.exp(log_intra - m_t)
        intra = _dot(qk.astype(BF16), va)
        c_loc = lax.dot_general((k * ecol_scr[c]).astype(BF16), va, TN_DIMS, preferred_element_type=F32)
        return q, intra, c_loc, jnp.exp(log_inter - m_t), jnp.exp(-m_t)

    def state_part(c, q, intra, c_loc, e_inter, floor):
        r0 = pl.multiple_of(c * chunk, chunk)
        decay, inject = (chunk_scr[i, pl.ds(c, 1), :] for i in (1, 2))
        ct = ct_scr[...]
        tot = intra + e_inter * _dot(q, ct.astype(BF16))
        h = tot[:, :dv] / jnp.maximum(jnp.abs(tot[:, dv:dv + 1]), floor)
        hn = _rms(h) * gain
        o_ref[pl.ds(r0, chunk), :] = (jax.nn.sigmoid(og_ref[pl.ds(r0, chunk), :]) * hn).astype(o_ref.dtype)
        ct_scr[...] = (jnp.concatenate([decay] * n_state_tiles, axis=1) * ct
                       + jnp.concatenate([inject] * n_state_tiles, axis=1) * c_loc)

    group = 4

    def body(i, carry):
        parts = [local_part(i * group + j) for j in range(group)]
        for j in range(group):
            state_part(i * group + j, *parts[j])
        return carry

    lax.fori_loop(0, nc // group, body, 0)


def _mlstm_call(main, gates_t, gate_b, conv_w, conv_b, gain):
    bsz, s, _ = main.shape
    h, dk, dv = ML_HEADS, ML_QK_DIM, ML_V_DIM
    chunk = ML_CHUNK
    nc = s // chunk
    vblk0 = ML_QK_COLS // dv

    def gate(off):
        return pl.BlockSpec((None, None, nc, chunk), lambda b, hi: (b, off + hi, 0, 0))

    def bias(off):
        return pl.BlockSpec((None, 1, 1), lambda b, hi: (off + hi, 0, 0))

    return pl.pallas_call(
        _mlstm_kernel,
        out_shape=jax.ShapeDtypeStruct((bsz, s, h * dv), BF16),
        grid=(bsz, h),
        in_specs=[
            pl.BlockSpec((None, s, dk), lambda b, hi: (b, 0, hi)),
            pl.BlockSpec((None, s, dk), lambda b, hi: (b, 0, h + hi)),
            pl.BlockSpec((None, s, dv), lambda b, hi: (b, 0, vblk0 + hi)),
            pl.BlockSpec((None, s, dv), lambda b, hi: (b, 0, vblk0 + h + hi)),
            gate(0), gate(h), bias(0), bias(h),
            pl.BlockSpec((ML_CONV, dk), lambda b, hi: (0, hi)),
            pl.BlockSpec((ML_CONV, dk), lambda b, hi: (0, h + hi)),
            pl.BlockSpec((1, dk), lambda b, hi: (0, hi)),
            pl.BlockSpec((1, dk), lambda b, hi: (0, h + hi)),
            pl.BlockSpec((1, dv), lambda b, hi: (0, hi)),
        ],
        out_specs=pl.BlockSpec((None, s, dv), lambda b, hi: (b, 0, hi)),
        scratch_shapes=[
            pltpu.VMEM((s + SUBLANES, dk), F32),
            pltpu.VMEM((s + SUBLANES, dk), F32),
            pltpu.VMEM((dk, dv + LANES), F32),
            pltpu.VMEM((nc, chunk), F32),
            pltpu.VMEM((3, nc, LANES), F32),
            pltpu.VMEM((nc, chunk, LANES), F32),
            pltpu.VMEM((nc, chunk, LANES), F32),
        ],
        compiler_params=_cparams(("parallel", "parallel")),
        name="mlstm",
    )(main, main, main, main, gates_t, gates_t, gate_b, gate_b, conv_w, conv_w, conv_b, conv_b, gain)


def _mlstm_layer(x, mod, ng, w_in, conv_w, conv_b, gate_b, mh_gain, w_out, *, layer):
    bsz, s, _ = x.shape
    ngate = 2 * ML_HEADS
    w_gate = jnp.pad(w_in[layer, :, ML_MAIN_DIM:], ((0, 0), (0, LANES - ngate)))
    main, gl = _proj_call(x, mod, ng, w_in, w_gate, layer=layer, n=ML_MAIN_DIM, mrow=3, grow=2)
    gates_t = jnp.swapaxes(gl[:, :, :ngate], 1, 2).reshape(bsz, ngate, s // ML_CHUNK, ML_CHUNK)
    o = _mlstm_call(main, gates_t, gate_b.reshape(ngate, 1, 1), conv_w, conv_b.reshape(1, -1),
                    mh_gain.reshape(1, -1))
    return _oproj_call(o, w_out, x, mod, ng, layer=layer, mrow=5, grow=3)


def kernel(x, c, mod_w, mod_b, norm_g, ffn_pre_w_in, ffn_pre_w_out, ffn_post_w_in, ffn_post_w_out, nsa_w_in, nsa_cmp_pos, nsa_cmp_w1, nsa_cmp_w2, nsa_w_out, ml_w_in, ml_conv_w, ml_conv_b, ml_gate_b, ml_mh_gain, ml_w_out):
    bsz, s, d = x.shape
    depth = mod_w.shape[0]
    mods = _mod_call(c, mod_w, mod_b).reshape(depth, bsz, N_MOD, d)
    tables = _nsa_tables(s)
    pre_in, pre_out = ffn_pre_w_in.astype(BF16), ffn_pre_w_out.astype(BF16)
    post_in, post_out = ffn_post_w_in.astype(BF16), ffn_post_w_out.astype(BF16)
    nsa_in, nsa_out = nsa_w_in.astype(BF16), nsa_w_out.astype(BF16)
    nsa_w1, nsa_w2 = nsa_cmp_w1.astype(BF16), nsa_cmp_w2.astype(BF16)
    ml_in, ml_out = ml_w_in.astype(BF16), ml_w_out.astype(BF16)
    for i in range(depth):
        mod, ng = mods[i], norm_g[i]
        x = _ffn_call(x, mod, ng, pre_in, pre_out, layer=i, mrow=0, grow=0)
        j = i // 2
        if i % 2 == 0:
            x = _nsa_layer(x, mod, ng, nsa_in, nsa_cmp_pos, nsa_w1, nsa_w2, nsa_out, tables, layer=j)
        else:
            x = _mlstm_layer(x, mod, ng, ml_in, ml_conv_w[j], ml_conv_b[j], ml_gate_b[j], ml_mh_gain[j], ml_out,
                             layer=j)
        x = _ffn_call(x, mod, ng, post_in, post_out, layer=i, mrow=6, grow=4)
    return x
```

```python
import functools

import jax
import jax.numpy as jnp
from jax import lax
from jax.experimental import pallas as pl
from jax.experimental.pallas import tpu as pltpu

F32 = jnp.float32
BF16 = jnp.bfloat16
HIGHEST = lax.Precision.HIGHEST

D_MODEL = 2048
DEPTH = 4
D_FF = 5632
FFN_RES = 0.5
N_MOD = 9
RMS_EPS = 1e-6
NEG_INF = -1e30
LOG2_E = 1.4426950408889634

NSA_HEADS = 16
NSA_HEAD_DIM = 128
NSA_KV_GROUPS = 4
NSA_HPG = NSA_HEADS // NSA_KV_GROUPS
NSA_Q_DIM = NSA_HEADS * NSA_HEAD_DIM
NSA_KV_DIM = NSA_KV_GROUPS * NSA_HEAD_DIM
NSA_MAIN_DIM = NSA_Q_DIM + 6 * NSA_KV_DIM
CMP_BLOCK = 32
CMP_STRIDE = 16
CMP_HIDDEN = 512
SEL_BLOCK = 64
SEL_TOPK = 16
N_LOCAL_BLOCKS = 2
FORCE_SCORE = 1e9
WINDOW = 512
ROPE_THETA = 10000.0

ML_HEADS = 8
ML_QK_DIM = 128
ML_V_DIM = 256
ML_CONV = 4
ML_CHUNK = 64
ML_QK_COLS = 2 * ML_HEADS * ML_QK_DIM
ML_MAIN_DIM = ML_QK_COLS + 2 * ML_HEADS * ML_V_DIM

LANES = 128
SUBLANES = 8
VMEM_LIMIT = 56 * 1024 * 1024

NT_DIMS = (((1,), (1,)), ((), ()))
TN_DIMS = (((0,), (0,)), ((), ()))


def _cparams(sem):
    return pltpu.CompilerParams(dimension_semantics=sem, vmem_limit_bytes=VMEM_LIMIT)


def _silu(x):
    return x * jax.nn.sigmoid(x)


def _rms(x):
    return x * lax.rsqrt(jnp.mean(x * x, axis=-1, keepdims=True) + RMS_EPS)


def _dot(a, b):
    return jnp.dot(a, b, preferred_element_type=F32)


def _dot_nt(a, b):
    return lax.dot_general(a, b, NT_DIMS, preferred_element_type=F32)


def _mod_kernel(c_ref, w_ref, b_ref, o_ref):
    ca = _silu(c_ref[...]).astype(BF16)
    o_ref[...] = _dot(ca, w_ref[...].astype(BF16)) + b_ref[...]


def _mod_call(c, mod_w, mod_b):
    depth, d, n = mod_w.shape
    bsz = c.shape[0]
    tn = 1024
    return pl.pallas_call(
        _mod_kernel,
        out_shape=jax.ShapeDtypeStruct((depth, bsz, n), F32),
        grid=(depth, n // tn),
        in_specs=[
            pl.BlockSpec((bsz, d), lambda i, j: (0, 0)),
            pl.BlockSpec((None, d, tn), lambda i, j: (i, 0, j)),
            pl.BlockSpec((None, 1, tn), lambda i, j: (i, 0, j)),
        ],
        out_specs=pl.BlockSpec((None, bsz, tn), lambda i, j: (i, 0, j)),
        compiler_params=_cparams(("parallel", "parallel")),
        name="mod",
    )(c, mod_w, mod_b.reshape(depth, 1, n))


def _prenorm(x, mod_ref, ng_ref, mrow, grow):
    gain = ng_ref[grow:grow + 1, :] * (1.0 + mod_ref[mrow + 1:mrow + 2, :])
    return _rms(x) * gain + mod_ref[mrow:mrow + 1, :]


ROW_CHUNK = 16
ROW_UNROLL = 8


def _inv_rms(x):
    return lax.rsqrt(jnp.mean(x * x, axis=-1, keepdims=True) + RMS_EPS)


def _row_loop(n_rows, stats, apply):
    def step(i, carry):
        slices = [pl.ds(pl.multiple_of((i * ROW_UNROLL + u) * ROW_CHUNK, ROW_CHUNK), ROW_CHUNK)
                  for u in range(ROW_UNROLL)]
        stat = [stats(rows) for rows in slices]
        for rows, st in zip(slices, stat):
            apply(rows, st)
        return carry
    lax.fori_loop(0, n_rows // (ROW_CHUNK * ROW_UNROLL), step, 0)


def _ffn_kernel(x_ref, mod_ref, ng_ref, wg_ref, wu_ref, wo_ref, o_ref, xn_ref, acc_ref, *, mrow, grow):
    j = pl.program_id(2)

    @pl.when(j == 0)
    def _():
        gain = ng_ref[grow:grow + 1, :] * (1.0 + mod_ref[mrow + 1:mrow + 2, :])
        shift = mod_ref[mrow:mrow + 1, :]

        def prenorm_rows(rows, inv):
            xn_ref[rows, :] = (x_ref[rows, :] * inv * gain + shift).astype(BF16)

        _row_loop(x_ref.shape[0], lambda rows: _inv_rms(x_ref[rows, :]), prenorm_rows)
        acc_ref[...] = jnp.zeros_like(acc_ref)

    xn = xn_ref[...]
    g = _dot(xn, wg_ref[...])
    u = _dot(xn, wu_ref[...])
    acc_ref[...] += _dot((_silu(g) * u).astype(BF16), wo_ref[...])

    @pl.when(j == pl.num_programs(2) - 1)
    def _():
        gain = ng_ref[grow + 1:grow + 2, :] * (FFN_RES * mod_ref[mrow + 2:mrow + 3, :])

        def finish_rows(rows, inv):
            o_ref[rows, :] = x_ref[rows, :] + acc_ref[rows, :] * inv * gain

        _row_loop(x_ref.shape[0], lambda rows: _inv_rms(acc_ref[rows, :]), finish_rows)


def _ffn_call(x, mod, ng, w_in, w_out, *, layer, mrow, grow):
    bsz, s, d = x.shape
    dff = w_out.shape[1]
    tm, tf = 512, 512
    nf = dff // tf
    return pl.pallas_call(
        functools.partial(_ffn_kernel, mrow=mrow, grow=grow),
        out_shape=jax.ShapeDtypeStruct(x.shape, F32),
        grid=(bsz, s // tm, nf),
        in_specs=[
            pl.BlockSpec((None, tm, d), lambda b, m, j: (b, m, 0)),
            pl.BlockSpec((None, N_MOD, d), lambda b, m, j: (b, 0, 0)),
            pl.BlockSpec(ng.shape, lambda b, m, j: (0, 0)),
            pl.BlockSpec((None, d, tf), lambda b, m, j: (layer, 0, j)),
            pl.BlockSpec((None, d, tf), lambda b, m, j: (layer, 0, nf + j)),
            pl.BlockSpec((None, tf, d), lambda b, m, j: (layer, j, 0)),
        ],
        out_specs=pl.BlockSpec((None, tm, d), lambda b, m, j: (b, m, 0)),
        scratch_shapes=[pltpu.VMEM((tm, d), BF16), pltpu.VMEM((tm, d), F32)],
        compiler_params=_cparams(("parallel", "parallel", "arbitrary")),
        name="ffn",
    )(x, mod, ng, w_in, w_in, w_out)


def _proj_kernel(x_ref, mod_ref, ng_ref, w_ref, wgate_ref, o_ref, og_ref, xn_ref, *, mrow, grow):
    @pl.when(pl.program_id(2) == 0)
    def _():
        xn = _prenorm(x_ref[...], mod_ref, ng_ref, mrow, grow).astype(BF16)
        xn_ref[...] = xn
        og_ref[...] = _dot(xn, wgate_ref[...])

    o_ref[...] = _dot(xn_ref[...], w_ref[...])


def _proj_call(x, mod, ng, w_in, w_gate, *, layer, n, mrow, grow):
    bsz, s, d = x.shape
    ngate = w_gate.shape[1]
    tm, tn = 1024, 1024
    return pl.pallas_call(
        functools.partial(_proj_kernel, mrow=mrow, grow=grow),
        out_shape=(jax.ShapeDtypeStruct((bsz, s, n), F32), jax.ShapeDtypeStruct((bsz, s, ngate), F32)),
        grid=(bsz, s // tm, n // tn),
        in_specs=[
            pl.BlockSpec((None, tm, d), lambda b, m, j: (b, m, 0)),
            pl.BlockSpec((None, N_MOD, d), lambda b, m, j: (b, 0, 0)),
            pl.BlockSpec(ng.shape, lambda b, m, j: (0, 0)),
            pl.BlockSpec((None, d, tn), lambda b, m, j: (layer, 0, j)),
            pl.BlockSpec((d, ngate), lambda b, m, j: (0, 0)),
        ],
        out_specs=(pl.BlockSpec((None, tm, tn), lambda b, m, j: (b, m, j)),
                   pl.BlockSpec((None, tm, ngate), lambda b, m, j: (b, m, 0))),
        scratch_shapes=[pltpu.VMEM((tm, d), BF16)],
        compiler_params=_cparams(("parallel", "parallel", "arbitrary")),
        name="proj",
    )(x, mod, ng, w_in, w_gate)


def _oproj_kernel(a_ref, w_ref, x_ref, mod_ref, ng_ref, o_ref, *, mrow, grow):
    yn = _rms(_dot(a_ref[...], w_ref[...])) * ng_ref[grow:grow + 1, :]
    o_ref[...] = x_ref[...] + mod_ref[mrow:mrow + 1, :] * yn


def _oproj_call(a, w, x, mod, ng, *, layer, mrow, grow):
    bsz, s, d = x.shape
    k = a.shape[-1]
    tm = 512
    return pl.pallas_call(
        functools.partial(_oproj_kernel, mrow=mrow, grow=grow),
        out_shape=jax.ShapeDtypeStruct(x.shape, F32),
        grid=(bsz, s // tm),
        in_specs=[
            pl.BlockSpec((None, tm, k), lambda b, m: (b, m, 0)),
            pl.BlockSpec((None, k, d), lambda b, m: (layer, 0, 0)),
            pl.BlockSpec((None, tm, d), lambda b, m: (b, m, 0)),
            pl.BlockSpec((None, N_MOD, d), lambda b, m: (b, 0, 0)),
            pl.BlockSpec(ng.shape, lambda b, m: (0, 0)),
        ],
        out_specs=pl.BlockSpec((None, tm, d), lambda b, m: (b, m, 0)),
        compiler_params=_cparams(("parallel", "parallel")),
        name="oproj",
    )(a, w, x, mod, ng)


def _rope(x, cos, sin_signed):
    return x * cos + pltpu.roll(x, NSA_HEAD_DIM // 2, 1) * sin_signed


def _nsa_prep_kernel(kc_ref, vc_ref, ks_ref, vs_ref, kw_ref, vw_ref, pe_ref, w1_ref, w2_ref, cos_ref, sin_ref,
                     kco_ref, vco_ref, kso_ref, vso_ref, kwo_ref, vwo_ref):
    s = kc_ref.shape[0]
    nrow = s // CMP_STRIDE
    half = CMP_BLOCK // CMP_STRIDE

    def compress(t_ref, idx):
        streams = [t_ref[pl.ds(j, nrow, stride=CMP_STRIDE), :] for j in range(CMP_STRIDE)]
        hidden = None
        for h in range(half):
            flat = jnp.concatenate(
                [(streams[j] + pe_ref[idx, h * CMP_STRIDE + j:h * CMP_STRIDE + j + 1, :]).astype(BF16)
                 for j in range(CMP_STRIDE)], axis=1)
            w = w1_ref[idx, h * CMP_STRIDE * NSA_HEAD_DIM:(h + 1) * CMP_STRIDE * NSA_HEAD_DIM, :]
            part = _dot(flat, w)
            if h:
                part = pltpu.roll(part, nrow - h, 0)
            hidden = part if hidden is None else hidden + part
        return _dot(_silu(hidden).astype(BF16), w2_ref[idx])

    kco_ref[...] = compress(kc_ref, 0).astype(kco_ref.dtype)
    vco_ref[...] = compress(vc_ref, 1).astype(vco_ref.dtype)
    cos = cos_ref[...]
    sin = sin_ref[...]
    kso_ref[...] = _rope(ks_ref[...], cos, sin).astype(BF16)
    kwo_ref[...] = _rope(kw_ref[...], cos, sin).astype(BF16)
    vso_ref[...] = vs_ref[...].T.astype(BF16)
    vwo_ref[...] = vw_ref[...].T.astype(BF16)


def _nsa_prep_call(main, pe, w1, w2, cos, sin, *, layer):
    bsz, s, _ = main.shape
    g, dh = NSA_KV_GROUPS, NSA_HEAD_DIM
    nrow = s // CMP_STRIDE
    q_blocks = NSA_Q_DIM // dh

    def col(k):
        return pl.BlockSpec((None, s, dh), lambda b, gi, k=k: (b, 0, q_blocks + k * g + gi))

    def full(a):
        return pl.BlockSpec(a.shape, lambda b, gi, nd=a.ndim: (0,) * nd)

    def stacked(a):
        return pl.BlockSpec((None,) + a.shape[1:], lambda b, gi, nd=a.ndim: (layer,) + (0,) * (nd - 1))

    def out(rows, cols):
        return (jax.ShapeDtypeStruct((bsz, g, rows, cols), BF16),
                pl.BlockSpec((None, None, rows, cols), lambda b, gi: (b, gi, 0, 0)))

    outs = [out(nrow, dh), out(nrow, dh), out(s, dh), out(dh, s), out(s, dh), out(dh, s)]
    return pl.pallas_call(
        _nsa_prep_kernel,
        out_shape=tuple(o[0] for o in outs),
        grid=(bsz, g),
        in_specs=[col(0), col(1), col(2), col(3), col(4), col(5), stacked(pe), stacked(w1), stacked(w2),
                  full(cos), full(sin)],
        out_specs=tuple(o[1] for o in outs),
        compiler_params=_cparams(("parallel", "parallel")),
        name="nsa_prep",
    )(main, main, main, main, main, main, pe, w1, w2, cos, sin)


def _nsa_attn_kernel(q_ref, gl_ref, cos_ref, sin_ref, kc_ref, vc_ref, ks_ref, vst_ref, kw_ref, vwt_ref,
                     o_ref, qrot_scr, selb_scr, m_scr, l_scr, alpha_scr, acc_scr, out_scr, s_scr, p_scr, psum_scr,
                     *, tq, tk, n_sel):
    qi = pl.program_id(2)
    dh = NSA_HEAD_DIM
    ng = kc_ref.shape[0]
    heads = [(g, r) for g in range(ng) for r in range(NSA_HPG)]
    scale = dh ** -0.5 * LOG2_E
    pos_q = qi * tq + lax.broadcasted_iota(jnp.int32, (1, tq), 1)
    cos = cos_ref[...]
    sin = sin_ref[...]
    gl_t = gl_ref[...].T
    gates = [jax.nn.sigmoid(gl_t[g * LANES:g * LANES + 4 * NSA_HPG, :]) for g in range(ng)]

    def gate(g, r, branch):
        return gates[g][3 * r + branch:3 * r + branch + 1, :]

    crow = lax.broadcasted_iota(jnp.int32, (LANES, 1), 0)
    cmask = crow * CMP_STRIDE + (CMP_BLOCK - 1) <= pos_q
    jrow = lax.broadcasted_iota(jnp.int32, (n_sel, 1), 0)
    q_blk = pos_q // SEL_BLOCK
    causal = jrow <= q_blk
    forced = (jrow == 0) | (causal & (jrow > q_blk - N_LOCAL_BLOCKS))
    top_k = min(SEL_TOPK, n_sel)
    rank_unroll = 4
    n_causal = (qi * tq + tq) // SEL_BLOCK
    n_rank_steps = jnp.where(n_causal <= top_k, 0, n_causal // rank_unroll)
    for h in range(len(heads)):
        acc_scr[h] = q_ref[:, h * dh:(h + 1) * dh].T * scale
    for h, (g, _) in enumerate(heads):
        s_scr[h, :LANES, :] = _dot(kc_ref[g], acc_scr[h].astype(BF16))
    for h in range(len(heads)):
        qt = acc_scr[h]
        rot = jnp.concatenate([qt[dh // 2:], qt[:dh // 2]], axis=0)
        qrot_scr[h] = (qt * cos + rot * sin).astype(BF16)
    for g in range(ng):
        p_sum = jnp.zeros((LANES, tq), F32)
        for h in range(g * NSA_HPG, (g + 1) * NSA_HPG):
            sc = jnp.where(cmask, s_scr[h, :LANES, :], NEG_INF)
            e = jnp.exp2(sc - jnp.max(sc, axis=0, keepdims=True))
            p = jnp.where(cmask, e * (1.0 / jnp.sum(e, axis=0, keepdims=True)), 0.0)
            p_sum = p_sum + p
            p_scr[0, h, :LANES, :] = p.astype(BF16)
        vct = vc_ref[g].astype(F32).T.astype(BF16)
        for r in range(NSA_HPG):
            h = g * NSA_HPG + r
            out_scr[h] = gate(g, r, 0) * _dot(vct, p_scr[0, h, :LANES, :])

        per = SEL_BLOCK // CMP_STRIDE
        back = CMP_BLOCK // CMP_STRIDE - 1
        for t in range(tq // LANES):
            psum_scr[t] = p_sum[:, t * LANES:(t + 1) * LANES]

        def block_rows(first):
            return jnp.concatenate([psum_scr[t, pl.ds(first, n_sel, stride=per), :] for t in range(tq // LANES)],
                                   axis=1)

        imp = block_rows(0)
        for k in range(1, per):
            imp = imp + block_rows(k)
        for k in range(1, back + 1):
            before = block_rows(per - k)
            imp = imp + jnp.where(jrow >= 1, pltpu.roll(before, 1, 0), 0.0)
        imp = jnp.where(forced, FORCE_SCORE, jnp.where(causal, imp, -1.0))
        selb_scr[g] = imp

        def rank_body(i, rank, g=g, imp=imp):
            for k in range(rank_unroll):
                jp = i * rank_unroll + k
                other = selb_scr[g, pl.ds(jp, 1), :]
                tie = jnp.where(jrow > jp, 1.0, 0.0)
                rank = rank + jnp.where(other > imp, 1.0, jnp.where(other == imp, tie, 0.0))
            return rank

        rank = lax.fori_loop(0, n_rank_steps, rank_body, jnp.zeros((n_sel, tq), F32))
        keep = (rank < float(top_k)) & (imp >= 0.0)
        selb_scr[g] = jnp.where(keep, 0.0, NEG_INF)

    def reset(first_kt):
        m_scr[...] = jnp.full(m_scr.shape, NEG_INF, F32)
        l_scr[...] = jnp.zeros(l_scr.shape, F32)
        acc_scr[...] = jnp.zeros(acc_scr.shape, F32)
        alpha_scr[...] = jnp.ones(alpha_scr.shape, F32)
        p_scr[(first_kt - 1) & 1] = jnp.zeros(p_scr.shape[1:], BF16)

    def tile0(kt):
        return pl.multiple_of(jnp.maximum(kt, 0) * tk, tk)

    def apply_pending(vt_ref, kt):
        slot = kt & 1
        for g in range(ng):
            vt = vt_ref[g, :, pl.ds(tile0(kt), tk)]
            for h in range(g * NSA_HPG, (g + 1) * NSA_HPG):
                acc_scr[h] = alpha_scr[slot, h] * acc_scr[h] + _dot(vt, p_scr[slot, h])

    def flash_step(k_ref, vt_ref, kt, bias):
        slot = kt & 1
        for g in range(ng):
            k_tile = k_ref[g, pl.ds(tile0(kt), tk), :]
            for h in range(g * NSA_HPG, (g + 1) * NSA_HPG):
                s_scr[h] = _dot(k_tile, qrot_scr[h])
        apply_pending(vt_ref, kt - 1)

        for h, (g, _) in enumerate(heads):
            sc = s_scr[h]
            head_bias = bias[g] if isinstance(bias, list) else bias
            if head_bias is not None:
                sc = sc + head_bias
            m_old = m_scr[h]
            m_new = jnp.maximum(m_old, jnp.max(sc, axis=0, keepdims=True))
            alpha = jnp.exp2(m_old - m_new)
            p = jnp.exp2(sc - m_new)
            l_scr[h] = alpha * l_scr[h] + jnp.sum(p, axis=0, keepdims=True)
            alpha_scr[slot, h] = alpha
            p_scr[slot, h] = p.astype(BF16)
            m_scr[h] = m_new

    def finish(branch):
        for h, (g, r) in enumerate(heads):
            w = gate(g, r, branch) / l_scr[h]
            out_scr[h] = out_scr[h] + w * acc_scr[h]

    def sel_bias(kt, extra=None):
        per_tile = tk // SEL_BLOCK
        biases = []
        for g in range(ng):
            rows = [selb_scr[g, pl.ds(kt * per_tile + j, 1), :] for j in range(per_tile)]
            b = jnp.concatenate([jnp.broadcast_to(row, (SEL_BLOCK, tq)) for row in rows], axis=0)
            biases.append(b if extra is None else b + extra)
        return biases

    def key_pos(kt):
        return kt * tk + lax.broadcasted_iota(jnp.int32, (tk, 1), 0)

    causal_bias = jnp.where(key_pos(qi) <= pos_q, 0.0, NEG_INF)

    reset(0)

    def sel_body(kt, carry):
        flash_step(ks_ref, vst_ref, kt, sel_bias(kt))
        return carry

    lax.fori_loop(0, qi, sel_body, 0)
    flash_step(ks_ref, vst_ref, qi, sel_bias(qi, causal_bias))
    apply_pending(vst_ref, qi)
    finish(1)

    n_back = WINDOW // tk
    reset(jnp.maximum(qi - n_back, 0))
    for back in range(n_back, 0, -1):
        @pl.when(qi >= back)
        def _(back=back):
            bias = jnp.where(pos_q - key_pos(qi - back) < WINDOW, 0.0, NEG_INF) if back == n_back else None
            flash_step(kw_ref, vwt_ref, qi - back, bias)
    flash_step(kw_ref, vwt_ref, qi, causal_bias)
    apply_pending(vwt_ref, qi)
    finish(2)

    for h in range(len(heads)):
        o_ref[:, h * dh:(h + 1) * dh] = out_scr[h].T.astype(o_ref.dtype)


def _nsa_attn_call(main, gl, cos_t, sin_t, kc, vc, ks, vst, kw, vwt):
    bsz, s, _ = main.shape
    g, r, dh = NSA_KV_GROUPS, NSA_HPG, NSA_HEAD_DIM
    tq = tk = 256
    ng = 1
    nh = ng * r
    assert WINDOW % tk == 0 and tk % SEL_BLOCK == 0 and s % tq == 0 and g % ng == 0
    nrow = kc.shape[2]
    n_sel = s // SEL_BLOCK

    def kv(rows, cols):
        return pl.BlockSpec((None, ng, rows, cols), lambda b, gi, qi: (b, gi, 0, 0))

    return pl.pallas_call(
        functools.partial(_nsa_attn_kernel, tq=tq, tk=tk, n_sel=n_sel),
        out_shape=jax.ShapeDtypeStruct((bsz, s, NSA_Q_DIM), BF16),
        grid=(bsz, g // ng, s // tq),
        in_specs=[
            pl.BlockSpec((None, tq, nh * dh), lambda b, gi, qi: (b, qi, gi)),
            pl.BlockSpec((None, tq, ng * LANES), lambda b, gi, qi: (b, qi, gi)),
            pl.BlockSpec((dh, tq), lambda b, gi, qi: (0, qi)),
            pl.BlockSpec((dh, tq), lambda b, gi, qi: (0, qi)),
            kv(nrow, dh), kv(nrow, dh), kv(s, dh), kv(dh, s), kv(s, dh), kv(dh, s),
        ],
        out_specs=pl.BlockSpec((None, tq, nh * dh), lambda b, gi, qi: (b, qi, gi)),
        scratch_shapes=[
            pltpu.VMEM((nh, dh, tq), BF16),
            pltpu.VMEM((ng, n_sel, tq), F32),
            pltpu.VMEM((nh, 1, tq), F32),
            pltpu.VMEM((nh, 1, tq), F32),
            pltpu.VMEM((2, nh, 1, tq), F32),
            pltpu.VMEM((nh, dh, tq), F32),
            pltpu.VMEM((nh, dh, tq), F32),
            pltpu.VMEM((nh, tk, tq), F32),
            pltpu.VMEM((2, nh, tk, tq), BF16),
            pltpu.VMEM((tq // LANES, LANES, LANES), F32),
        ],
        compiler_params=_cparams(("parallel", "parallel", "arbitrary")),
        name="nsa_attn",
    )(main, gl, cos_t, sin_t, kc, vc, ks, vst, kw, vwt)


def _nsa_tables(s):
    half = NSA_HEAD_DIM // 2
    freqs = ROPE_THETA ** (-jnp.arange(half, dtype=F32) / half)
    ang = jnp.arange(s).astype(F32)[:, None] * freqs[None, :]
    cos, sin = jnp.cos(ang), jnp.sin(ang)
    cos_full = jnp.concatenate([cos, cos], axis=-1)
    sin_signed = jnp.concatenate([-sin, sin], axis=-1)
    return cos_full, sin_signed, cos_full.T, sin_signed.T


def _nsa_layer(x, mod, ng, w_in, cmp_pos, cmp_w1, cmp_w2, w_out, tables, *, layer):
    cos, sin, cos_t, sin_t = tables
    g, r = NSA_KV_GROUPS, NSA_HPG
    w_gate = w_in[layer, :, NSA_MAIN_DIM:].reshape(D_MODEL, g, 3 * r)
    w_gate = jnp.pad(w_gate, ((0, 0), (0, 0), (0, LANES - 3 * r))).reshape(D_MODEL, g * LANES)
    main, gl = _proj_call(x, mod, ng, w_in, w_gate, layer=layer, n=NSA_MAIN_DIM, mrow=3, grow=2)
    kc, vc, ks, vst, kw, vwt = _nsa_prep_call(main, cmp_pos, cmp_w1, cmp_w2, cos, sin, layer=layer)
    o = _nsa_attn_call(main, gl, cos_t, sin_t, kc, vc, ks, vst, kw, vwt)
    return _oproj_call(o, w_out, x, mod, ng, layer=layer, mrow=5, grow=3)


def _mlstm_kernel(q_ref, k_ref, v_ref, og_ref, gi_ref, gf_ref, bi_ref, bf_ref, cwq_ref, cwk_ref, cbq_ref, cbk_ref,
                  gain_ref, o_ref, qpad_scr, kpad_scr, ct_scr, rowi_scr, chunk_scr, bcol_scr, ecol_scr):
    s, dk = q_ref.shape
    dv = v_ref.shape[1]
    chunk = ML_CHUNK
    nc = s // chunk
    pad = qpad_scr.shape[0] - s

    for src, dst in ((q_ref, qpad_scr), (k_ref, kpad_scr)):
        dst[:pad] = jnp.zeros((pad, dk), F32)
        dst[pad:] = src[...]

    def conv_silu(pad_ref, w_ref, b_ref, r0):
        win = pad_ref[pl.ds(r0, chunk + pad), :]
        acc = win[pad:] * w_ref[ML_CONV - 1:ML_CONV, :] + b_ref[...]
        for d in range(1, ML_CONV):
            acc = acc + pltpu.roll(win, d, 0)[pad:] * w_ref[ML_CONV - 1 - d:ML_CONV - d, :]
        return _silu(acc)

    ig = gi_ref[...] + bi_ref[...]
    fg = gf_ref[...] + bf_ref[...]
    logf = jnp.minimum(fg, 0.0) - jnp.log1p(jnp.exp(-jnp.abs(fg)))
    tri_r = lax.broadcasted_iota(jnp.int32, (chunk, chunk), 0)
    tri_c = lax.broadcasted_iota(jnp.int32, (chunk, chunk), 1)
    upper = jnp.where(tri_r <= tri_c, 1.0, 0.0)
    b = jnp.dot(logf, upper, precision=HIGHEST, preferred_element_type=F32)
    w_end = b[:, chunk - 1:chunk] - b + ig
    m_loc = jnp.max(w_end, axis=-1, keepdims=True)
    e_end = jnp.exp(w_end - m_loc)
    rowi_scr[...] = ig - b
    b_tot = b[:, chunk - 1:chunk]
    m_run = jnp.zeros((1, 1), F32)
    m_before, m_after = [], []
    for c in range(nc):
        m_before.append(m_run)
        m_run = jnp.maximum(b_tot[c:c + 1, :] + m_run, m_loc[c:c + 1, :])
        m_after.append(m_run)
    m_prev = jnp.concatenate(m_before, axis=0)
    m_next = jnp.concatenate(m_after, axis=0)
    for i, val in enumerate((m_prev, jnp.exp(b_tot + m_prev - m_next), jnp.exp(m_loc - m_next))):
        chunk_scr[i] = jnp.broadcast_to(val, (nc, LANES))
    eye = jnp.where(tri_r == tri_c, 1.0, 0.0)
    cols = lax.dot_general(eye, jnp.concatenate([b, e_end], axis=0), NT_DIMS, precision=HIGHEST,
                           preferred_element_type=F32)
    for c in range(nc):
        bcol_scr[c] = jnp.broadcast_to(cols[:, c:c + 1], (chunk, LANES))
        ecol_scr[c] = jnp.broadcast_to(cols[:, nc + c:nc + c + 1], (chunk, LANES))

    ct_scr[...] = jnp.zeros(ct_scr.shape, F32)
    lower = tri_c <= tri_r
    gain = gain_ref[...]

    n_state_tiles = ct_scr.shape[1] // LANES

    def local_part(c):
        r0 = pl.multiple_of(c * chunk, chunk)
        q = conv_silu(qpad_scr, cwq_ref, cbq_ref, r0).astype(BF16)
        k = conv_silu(kpad_scr, cwk_ref, cbk_ref, r0) * (dk ** -0.5)
        va = jnp.concatenate([v_ref[pl.ds(r0, chunk), :].astype(BF16), jnp.ones((chunk, LANES), BF16)], axis=1)
        m_prev = chunk_scr[0, pl.ds(c, 1), :]
        bcol = bcol_scr[c]
        log_intra = jnp.where(lower, bcol[:, :chunk] + rowi_scr[pl.ds(c, 1), :], NEG_INF)
        log_inter = bcol[:, :1] + m_prev[:, :1]
        m_t = jnp.maximum(log_inter, jnp.max(log_intra, axis=-1, keepdims=True))
        qk = _dot_nt(q, k.astype(BF16)) * jnp.exp(log_intra - m_t)
        intra = _dot(qk.astype(BF16), va)
        c_loc = lax.dot_general((k * ecol_scr[c]).astype(BF16), va, TN_DIMS, preferred_element_type=F32)
        return q, intra, c_loc, jnp.exp(log_inter - m_t), jnp.exp(-m_t)

    def state_part(c, q, intra, c_loc, e_inter, floor):
        r0 = pl.multiple_of(c * chunk, chunk)
        decay, inject = (chunk_scr[i, pl.ds(c, 1), :] for i in (1, 2))
        ct = ct_scr[...]
        tot = intra + e_inter * _dot(q, ct.astype(BF16))
        h = tot[:, :dv] / jnp.maximum(jnp.abs(tot[:, dv:dv + 1]), floor)
        hn = _rms(h) * gain
        o_ref[pl.ds(r0, chunk), :] = (jax.nn.sigmoid(og_ref[pl.ds(r0, chunk), :]) * hn).astype(o_ref.dtype)
        ct_scr[...] = (jnp.concatenate([decay] * n_state_tiles, axis=1) * ct
                       + jnp.concatenate([inject] * n_state_tiles, axis=1) * c_loc)

    group = 4

    def body(i, carry):
        parts = [local_part(i * group + j) for j in range(group)]
        for j in range(group):
            state_part(i * group + j, *parts[j])
        return carry

    lax.fori_loop(0, nc // group, body, 0)


def _mlstm_call(main, gates_t, gate_b, conv_w, conv_b, gain):
    bsz, s, _ = main.shape
    h, dk, dv = ML_HEADS, ML_QK_DIM, ML_V_DIM
    chunk = ML_CHUNK
    nc = s // chunk
    vblk0 = ML_QK_COLS // dv

    def gate(off):
        return pl.BlockSpec((None, None, nc, chunk), lambda b, hi: (b, off + hi, 0, 0))

    def bias(off):
        return pl.BlockSpec((None, 1, 1), lambda b, hi: (off + hi, 0, 0))

    return pl.pallas_call(
        _mlstm_kernel,
        out_shape=jax.ShapeDtypeStruct((bsz, s, h * dv), BF16),
        grid=(bsz, h),
        in_specs=[
            pl.BlockSpec((None, s, dk), lambda b, hi: (b, 0, hi)),
            pl.BlockSpec((None, s, dk), lambda b, hi: (b, 0, h + hi)),
            pl.BlockSpec((None, s, dv), lambda b, hi: (b, 0, vblk0 + hi)),
            pl.BlockSpec((None, s, dv), lambda b, hi: (b, 0, vblk0 + h + hi)),
            gate(0), gate(h), bias(0), bias(h),
            pl.BlockSpec((ML_CONV, dk), lambda b, hi: (0, hi)),
            pl.BlockSpec((ML_CONV, dk), lambda b, hi: (0, h + hi)),
            pl.BlockSpec((1, dk), lambda b, hi: (0, hi)),
            pl.BlockSpec((1, dk), lambda b, hi: (0, h + hi)),
            pl.BlockSpec((1, dv), lambda b, hi: (0, hi)),
        ],
        out_specs=pl.BlockSpec((None, s, dv), lambda b, hi: (b, 0, hi)),
        scratch_shapes=[
            pltpu.VMEM((s + SUBLANES, dk), F32),
            pltpu.VMEM((s + SUBLANES, dk), F32),
            pltpu.VMEM((dk, dv + LANES), F32),
            pltpu.VMEM((nc, chunk), F32),
            pltpu.VMEM((3, nc, LANES), F32),
            pltpu.VMEM((nc, chunk, LANES), F32),
            pltpu.VMEM((nc, chunk, LANES), F32),
        ],
        compiler_params=_cparams(("parallel", "parallel")),
        name="mlstm",
    )(main, main, main, main, gates_t, gates_t, gate_b, gate_b, conv_w, conv_w, conv_b, conv_b, gain)


def _mlstm_layer(x, mod, ng, w_in, conv_w, conv_b, gate_b, mh_gain, w_out, *, layer):
    bsz, s, _ = x.shape
    ngate = 2 * ML_HEADS
    w_gate = jnp.pad(w_in[layer, :, ML_MAIN_DIM:], ((0, 0), (0, LANES - ngate)))
    main, gl = _proj_call(x, mod, ng, w_in, w_gate, layer=layer, n=ML_MAIN_DIM, mrow=3, grow=2)
    gates_t = jnp.swapaxes(gl[:, :, :ngate], 1, 2).reshape(bsz, ngate, s // ML_CHUNK, ML_CHUNK)
    o = _mlstm_call(main, gates_t, gate_b.reshape(ngate, 1, 1), conv_w, conv_b.reshape(1, -1),
                    mh_gain.reshape(1, -1))
    return _oproj_call(o, w_out, x, mod, ng, layer=layer, mrow=5, grow=3)


def kernel(x, c, mod_w, mod_b, norm_g, ffn_pre_w_in, ffn_pre_w_out, ffn_post_w_in, ffn_post_w_out, nsa_w_in, nsa_cmp_pos, nsa_cmp_w1, nsa_cmp_w2, nsa_w_out, ml_w_in, ml_conv_w, ml_conv_b, ml_gate_b, ml_mh_gain, ml_w_out):
    bsz, s, d = x.shape
    depth = mod_w.shape[0]
    mods = _mod_call(c, mod_w, mod_b).reshape(depth, bsz, N_MOD, d)
    tables = _nsa_tables(s)
    pre_in, pre_out = ffn_pre_w_in.astype(BF16), ffn_pre_w_out.astype(BF16)
    post_in, post_out = ffn_post_w_in.astype(BF16), ffn_post_w_out.astype(BF16)
    nsa_in, nsa_out = nsa_w_in.astype(BF16), nsa_w_out.astype(BF16)
    nsa_w1, nsa_w2 = nsa_cmp_w1.astype(BF16), nsa_cmp_w2.astype(BF16)
    ml_in, ml_out = ml_w_in.astype(BF16), ml_w_out.astype(BF16)
    for i in range(depth):
        mod, ng = mods[i], norm_g[i]
        x = _ffn_call(x, mod, ng, pre_in, pre_out, layer=i, mrow=0, grow=0)
        j = i // 2
        if i % 2 == 0:
            x = _nsa_layer(x, mod, ng, nsa_in, nsa_cmp_pos, nsa_w1, nsa_w2, nsa_out, tables, layer=j)
        else:
            x = _mlstm_layer(x, mod, ng, ml_in, ml_conv_w[j], ml_conv_b[j], ml_gate_b[j], ml_mh_gain[j], ml_out,
                             layer=j)
        x = _ffn_call(x, mod, ng, post_in, post_out, layer=i, mrow=6, grow=4)
    return x
```
